```python
import jax
import jax.numpy as jnp
from jax import lax
import numpy as np

D_MODEL = 1024
BATCH = 8
SEQ = 4096
DEPTH = 2
DEC_BATCH = 32
DEC_SEQ = 1
PAST_LEN = 16384
PAGE_SIZE = 128

N_MIXERS = 2
N_NSA_LAYERS = (DEPTH + 1) // 2
N_GLA_LAYERS = DEPTH // 2

NSA_HEADS = 16
NSA_HEAD_DIM = D_MODEL // NSA_HEADS
NSA_KV_HEADS = 4
NSA_GROUP = NSA_HEADS // NSA_KV_HEADS
NSA_Q_COLS = NSA_HEADS * NSA_HEAD_DIM
NSA_KV_COLS = 2 * NSA_KV_HEADS * NSA_HEAD_DIM
NSA_GATE_OFF = NSA_Q_COLS + 3 * NSA_KV_COLS
NSA_PROJ = NSA_GATE_OFF + 3 * NSA_HEADS
CMP_BLK = 64
SEL_BLK = 64
N_SEL = 16
WINDOW = 512
WIN_Q_BLK = 128
SEL_Q_BLK = 32
ATTN_SCALE = NSA_HEAD_DIM ** -0.5
ROPE_THETA = 10000.0
NEG_INF = -1e30
SEL_FORCE = 1e4

GLA_HEADS = 4
GLA_DK = D_MODEL // 2
GLA_DV = D_MODEL
GLA_DK_HEAD = GLA_DK // GLA_HEADS
GLA_DV_HEAD = GLA_DV // GLA_HEADS
GLA_RANK = 16
GLA_TAU = 16.0
GLA_CHUNK = 32
GLA_SCALE = GLA_DK_HEAD ** -0.5
GLA_PROJ = 2 * GLA_DK + 2 * GLA_DV + GLA_RANK

D_FF = -(-8 * D_MODEL // (3 * 256)) * 256
DN_ALPHA = (2.0 * DEPTH) ** 0.25
DN_BETA = (8.0 * DEPTH) ** -0.25
LN_EPS = 1e-5

kernel_name = 'nsa_gla_hybrid_decode_step'


def layer_norm(x, g, b):
    xf = x.astype(jnp.float32)
    mu = jnp.mean(xf, axis=-1, keepdims=True)
    var = jnp.mean(jnp.square(xf - mu), axis=-1, keepdims=True)
    return ((xf - mu) * lax.rsqrt(var + LN_EPS) * g + b).astype(x.dtype)


def rope(x, pos):
    half = x.shape[-1] // 2
    inv = ROPE_THETA ** (-jnp.arange(half, dtype=jnp.float32) / half)
    ang = pos.astype(jnp.float32)[:, None] * inv[None, :]
    cos = jnp.cos(ang)[:, None, :]
    sin = jnp.sin(ang)[:, None, :]
    xf = x.astype(jnp.float32)
    x1, x2 = xf[..., :half], xf[..., half:]
    return jnp.concatenate([x1 * cos - x2 * sin, x2 * cos + x1 * sin], axis=-1).astype(x.dtype)


def masked_softmax(s, mask):
    s = jnp.where(mask, s.astype(jnp.float32), NEG_INF)
    p = jnp.where(mask, jnp.exp(s - jnp.max(s, axis=-1, keepdims=True)), 0.0)
    return p / jnp.maximum(jnp.sum(p, axis=-1, keepdims=True), 1e-30)


def swiglu(x, w_gu, w_down):
    gu = x @ w_gu
    return (jax.nn.silu(gu[..., :D_FF]) * gu[..., D_FF:]) @ w_down


def nsa_project(x, pos, w_in, b_gate):
    B, T, _ = x.shape
    G, R, D = NSA_KV_HEADS, NSA_GROUP, NSA_HEAD_DIM
    proj = x @ w_in
    q = proj[..., :NSA_Q_COLS].reshape(B, T, NSA_HEADS, D)

    def branch(i):
        lo = NSA_Q_COLS + i * NSA_KV_COLS
        return proj[..., lo:lo + NSA_KV_COLS].reshape(B, T, 2, G, D)

    cmp_kv = branch(0)
    slc_kv = branch(1)
    win_kv = branch(2)
    slc_kv = jnp.stack([rope(slc_kv[:, :, 0], pos), slc_kv[:, :, 1]], axis=2)
    win_kv = jnp.stack([rope(win_kv[:, :, 0], pos), win_kv[:, :, 1]], axis=2)
    gates = jax.nn.sigmoid((proj[..., NSA_GATE_OFF:] + b_gate).astype(jnp.float32)).reshape(B, T, 3, G, R)
    q_rot = rope(q, pos).reshape(B, T, G, R, D)
    return q.reshape(B, T, G, R, D), q_rot, cmp_kv, slc_kv, win_kv, gates


def compress_blocks(rows, pe, w1, w2):
    B, L = rows.shape[:2]
    nb = L // CMP_BLK
    blk = rows[:, :nb * CMP_BLK].reshape(B, nb, CMP_BLK, 2, NSA_KV_HEADS, NSA_HEAD_DIM)
    h = jax.nn.silu(jnp.einsum('bnlcgd,lcde->bncge', blk + pe[:, :, None, :], w1))
    return jnp.einsum('bncge,cef->bncgf', h, w2)


def cmp_attention(q, q_pos, ckv):
    nb = ckv.shape[1]
    s = jnp.einsum('btgrd,bngd->btgrn', q, ckv[:, :, 0]) * ATTN_SCALE
    blk_end = (jnp.arange(nb, dtype=jnp.int32) + 1) * CMP_BLK - 1
    mask = (blk_end[None, :] <= q_pos[:, None])[None, :, None, None, :]
    p = masked_softmax(s, mask)
    o = jnp.einsum('btgrn,bngd->btgrd', p.astype(ckv.dtype), ckv[:, :, 1])
    return o, jnp.sum(p, axis=3)


def select_blocks(imp, q_pos, nb_sel):
    nbc = imp.shape[-1]
    imp = jnp.pad(imp, ((0, 0), (0, 0), (0, 0), (0, nb_sel - nbc)))
    blk = jnp.arange(nb_sel, dtype=jnp.int32)[None, :]
    cur = (q_pos // SEL_BLK)[:, None]
    forced = (blk == 0) | (blk == cur) | (blk == cur - 1)
    valid = blk <= cur
    score = jnp.where(forced[None, :, None, :], SEL_FORCE,
                      jnp.where(valid[None, :, None, :], imp, -1.0))
    _, idx = lax.top_k(score, min(N_SEL, nb_sel))
    return idx, idx <= (q_pos // SEL_BLK)[None, :, None, None]


def sel_attention(q, q_pos, key_pos, sel_valid, kg, vg):
    B, T, G, R, D = q.shape
    kk = kg.reshape(B, T, G, -1, D)
    vv = vg.reshape(B, T, G, -1, D)
    s = jnp.einsum('btgrd,btgkd->btgrk', q, kk) * ATTN_SCALE
    mask = sel_valid[..., None] & (key_pos <= q_pos[None, :, None, None, None])
    p = masked_softmax(s, mask.reshape(B, T, G, 1, -1))
    return jnp.einsum('btgrk,btgkd->btgrd', p.astype(vv.dtype), vv)


def win_attention(q, q_pos, k, v, k_pos):
    s = jnp.einsum('btgrd,bkgd->btgrk', q, k) * ATTN_SCALE
    diff = q_pos[:, None] - k_pos[None, :]
    mask = ((diff >= 0) & (diff <= WINDOW) & (k_pos[None, :] >= 0))[None, :, None, None, :]
    p = masked_softmax(s, mask)
    return jnp.einsum('btgrk,bkgd->btgrd', p.astype(v.dtype), v)


def nsa_combine(o_cmp, o_slc, o_win, gates, w_out, dtype):
    B, T = o_cmp.shape[:2]
    o = (gates[:, :, 0, :, :, None] * o_cmp + gates[:, :, 1, :, :, None] * o_slc
         + gates[:, :, 2, :, :, None] * o_win)
    return o.reshape(B, T, NSA_Q_COLS).astype(dtype) @ w_out


def nsa_prompt(x, pos, w_in, b_gate, pe, w1, w2, w_out):
    q, q_rot, cmp_kv, slc_kv, win_kv, gates = nsa_project(x, pos, w_in, b_gate)
    B, T = x.shape[:2]
    G, R, D = NSA_KV_HEADS, NSA_GROUP, NSA_HEAD_DIM
    ckv = compress_blocks(cmp_kv, pe, w1, w2)
    o_cmp, imp = cmp_attention(q, pos, ckv)
    idx, valid = select_blocks(imp, pos, T // SEL_BLK)
    nq = T // SEL_Q_BLK
    b_e = jnp.arange(B)[:, None, None, None, None]
    g_e = jnp.arange(G)[None, None, :, None, None]

    def chunks(a):
        return a.reshape((B, nq, SEL_Q_BLK) + a.shape[2:]).swapaxes(0, 1)

    def sel_block(args):
        qb, pb, ib, vb = args
        key_pos = ib[..., None] * SEL_BLK + jnp.arange(SEL_BLK, dtype=jnp.int32)
        kg = slc_kv[b_e, key_pos, 0, g_e]
        vg = slc_kv[b_e, key_pos, 1, g_e]
        return sel_attention(qb, pb, key_pos, vb, kg, vg)

    o_slc = lax.map(sel_block, (chunks(q_rot), pos.reshape(nq, SEL_Q_BLK), chunks(idx), chunks(valid)))
    o_slc = o_slc.swapaxes(0, 1).reshape(B, T, G, R, D)
    win_pad = jnp.pad(win_kv, ((0, 0), (WINDOW, 0), (0, 0), (0, 0), (0, 0)))

    def win_block(j):
        start = j * WIN_Q_BLK
        qb = lax.dynamic_slice_in_dim(q_rot, start, WIN_Q_BLK, axis=1)
        kvb = lax.dynamic_slice_in_dim(win_pad, start, WINDOW + WIN_Q_BLK, axis=1)
        qp = start + jnp.arange(WIN_Q_BLK, dtype=jnp.int32)
        kp = start - WINDOW + jnp.arange(WINDOW + WIN_Q_BLK, dtype=jnp.int32)
        return win_attention(qb, qp, kvb[:, :, 0], kvb[:, :, 1], kp)

    o_win = lax.map(win_block, jnp.arange(T // WIN_Q_BLK, dtype=jnp.int32))
    o_win = o_win.swapaxes(0, 1).reshape(B, T, G, R, D)
    y = nsa_combine(o_cmp, o_slc, o_win, gates, w_out, x.dtype)
    return y, cmp_kv, slc_kv, win_kv[:, -min(WINDOW, T):]


def nsa_sample(x, pos, li, cache_cmp_kv, cache_slc_kv, cache_win_kv, page_table,
               w_in, b_gate, pe, w1, w2, w_out):
    q, q_rot, cmp_kv, slc_kv, win_kv, gates = nsa_project(x, pos, w_in, b_gate)
    B, T = x.shape[:2]
    G, D = NSA_KV_HEADS, NSA_HEAD_DIM
    past_len = page_table.shape[1] * PAGE_SIZE
    cmp_past = cache_cmp_kv[li, page_table].reshape(B, past_len, 2, G, D)
    ckv = jnp.concatenate([compress_blocks(cmp_past, pe, w1, w2),
                           compress_blocks(cmp_kv, pe, w1, w2)], axis=1)
    o_cmp, imp = cmp_attention(q, pos, ckv)
    idx, valid = select_blocks(imp, pos, -(-(past_len + T) // SEL_BLK))
    key_pos = idx[..., None] * SEL_BLK + jnp.arange(SEL_BLK, dtype=jnp.int32)
    b_e = jnp.arange(B)[:, None, None, None, None]
    g_e = jnp.arange(G)[None, None, :, None, None]
    kp = jnp.minimum(key_pos, past_len - 1)
    phys = page_table[b_e, kp // PAGE_SIZE]
    off = kp % PAGE_SIZE
    kn = jnp.clip(key_pos - past_len, 0, T - 1)
    from_past = (key_pos < past_len)[..., None]
    kg = jnp.where(from_past, cache_slc_kv[li, phys, off, 0, g_e], slc_kv[b_e, kn, 0, g_e])
    vg = jnp.where(from_past, cache_slc_kv[li, phys, off, 1, g_e], slc_kv[b_e, kn, 1, g_e])
    o_slc = sel_attention(q_rot, pos, key_pos, valid, kg, vg)
    wb = cache_win_kv.shape[2]
    buf = jnp.concatenate([cache_win_kv[li], win_kv], axis=1)
    k_pos = past_len - wb + jnp.arange(wb + T, dtype=jnp.int32)
    o_win = win_attention(q_rot, pos, buf[:, :, 0], buf[:, :, 1], k_pos)
    y = nsa_combine(o_cmp, o_slc, o_win, gates, w_out, x.dtype)
    return y, cmp_kv, slc_kv, buf[:, T:]


def gla_chunked(q, k, v, log_a, s0):
    B, T, H, DK = q.shape
    DV = v.shape[-1]
    C = GLA_CHUNK
    pad = (-T) % C

    def prep(a):
        a = jnp.pad(a.astype(jnp.float32), ((0, 0), (0, pad), (0, 0), (0, 0)))
        nc = a.shape[1] // C
        return a.reshape(B, nc, C, H, a.shape[-1]).transpose(1, 0, 3, 2, 4)

    qc, kc, vc, lac = prep(q), prep(k), prep(v), prep(log_a)
    lb = jnp.cumsum(lac, axis=3)
    lb_last = lb[:, :, :, -1:, :]
    q_i = qc * jnp.exp(lb) * GLA_SCALE
    k_i = kc * jnp.exp(-lb)
    k_s = kc * jnp.exp(lb_last - lb)
    a_last = jnp.exp(lb_last[:, :, :, 0, :])
    causal = jnp.tril(jnp.ones((C, C), dtype=bool))

    def step(S, inp):
        qi, ki, ks, vv, al = inp
        att = jnp.where(causal, jnp.einsum('bhtk,bhsk->bhts', qi, ki), 0.0)
        o = jnp.einsum('bhts,bhsv->bhtv', att, vv) + jnp.einsum('bhtk,bhkv->bhtv', qi, S)
        S = al[..., None] * S + jnp.einsum('bhsk,bhsv->bhkv', ks, vv)
        return S, o

    S, o = lax.scan(step, s0.astype(jnp.float32), (q_i, k_i, k_s, vc, a_last))
    o = o.transpose(1, 0, 3, 2, 4).reshape(B, -1, H, DV)[:, :T]
    return o, S


def gla_mixer(x, s0, w_in, w_gate2, b_gate2, norm_g, w_out):
    B, T, _ = x.shape
    H = GLA_HEADS
    proj = x @ w_in
    o1, o2, o3, o4 = GLA_DK, 2 * GLA_DK, 2 * GLA_DK + GLA_DV, 2 * GLA_DK + 2 * GLA_DV
    q = proj[..., :o1].reshape(B, T, H, GLA_DK_HEAD)
    k = proj[..., o1:o2].reshape(B, T, H, GLA_DK_HEAD)
    v = proj[..., o2:o3].reshape(B, T, H, GLA_DV_HEAD)
    r = proj[..., o3:o4]
    low = proj[..., o4:]
    log_a = jax.nn.log_sigmoid((low @ w_gate2 + b_gate2).astype(jnp.float32)) / GLA_TAU
    o, s_new = gla_chunked(q, k, v, log_a.reshape(B, T, H, GLA_DK_HEAD), s0)
    o = o * lax.rsqrt(jnp.mean(jnp.square(o), axis=-1, keepdims=True) + LN_EPS)
    o = o.reshape(B, T, GLA_DV) * norm_g * jax.nn.silu(r.astype(jnp.float32))
    return o.astype(x.dtype) @ w_out, s_new


def setup_inputs(seed: int = 0) -> dict:
    key = jax.random.key(seed)
    ks = jax.random.split(key, 24)
    f32 = jnp.float32
    n_pages = PAST_LEN // PAGE_SIZE
    n_used = DEC_BATCH * n_pages
    n_pool = (5 * n_used + 3) // 4
    win_buf = min(WINDOW, PAST_LEN)

    def nrm(k, shape, scale):
        return jax.random.normal(k, shape, f32) * scale

    page_table = jax.random.permutation(ks[0], n_pool)[:n_used].reshape(DEC_BATCH, n_pages).astype(jnp.int32)
    kv_shape = (N_NSA_LAYERS, n_pool, PAGE_SIZE, 2, NSA_KV_HEADS, NSA_HEAD_DIM)
    return {
        'x_prompt': nrm(ks[1], (BATCH, SEQ, D_MODEL), 1.0),
        'x_sample': nrm(ks[2], (DEC_BATCH, DEC_SEQ, D_MODEL), 1.0),
        'cache_cmp_kv': nrm(ks[3], kv_shape, 1.0),
        'cache_slc_kv': nrm(ks[4], kv_shape, 1.0),
        'cache_win_kv': nrm(ks[5], (N_NSA_LAYERS, DEC_BATCH, win_buf, 2, NSA_KV_HEADS, NSA_HEAD_DIM), 1.0),
        'state_gla': nrm(ks[6], (N_GLA_LAYERS, DEC_BATCH, GLA_HEADS, GLA_DK_HEAD, GLA_DV_HEAD), 1.0),
        'page_table': page_table,
        'nsa_w_in': nrm(ks[7], (N_NSA_LAYERS, D_MODEL, NSA_PROJ), D_MODEL ** -0.5),
        'nsa_b_gate': nrm(ks[8], (N_NSA_LAYERS, 3 * NSA_HEADS), 0.1),
        'nsa_pe_cmp': nrm(ks[9], (N_NSA_LAYERS, CMP_BLK, 2, NSA_HEAD_DIM), 0.5),
        'nsa_w_cmp1': nrm(ks[10], (N_NSA_LAYERS, CMP_BLK, 2, NSA_HEAD_DIM, NSA_HEAD_DIM), (CMP_BLK * NSA_HEAD_DIM) ** -0.5),
        'nsa_w_cmp2': nrm(ks[11], (N_NSA_LAYERS, 2, NSA_HEAD_DIM, NSA_HEAD_DIM), NSA_HEAD_DIM ** -0.5),
        'nsa_w_out': nrm(ks[12], (N_NSA_LAYERS, NSA_Q_COLS, D_MODEL), DN_BETA * NSA_Q_COLS ** -0.5),
        'gla_w_in': nrm(ks[13], (N_GLA_LAYERS, D_MODEL, GLA_PROJ), D_MODEL ** -0.5),
        'gla_w_gate2': nrm(ks[14], (N_GLA_LAYERS, GLA_RANK, GLA_DK), GLA_RANK ** -0.5),
        'gla_b_gate2': nrm(ks[15], (N_GLA_LAYERS, GLA_DK), 0.1),
        'gla_norm_g': 1.0 + nrm(ks[16], (N_GLA_LAYERS, GLA_DV), 0.1),
        'gla_w_out': nrm(ks[17], (N_GLA_LAYERS, GLA_DV, D_MODEL), DN_BETA * GLA_DV ** -0.5),
        'ffn_w_gu': nrm(ks[18], (DEPTH, D_MODEL, 2 * D_FF), D_MODEL ** -0.5),
        'ffn_w_down': nrm(ks[19], (DEPTH, D_FF, D_MODEL), DN_BETA * D_FF ** -0.5),
        'ln_g': 1.0 + nrm(ks[20], (DEPTH, 2, D_MODEL), 0.1),
        'ln_b': nrm(ks[21], (DEPTH, 2, D_MODEL), 0.1),
    }


def reference(x_prompt, x_sample, cache_cmp_kv, cache_slc_kv, cache_win_kv, state_gla, page_table,
              nsa_w_in, nsa_b_gate, nsa_pe_cmp, nsa_w_cmp1, nsa_w_cmp2, nsa_w_out,
              gla_w_in, gla_w_gate2, gla_b_gate2, gla_norm_g, gla_w_out,
              ffn_w_gu, ffn_w_down, ln_g, ln_b):
    past_len = page_table.shape[1] * PAGE_SIZE
    pos_p = jnp.arange(x_prompt.shape[1], dtype=jnp.int32)
    pos_s = past_len + jnp.arange(x_sample.shape[1], dtype=jnp.int32)
    cmp_p, slc_p, win_p, gla_p = [], [], [], []
    cmp_s, slc_s, win_s, gla_s = [], [], [], []
    yp, ys = x_prompt, x_sample
    for i in range(DEPTH):
        li = i // N_MIXERS
        if i % N_MIXERS == 0:
            nsa_w = (nsa_w_in[li], nsa_b_gate[li], nsa_pe_cmp[li], nsa_w_cmp1[li], nsa_w_cmp2[li], nsa_w_out[li])
            mp, ckp, skp, wkp = nsa_prompt(yp, pos_p, *nsa_w)
            ms, cks, sks, wks = nsa_sample(ys, pos_s, li, cache_cmp_kv, cache_slc_kv, cache_win_kv,
                                           page_table, *nsa_w)
            cmp_p.append(ckp); slc_p.append(skp); win_p.append(wkp)
            cmp_s.append(cks); slc_s.append(sks); win_s.append(wks)
        else:
            gla_w = (gla_w_in[li], gla_w_gate2[li], gla_b_gate2[li], gla_norm_g[li], gla_w_out[li])
            s0 = jnp.zeros((yp.shape[0], GLA_HEADS, GLA_DK_HEAD, GLA_DV_HEAD), jnp.float32)
            mp, sp = gla_mixer(yp, s0, *gla_w)
            ms, ss = gla_mixer(ys, state_gla[li], *gla_w)
            gla_p.append(sp); gla_s.append(ss)
        yp = layer_norm(DN_ALPHA * yp + mp, ln_g[i, 0], ln_b[i, 0])
        ys = layer_norm(DN_ALPHA * ys + ms, ln_g[i, 0], ln_b[i, 0])
        yp = layer_norm(DN_ALPHA * yp + swiglu(yp, ffn_w_gu[i], ffn_w_down[i]), ln_g[i, 1], ln_b[i, 1])
        ys = layer_norm(DN_ALPHA * ys + swiglu(ys, ffn_w_gu[i], ffn_w_down[i]), ln_g[i, 1], ln_b[i, 1])
    return (yp, ys, jnp.stack(cmp_p), jnp.stack(slc_p), jnp.stack(win_p), jnp.stack(gla_p),
            jnp.stack(cmp_s), jnp.stack(slc_s), jnp.stack(win_s), jnp.stack(gla_s))
```

```python
import functools

import jax
import jax.numpy as jnp
from jax import lax
from jax.experimental import pallas as pl
from jax.experimental.pallas import tpu as pltpu

PAGE = 128
NSA_H = 16
NSA_D = 64
NSA_G = 4
NSA_R = NSA_H // NSA_G
CMP_BLK = 64
SEL_BLK = 64
SEL_SHIFT = 6
R_SHIFT = 2
LANE_SHIFT = 7
N_SEL = 16
WINDOW = 512
ATTN_SCALE = NSA_D ** -0.5
ROPE_THETA = 10000.0
NEG_INF = -1e30
SEL_FORCE = 1e4

GLA_H = 4
GLA_RANK = 16
GLA_TAU = 16.0
GLA_CHUNK = 32
LN_EPS = 1e-5

LANES = 128
HALF = LANES // 2
VMEM_LIMIT = 56 * 1024 * 1024

F32 = jnp.float32
BF16 = jnp.bfloat16

NT_DIMS = (((1,), (1,)), ((), ()))
TN_DIMS = (((0,), (0,)), ((), ()))


def _params(sem):
    return pltpu.CompilerParams(dimension_semantics=sem, vmem_limit_bytes=VMEM_LIMIT)


def _dot(a, b):
    return jnp.dot(a, b, preferred_element_type=F32)


def _dot_nt(a, b):
    return lax.dot_general(a, b, NT_DIMS, preferred_element_type=F32)


def _dot_tn(a, b):
    return lax.dot_general(a, b, TN_DIMS, preferred_element_type=F32)


def _layer_norm(z, g, b):
    mu = jnp.mean(z, axis=-1, keepdims=True)
    zc = z - mu
    var = jnp.mean(zc * zc, axis=-1, keepdims=True)
    return zc * lax.rsqrt(var + LN_EPS) * g + b


def _silu(x):
    return x * jax.nn.sigmoid(x)


def _rope(x, cos_t, sin_lo, sin_hi):
    out = []
    for p in range(x.shape[1] // LANES):
        blk = x[:, p * LANES:(p + 1) * LANES]
        x_up = pltpu.roll(blk, LANES - NSA_D // 2, 1)
        x_dn = pltpu.roll(blk, NSA_D // 2, 1)
        out.append(blk * cos_t + x_up * sin_lo + x_dn * sin_hi)
    return out[0] if len(out) == 1 else jnp.concatenate(out, axis=1)


def _dup_heads(x):
    rows, c = x.shape
    lane = lax.broadcasted_iota(jnp.int32, (rows, LANES), 1)
    lo = lane < HALF
    out = []
    for p in range(c // LANES):
        blk = x[:, p * LANES:(p + 1) * LANES]
        sw = pltpu.roll(blk, HALF, 1)
        out.append(jnp.where(lo, blk, sw))
        out.append(jnp.where(lo, sw, blk))
    return jnp.concatenate(out, axis=1)


def _nsa_proj_kernel(x_ref, w_ref, bg_ref, cos_ref, slo_ref, shi_ref,
                     qraw_ref, qrot_ref, cmp_ref, slc_ref, win_ref,
                     skd_ref, svd_ref, wkd_ref, wvd_ref, gates_ref):
    qc = NSA_H * NSA_D
    kc = NSA_G * NSA_D
    xb = x_ref[...].astype(BF16)
    cos_t, slo, shi = cos_ref[...], slo_ref[...], shi_ref[...]

    q = _dot(xb, w_ref[:, 0:qc])
    qraw_ref[...] = (q * ATTN_SCALE).astype(BF16)
    qrot_ref[...] = (_rope(q, cos_t, slo, shi) * ATTN_SCALE).astype(BF16)

    cmp_ref[...] = _dot(xb, w_ref[:, qc:qc + 2 * kc])

    slc = _dot(xb, w_ref[:, qc + 2 * kc:qc + 4 * kc])
    sk = _rope(slc[:, :kc], cos_t, slo, shi)
    sv = slc[:, kc:]
    slc_ref[:, :kc] = sk
    slc_ref[:, kc:] = sv
    skd_ref[...] = _dup_heads(sk).astype(BF16)
    svd_ref[...] = _dup_heads(sv).astype(BF16)

    win = _dot(xb, w_ref[:, qc + 4 * kc:qc + 6 * kc])
    wk = _rope(win[:, :kc], cos_t, slo, shi)
    wv = win[:, kc:]
    win_ref[:, :kc] = wk
    win_ref[:, kc:] = wv
    wkd_ref[...] = _dup_heads(wk).astype(BF16)
    wvd_ref[...] = _dup_heads(wv).astype(BF16)

    gz = _dot(xb, w_ref[:, qc + 6 * kc:qc + 6 * kc + NSA_G * LANES]) + bg_ref[...]
    gates_ref[...] = jax.nn.sigmoid(gz)


def _nsa_proj(x, w, bg, cos_t, slo, shi, *, tm, seq_tiles, win_rows, win_map):
    n, d = x.shape
    qc = NSA_H * NSA_D
    kc2 = 2 * NSA_G * NSA_D
    kd = NSA_G * LANES
    row = lambda i: (i, 0)
    tab = lambda i: (i % seq_tiles, 0)
    const = lambda i: (0, 0)
    out_shape = (
        jax.ShapeDtypeStruct((n, qc), BF16), jax.ShapeDtypeStruct((n, qc), BF16),
        jax.ShapeDtypeStruct((n, kc2), F32), jax.ShapeDtypeStruct((n, kc2), F32),
        jax.ShapeDtypeStruct((win_rows, kc2), F32),
        jax.ShapeDtypeStruct((n, kd), BF16), jax.ShapeDtypeStruct((n, kd), BF16),
        jax.ShapeDtypeStruct((n, kd), BF16), jax.ShapeDtypeStruct((n, kd), BF16),
        jax.ShapeDtypeStruct((n, kd), F32),
    )
    out_specs = (
        pl.BlockSpec((tm, qc), row), pl.BlockSpec((tm, qc), row),
        pl.BlockSpec((tm, kc2), row), pl.BlockSpec((tm, kc2), row),
        pl.BlockSpec((tm, kc2), win_map),
        pl.BlockSpec((tm, kd), row), pl.BlockSpec((tm, kd), row),
        pl.BlockSpec((tm, kd), row), pl.BlockSpec((tm, kd), row),
        pl.BlockSpec((tm, kd), row),
    )
    return pl.pallas_call(
        _nsa_proj_kernel,
        grid=(n // tm,),
        in_specs=[pl.BlockSpec((tm, d), row), pl.BlockSpec(w.shape, const),
                  pl.BlockSpec(bg.shape, const),
                  pl.BlockSpec((tm, LANES), tab), pl.BlockSpec((tm, LANES), tab),
                  pl.BlockSpec((tm, LANES), tab)],
        out_specs=out_specs,
        out_shape=out_shape,
        compiler_params=_params(("arbitrary",)),
        name="nsa_proj",
    )(x, w, bg, cos_t, slo, shi)


def _compress_body(tok_refs, pe_ref, w1_ref, w2_ref, out_ref):
    m = tok_refs[0].shape[0] // CMP_BLK
    acc = jnp.zeros((m, w1_ref.shape[2]), F32)
    for l in range(CMP_BLK):
        xl = jnp.concatenate([r[pl.ds(l, m, stride=CMP_BLK), :] for r in tok_refs], axis=1)
        xl = xl + pe_ref[l:l + 1, :]
        acc = acc + _dot(xl.astype(BF16), w1_ref[l])
    h = _silu(acc)
    out_ref[...] = _dot(h.astype(BF16), w2_ref[...])


def _compress_prompt_kernel(tok0_ref, tok1_ref, pe_ref, w1_ref, w2_ref, out_ref):
    _compress_body((tok0_ref, tok1_ref), pe_ref, w1_ref, w2_ref, out_ref)


def _compress_prompt(cmp_kv, pe, w1, w2, *, batch, seq):
    kc = NSA_G * NSA_D
    nb = seq // CMP_BLK
    return pl.pallas_call(
        _compress_prompt_kernel,
        grid=(2, batch),
        in_specs=[pl.BlockSpec((seq, LANES), lambda c, b: (b, 2 * c)),
                  pl.BlockSpec((seq, LANES), lambda c, b: (b, 2 * c + 1)),
                  pl.BlockSpec((None, CMP_BLK, kc), lambda c, b: (c, 0, 0)),
                  pl.BlockSpec((None, CMP_BLK, kc, kc), lambda c, b: (c, 0, 0, 0)),
                  pl.BlockSpec((None, kc, 2 * kc), lambda c, b: (c, 0, 0))],
        out_specs=pl.BlockSpec((None, nb, 2 * kc), lambda c, b: (c, b, 0)),
        out_shape=jax.ShapeDtypeStruct((2, batch * nb, 2 * kc), F32),
        compiler_params=_params(("arbitrary", "arbitrary")),
        name="compress_prompt",
    )(cmp_kv, cmp_kv, pe, w1, w2)


def _compress_pages_kernel(pt_ref, *refs, pages):
    del pt_ref
    page_refs = refs[:pages]
    pe_ref, w1_ref, w2_ref, out_ref, buf_ref = refs[pages:]
    n_blk = buf_ref.shape[0]
    for k in range(pages):
        for p in range(n_blk):
            buf_ref[p, k * PAGE:(k + 1) * PAGE, :] = page_refs[k][:, p * LANES:(p + 1) * LANES]
    _compress_body([buf_ref.at[p] for p in range(n_blk)], pe_ref, w1_ref, w2_ref, out_ref)


def _compress_pages(cache, page_table, pe, w1, w2):
    kc = NSA_G * NSA_D
    bsz, n_pages = page_table.shape
    pages = min(64, n_pages)
    groups = n_pages // pages
    blocks = pages * PAGE // CMP_BLK

    def page_map(k):
        return lambda c, b, h, pt: (pt[b, h * pages + k], 0, c)

    in_specs = [pl.BlockSpec((None, PAGE, kc), page_map(k)) for k in range(pages)]
    in_specs += [pl.BlockSpec((None, CMP_BLK, kc), lambda c, b, h, pt: (c, 0, 0)),
                 pl.BlockSpec((None, CMP_BLK, kc, kc), lambda c, b, h, pt: (c, 0, 0, 0)),
                 pl.BlockSpec((None, kc, 2 * kc), lambda c, b, h, pt: (c, 0, 0))]
    grid_spec = pltpu.PrefetchScalarGridSpec(
        num_scalar_prefetch=1,
        grid=(2, bsz, groups),
        in_specs=in_specs,
        out_specs=pl.BlockSpec((None, blocks, 2 * kc), lambda c, b, h, pt: (c, b * groups + h, 0)),
        scratch_shapes=[pltpu.VMEM((kc // LANES, pages * PAGE, LANES), F32)],
    )
    return pl.pallas_call(
        functools.partial(_compress_pages_kernel, pages=pages),
        grid_spec=grid_spec,
        out_shape=jax.ShapeDtypeStruct((2, bsz * n_pages * PAGE // CMP_BLK, 2 * kc), F32),
        compiler_params=_params(("arbitrary", "arbitrary", "arbitrary")),
        name="compress_pages",
    )(page_table, *([cache] * pages), pe, w1, w2)


def _softmax_tile(carry, s, mk, vt):
    m, l, acc = carry
    sm = jnp.where(mk > 0.0, s, NEG_INF)
    m_new = jnp.maximum(m, jnp.max(sm, axis=-1, keepdims=True))
    p = jnp.exp(sm - m_new) * mk
    alpha = jnp.exp(m - m_new)
    l = alpha * l + jnp.sum(p, axis=-1, keepdims=True)
    acc = alpha * acc + _dot(p.astype(BF16), vt)
    return m_new, l, acc


def _nsa_attn_kernel(qraw_ref, qrot_ref, ck_ref, cv_ref, sk_ref, sv_ref, wk_ref, wv_ref,
                     gates_ref, out_ref, *, tq, kt, n_sel):
    nb = ck_ref.shape[0]
    rows = NSA_R * tq
    q0 = pl.program_id(2) * tq

    lane = lax.broadcasted_iota(jnp.int32, (tq, LANES), 1)
    lo = lane < HALF

    def stack(ref):
        qa = ref[:, 0:LANES].astype(F32)
        qb = ref[:, LANES:2 * LANES].astype(F32)
        parts = [jnp.where(lo, qa, 0.0), jnp.where(lo, 0.0, qa),
                 jnp.where(lo, qb, 0.0), jnp.where(lo, 0.0, qb)]
        return jnp.concatenate(parts, axis=0).astype(BF16)

    q_raw = stack(qraw_ref)
    q_rot = stack(qrot_ref)
    ck = ck_ref[...].astype(BF16)
    cv = cv_ref[...].astype(BF16)

    sc = _dot_nt(q_raw, ck)
    t_r = q0 + (lax.broadcasted_iota(jnp.int32, (rows, nb), 0) & (tq - 1))
    n_r = lax.broadcasted_iota(jnp.int32, (rows, nb), 1)
    mask_c = (n_r + 1) * CMP_BLK - 1 <= t_r
    sc = jnp.where(mask_c, sc, NEG_INF)
    pc = jnp.where(mask_c, jnp.exp(sc - jnp.max(sc, axis=-1, keepdims=True)), 0.0)
    pc = pc / jnp.maximum(jnp.sum(pc, axis=-1, keepdims=True), 1e-30)
    o_cmp = _dot(pc.astype(BF16), cv)

    st = _dot_nt(ck, q_raw)
    t_c = q0 + (lax.broadcasted_iota(jnp.int32, (nb, rows), 1) & (tq - 1))
    n_c = lax.broadcasted_iota(jnp.int32, (nb, rows), 0)
    mask_t = (n_c + 1) * CMP_BLK - 1 <= t_c
    st = jnp.where(mask_t, st, NEG_INF)
    pt = jnp.where(mask_t, jnp.exp(st - jnp.max(st, axis=0, keepdims=True)), 0.0)
    pt = pt / jnp.maximum(jnp.sum(pt, axis=0, keepdims=True), 1e-30)
    imp = pt[:, 0:tq]
    for r in range(1, NSA_R):
        imp = imp + pt[:, r * tq:(r + 1) * tq]
    blk = lax.broadcasted_iota(jnp.int32, (nb, tq), 0)
    cur = (q0 + lax.broadcasted_iota(jnp.int32, (nb, tq), 1)) >> SEL_SHIFT
    forced = (blk == 0) | (blk == cur) | (blk == cur - 1)
    valid = blk <= cur
    score = jnp.where(forced, SEL_FORCE, jnp.where(valid, imp, -1.0))
    rank = jnp.zeros((nb, tq), F32)
    for m in range(nb):
        sm = score[m:m + 1, :]
        beats = (sm > score) | ((sm == score) & (blk > m))
        rank = rank + jnp.where(beats, 1.0, 0.0)
    sel_t = jnp.where((rank < n_sel) & valid, 1.0, 0.0).astype(BF16)

    tq_pos = q0 + lax.broadcasted_iota(jnp.int32, (tq, kt), 0)
    k_off = lax.broadcasted_iota(jnp.int32, (tq, kt), 1)
    e_blk = lax.broadcasted_iota(jnp.int32, (nb, kt), 0)
    e_key = lax.broadcasted_iota(jnp.int32, (nb, kt), 1)

    init = (jnp.full((rows, 1), NEG_INF, F32), jnp.zeros((rows, 1), F32),
            jnp.zeros((rows, LANES), F32))

    def sel_step(j, carry):
        k0 = pl.multiple_of(j * kt, kt)
        s = _dot_nt(q_rot, sk_ref[pl.ds(k0, kt), :])
        expand = jnp.where(e_blk == ((k0 + e_key) >> SEL_SHIFT), 1.0, 0.0).astype(BF16)
        mk = _dot_tn(sel_t, expand)
        mk = jnp.where(k0 + k_off <= tq_pos, mk, 0.0)
        mk = jnp.concatenate([mk] * NSA_R, axis=0)
        return _softmax_tile(carry, s, mk, sv_ref[pl.ds(k0, kt), :])

    _, l_s, acc_s = lax.fori_loop(0, (q0 + tq + kt - 1) // kt, sel_step, init)

    def win_step(j, carry):
        k0 = pl.multiple_of(j * kt, kt)
        s = _dot_nt(q_rot, wk_ref[pl.ds(k0, kt), :])
        diff = tq_pos - (k0 + k_off)
        mk = jnp.where((diff >= 0) & (diff <= WINDOW), 1.0, 0.0)
        mk = jnp.concatenate([mk] * NSA_R, axis=0)
        return _softmax_tile(carry, s, mk, wv_ref[pl.ds(k0, kt), :])

    w_lo = jnp.maximum(q0 - WINDOW, 0) // kt
    _, l_w, acc_w = lax.fori_loop(w_lo, (q0 + tq + kt - 1) // kt, win_step, init)

    o_s = acc_s / jnp.maximum(l_s, 1e-30)
    o_w = acc_w / jnp.maximum(l_w, 1e-30)
    g = gates_ref[...]
    heads = []
    for r in range(NSA_R):
        sl = slice(r * tq, (r + 1) * tq)
        heads.append(g[:, r:r + 1] * o_cmp[sl]
                     + g[:, NSA_R + r:NSA_R + r + 1] * o_s[sl]
                     + g[:, 2 * NSA_R + r:2 * NSA_R + r + 1] * o_w[sl])
    out_ref[:, 0:LANES] = jnp.where(lo, heads[0], heads[1]).astype(BF16)
    out_ref[:, LANES:2 * LANES] = jnp.where(lo, heads[2], heads[3]).astype(BF16)


def _nsa_attn_prompt(qraw, qrot, ckv, skd, svd, wkd, wvd, gates, *, batch, seq):
    tq, kt = 128, 256
    nqt = seq // tq
    nb = seq // CMP_BLK
    gc = NSA_R * NSA_D
    q_map = lambda b, g, t: (b * nqt + t, g)
    kv_map = lambda b, g, t: (b, g)
    return pl.pallas_call(
        functools.partial(_nsa_attn_kernel, tq=tq, kt=kt, n_sel=min(N_SEL, seq // SEL_BLK)),
        grid=(batch, NSA_G, nqt),
        in_specs=[pl.BlockSpec((tq, gc), q_map), pl.BlockSpec((tq, gc), q_map),
                  pl.BlockSpec((None, nb, LANES), lambda b, g, t: (0, b, g)),
                  pl.BlockSpec((None, nb, LANES), lambda b, g, t: (1, b, g)),
                  pl.BlockSpec((seq, LANES), kv_map), pl.BlockSpec((seq, LANES), kv_map),
                  pl.BlockSpec((seq, LANES), kv_map), pl.BlockSpec((seq, LANES), kv_map),
                  pl.BlockSpec((tq, LANES), q_map)],
        out_specs=pl.BlockSpec((tq, gc), q_map),
        out_shape=jax.ShapeDtypeStruct((batch * seq, NSA_H * NSA_D), BF16),
        compiler_params=_params(("arbitrary", "arbitrary", "arbitrary")),
        name="nsa_attn_prompt",
    )(qraw, qrot, ckv, ckv, skd, svd, wkd, wvd, gates)


def _nsa_sample_select_kernel(q_ref, ck_ref, cv_ref, ocmp_ref, idx_ref, *, n_sel_past):
    nbp = ck_ref.shape[0]
    q = q_ref[...]
    ck = ck_ref[...].astype(BF16)
    cv = cv_ref[...].astype(BF16)
    s = _dot_nt(q, ck)
    p = jnp.exp(s - jnp.max(s, axis=-1, keepdims=True))
    p = p / jnp.maximum(jnp.sum(p, axis=-1, keepdims=True), 1e-30)
    ocmp_ref[...] = _dot(p.astype(BF16), cv)

    n_row = lax.broadcasted_iota(jnp.int32, (1, nbp), 1)
    forced = (n_row == 0) | (n_row == nbp - 1)
    rows = []
    for g in range(NSA_G):
        imp = p[NSA_R * g:NSA_R * g + 1, :]
        for r in range(1, NSA_R):
            imp = imp + p[NSA_R * g + r:NSA_R * g + r + 1, :]
        rows.append(jnp.where(forced, SEL_FORCE, imp))
    m_idx = lax.broadcasted_iota(jnp.int32, (nbp, nbp), 0)
    n_idx = lax.broadcasted_iota(jnp.int32, (nbp, nbp), 1)
    k_idx = lax.broadcasted_iota(jnp.int32, (N_SEL, nbp), 0).astype(F32)
    n_f = lax.broadcasted_iota(jnp.int32, (N_SEL, nbp), 1).astype(F32)
    for g in range(NSA_G):
        row = rows[g]
        col = jnp.sum(jnp.where(m_idx == n_idx, row, 0.0), axis=1, keepdims=True)
        beats = (col > row) | ((col == row) & (m_idx < n_idx))
        rank = jnp.sum(jnp.where(beats, 1.0, 0.0), axis=0, keepdims=True)
        onehot = jnp.where(rank == k_idx, n_f, 0.0)
        idx_ref[g] = jnp.sum(onehot, axis=1, keepdims=True).astype(jnp.int32)
    del n_sel_past


def _nsa_sample_select(q_dup, ckv, *, bsz, nbp):
    hd = NSA_G * LANES
    return pl.pallas_call(
        functools.partial(_nsa_sample_select_kernel, n_sel_past=N_SEL - 1),
        grid=(bsz,),
        in_specs=[pl.BlockSpec((None, NSA_H, hd), lambda b: (b, 0, 0)),
                  pl.BlockSpec((None, nbp, hd), lambda b: (0, b, 0)),
                  pl.BlockSpec((None, nbp, hd), lambda b: (1, b, 0))],
        out_specs=(pl.BlockSpec((None, NSA_H, hd), lambda b: (b, 0, 0)),
                   pl.BlockSpec((None, NSA_G, N_SEL, 1), lambda b: (b, 0, 0, 0))),
        out_shape=(jax.ShapeDtypeStruct((bsz, NSA_H, hd), F32),
                   jax.ShapeDtypeStruct((bsz, NSA_G, N_SEL, 1), jnp.int32)),
        compiler_params=_params(("arbitrary",)),
        name="nsa_sample_select",
    )(q_dup, ckv, ckv)


def _nsa_sample_attn_kernel(idx_ref, pt_ref, *refs, n_past):
    del idx_ref, pt_ref
    slabs = refs[:NSA_G * n_past]
    (q_ref, snew_ref, wcache_ref, wnew_ref, gates_ref, ocmp_ref,
     out_ref, wout_ref, kbuf_ref, wbuf_ref) = refs[NSA_G * n_past:]
    kc = NSA_G * NSA_D
    q = q_ref[...]
    head_row = lax.broadcasted_iota(jnp.int32, (NSA_H, kc), 0)
    head_lane = lax.broadcasted_iota(jnp.int32, (NSA_H, kc), 1)

    def attend(buf_ref, n_keys):
        kk = buf_ref[:, :kc].astype(BF16)
        vv = buf_ref[:, kc:].astype(BF16)
        s = _dot_nt(q, kk)
        ok = lax.broadcasted_iota(jnp.int32, s.shape, 1) < n_keys
        s = jnp.where(ok, s, NEG_INF)
        p = jnp.where(ok, jnp.exp(s - jnp.max(s, axis=-1, keepdims=True)), 0.0)
        p = p / jnp.maximum(jnp.sum(p, axis=-1, keepdims=True), 1e-30)
        return _dot(p.astype(BF16), vv)

    n_keys = n_past * SEL_BLK + 1
    kbuf_ref[n_past * SEL_BLK:, :] = jnp.zeros((kbuf_ref.shape[0] - n_past * SEL_BLK, 2 * kc), F32)
    kbuf_ref[n_past * SEL_BLK:n_keys, :] = snew_ref[...]
    o_sel = jnp.zeros((NSA_H, kc), F32)
    for g in range(NSA_G):
        for k in range(n_past):
            kbuf_ref[k * SEL_BLK:(k + 1) * SEL_BLK, :] = slabs[g * n_past + k][...]
        o_g = attend(kbuf_ref, n_keys)
        o_sel = jnp.where((head_row >> R_SHIFT) == g, o_g, o_sel)

    wb = wcache_ref.shape[0]
    wbuf_ref[0:wb, :] = wcache_ref[...]
    wbuf_ref[wb:wb + 1, :] = wnew_ref[...]
    wbuf_ref[wb + 1:, :] = jnp.zeros((wbuf_ref.shape[0] - wb - 1, 2 * kc), F32)
    o_win = attend(wbuf_ref, wb + 1)
    wout_ref[...] = wbuf_ref[1:wb + 1, :]

    gts = gates_ref[...]
    own_dup = (lax.broadcasted_iota(jnp.int32, (NSA_H, NSA_G * LANES), 1) >> LANE_SHIFT) \
        == (lax.broadcasted_iota(jnp.int32, (NSA_H, NSA_G * LANES), 0) >> R_SHIFT)
    own = (head_lane >> SEL_SHIFT) == (head_row >> R_SHIFT)
    fold_dup = jnp.where((lax.broadcasted_iota(jnp.int32, (NSA_G * LANES, NSA_D), 0) & (LANES - 1))
                         == lax.broadcasted_iota(jnp.int32, (NSA_G * LANES, NSA_D), 1), 1.0, 0.0)
    fold = jnp.where((lax.broadcasted_iota(jnp.int32, (kc, NSA_D), 0) & (NSA_D - 1))
                     == lax.broadcasted_iota(jnp.int32, (kc, NSA_D), 1), 1.0, 0.0)
    hi = lax.Precision.HIGHEST
    f_cmp = jnp.dot(jnp.where(own_dup, ocmp_ref[...], 0.0), fold_dup, precision=hi,
                    preferred_element_type=F32)
    f_sel = jnp.dot(jnp.where(own, o_sel, 0.0), fold, precision=hi, preferred_element_type=F32)
    f_win = jnp.dot(jnp.where(own, o_win, 0.0), fold, precision=hi, preferred_element_type=F32)
    out_ref[...] = gts[:, 0:1] * f_cmp + gts[:, 1:2] * f_sel + gts[:, 2:3] * f_win


def _nsa_sample_attn(idx, page_table, slc_cache, q_cmp, slc_new, win_cache, win_new, gates, ocmp):
    bsz = page_table.shape[0]
    n_past = idx.shape[1] // NSA_G
    kc = NSA_G * NSA_D
    wb = win_cache.shape[1]
    half = PAGE // SEL_BLK

    def slab_map(j):
        def f(b, idx_ref, pt_ref):
            n = idx_ref[b, j]
            return (pt_ref[b, n // half] * half + n % half, 0, 0)
        return f

    pad = lambda n: -(-n // LANES) * LANES
    in_specs = [pl.BlockSpec((None, SEL_BLK, 2 * kc), slab_map(j)) for j in range(NSA_G * n_past)]
    in_specs += [pl.BlockSpec((None, NSA_H, kc), lambda b, i, p: (b, 0, 0)),
                 pl.BlockSpec((None, 1, 2 * kc), lambda b, i, p: (b, 0, 0)),
                 pl.BlockSpec((None, wb, 2 * kc), lambda b, i, p: (b, 0, 0)),
                 pl.BlockSpec((None, 1, 2 * kc), lambda b, i, p: (b, 0, 0)),
                 pl.BlockSpec((None, NSA_H, 3), lambda b, i, p: (b, 0, 0)),
                 pl.BlockSpec((None, NSA_H, NSA_G * LANES), lambda b, i, p: (b, 0, 0))]
    grid_spec = pltpu.PrefetchScalarGridSpec(
        num_scalar_prefetch=2,
        grid=(bsz,),
        in_specs=in_specs,
        out_specs=(pl.BlockSpec((None, NSA_H, NSA_D), lambda b, i, p: (b, 0, 0)),
                   pl.BlockSpec((None, wb, 2 * kc), lambda b, i, p: (b, 0, 0))),
        scratch_shapes=[pltpu.VMEM((pad(n_past * SEL_BLK + 1), 2 * kc), F32),
                        pltpu.VMEM((pad(wb + 1), 2 * kc), F32)],
    )
    return pl.pallas_call(
        functools.partial(_nsa_sample_attn_kernel, n_past=n_past),
        grid_spec=grid_spec,
        out_shape=(jax.ShapeDtypeStruct((bsz, NSA_H, NSA_D), F32),
                   jax.ShapeDtypeStruct((bsz, wb, 2 * kc), F32)),
        compiler_params=_params(("arbitrary",)),
        name="nsa_sample_attn",
    )(idx, page_table, *([slc_cache] * (NSA_G * n_past)), q_cmp, slc_new, win_cache, win_new, gates, ocmp)


def _proj_ln_kernel(a_ref, w_ref, x_ref, g_ref, b_ref, out_ref, *, alpha):
    y = _dot(a_ref[...].astype(BF16), w_ref[...])
    out_ref[...] = _layer_norm(alpha * x_ref[...] + y, g_ref[...], b_ref[...])


def _gla_proj_ln_kernel(o_ref, rs_ref, ng_ref, w_ref, x_ref, g_ref, b_ref, out_ref, *, alpha):
    a = (o_ref[...] * ng_ref[...] * rs_ref[...]).astype(BF16)
    y = _dot(a, w_ref[...])
    out_ref[...] = _layer_norm(alpha * x_ref[...] + y, g_ref[...], b_ref[...])


def _proj_ln(a, w, x, g, b, *, tm, alpha):
    n, d = x.shape
    row = lambda i: (i, 0)
    const = lambda i: (0, 0)
    return pl.pallas_call(
        functools.partial(_proj_ln_kernel, alpha=alpha),
        grid=(n // tm,),
        in_specs=[pl.BlockSpec((tm, a.shape[1]), row), pl.BlockSpec(w.shape, const),
                  pl.BlockSpec((tm, d), row), pl.BlockSpec((1, d), const), pl.BlockSpec((1, d), const)],
        out_specs=pl.BlockSpec((tm, d), row),
        out_shape=jax.ShapeDtypeStruct((n, d), F32),
        compiler_params=_params(("arbitrary",)),
        name="proj_ln",
    )(a, w, x, g, b)


def _gla_proj_ln(o, rs, ng, w, x, g, b, *, tm, alpha):
    n, d = x.shape
    row = lambda i: (i, 0)
    const = lambda i: (0, 0)
    return pl.pallas_call(
        functools.partial(_gla_proj_ln_kernel, alpha=alpha),
        grid=(n // tm,),
        in_specs=[pl.BlockSpec((tm, o.shape[1]), row), pl.BlockSpec((tm, o.shape[1]), row),
                  pl.BlockSpec((1, o.shape[1]), const), pl.BlockSpec(w.shape, const),
                  pl.BlockSpec((tm, d), row), pl.BlockSpec((1, d), const), pl.BlockSpec((1, d), const)],
        out_specs=pl.BlockSpec((tm, d), row),
        out_shape=jax.ShapeDtypeStruct((n, d), F32),
        compiler_params=_params(("arbitrary",)),
        name="gla_proj_ln",
    )(o, rs, ng, w, x, g, b)


def _ffn_kernel(x_ref, wg_ref, wu_ref, wd_ref, g_ref, b_ref, out_ref, acc_ref, xb_ref, *, alpha):
    k = pl.program_id(1)

    @pl.when(k == 0)
    def _():
        xb_ref[...] = x_ref[...].astype(BF16)
        acc_ref[...] = jnp.zeros_like(acc_ref)

    xb = xb_ref[...]
    h = _silu(_dot(xb, wg_ref[...])) * _dot(xb, wu_ref[...])
    acc_ref[...] += _dot(h.astype(BF16), wd_ref[...])

    @pl.when(k == pl.num_programs(1) - 1)
    def _():
        out_ref[...] = _layer_norm(alpha * x_ref[...] + acc_ref[...], g_ref[...], b_ref[...])


def _ffn(x, w_gu, w_down, g, b, *, tm, alpha):
    n, d = x.shape
    d_ff = w_down.shape[0]
    splits = 2
    tf = d_ff // splits
    assert tf * splits == d_ff and tf % LANES == 0
    return pl.pallas_call(
        functools.partial(_ffn_kernel, alpha=alpha),
        grid=(n // tm, splits),
        in_specs=[pl.BlockSpec((tm, d), lambda i, k: (i, 0)),
                  pl.BlockSpec((d, tf), lambda i, k: (0, k)),
                  pl.BlockSpec((d, tf), lambda i, k: (0, splits + k)),
                  pl.BlockSpec((tf, d), lambda i, k: (k, 0)),
                  pl.BlockSpec((1, d), lambda i, k: (0, 0)),
                  pl.BlockSpec((1, d), lambda i, k: (0, 0))],
        out_specs=pl.BlockSpec((tm, d), lambda i, k: (i, 0)),
        out_shape=jax.ShapeDtypeStruct((n, d), F32),
        scratch_shapes=[pltpu.VMEM((tm, d), F32), pltpu.VMEM((tm, d), BF16)],
        compiler_params=_params(("arbitrary", "arbitrary")),
        name="ffn",
    )(x, w_gu, w_gu, w_down, g, b)


def _gla_log_decay(low, w2_ref, b2_ref):
    z = _dot(low.astype(BF16), w2_ref[...]) + b2_ref[...]
    return jax.nn.log_sigmoid(z) / GLA_TAU


def _gla_proj_prompt_kernel(x_ref, w_ref, w2_ref, b2_ref, qi_ref, ki_ref, ks_ref, v_ref, rs_ref,
                            al_ref, *, dk, dv, scale):
    tm = x_ref.shape[0]
    xb = x_ref[...].astype(BF16)
    q = _dot(xb, w_ref[:, 0:dk])
    k = _dot(xb, w_ref[:, dk:2 * dk])
    v_ref[...] = _dot(xb, w_ref[:, 2 * dk:2 * dk + dv]).astype(BF16)
    rs_ref[...] = _silu(_dot(xb, w_ref[:, 2 * dk + dv:2 * dk + 2 * dv]))
    low = _dot(xb, w_ref[:, 2 * dk + 2 * dv:])
    la = _gla_log_decay(low, w2_ref, b2_ref)

    rin = lax.broadcasted_iota(jnp.int32, (tm, dk), 0) & (GLA_CHUNK - 1)
    lb = la
    s = 1
    while s < GLA_CHUNK:
        lb = lb + jnp.where(rin >= s, pltpu.roll(lb, s, 0), 0.0)
        s *= 2
    lb3 = lb.reshape(tm // GLA_CHUNK, GLA_CHUNK, dk)
    last = lb3[:, GLA_CHUNK - 1:GLA_CHUNK, :]
    qi_ref[...] = (q * jnp.exp(lb) * scale).astype(BF16)
    ki_ref[...] = (k * jnp.exp(-lb)).astype(BF16)
    ks_ref[...] = (k * jnp.exp(last - lb3).reshape(tm, dk)).astype(BF16)
    al_ref[...] = jnp.exp(last.reshape(tm // GLA_CHUNK, dk))


def _gla_proj_prompt(x, w, w2, b2, *, tm, dk, dv):
    n, d = x.shape
    row = lambda i: (i, 0)
    const = lambda i: (0, 0)
    return pl.pallas_call(
        functools.partial(_gla_proj_prompt_kernel, dk=dk, dv=dv, scale=(dk // GLA_H) ** -0.5),
        grid=(n // tm,),
        in_specs=[pl.BlockSpec((tm, d), row), pl.BlockSpec(w.shape, const),
                  pl.BlockSpec(w2.shape, const), pl.BlockSpec(b2.shape, const)],
        out_specs=(pl.BlockSpec((tm, dk), row), pl.BlockSpec((tm, dk), row), pl.BlockSpec((tm, dk), row),
                   pl.BlockSpec((tm, dv), row), pl.BlockSpec((tm, dv), row),
                   pl.BlockSpec((tm // GLA_CHUNK, dk), row)),
        out_shape=(jax.ShapeDtypeStruct((n, dk), BF16), jax.ShapeDtypeStruct((n, dk), BF16),
                   jax.ShapeDtypeStruct((n, dk), BF16), jax.ShapeDtypeStruct((n, dv), BF16),
                   jax.ShapeDtypeStruct((n, dv), F32),
                   jax.ShapeDtypeStruct((n // GLA_CHUNK, dk), F32)),
        compiler_params=_params(("arbitrary",)),
        name="gla_proj_prompt",
    )(x, w, w2, b2)


def _gla_proj_sample_kernel(x_ref, w_ref, w2_ref, b2_ref, q_ref, k_ref, v_ref, rs_ref, la_ref, *, dk, dv):
    xb = x_ref[...].astype(BF16)
    q_ref[...] = _dot(xb, w_ref[:, 0:dk])
    k_ref[...] = _dot(xb, w_ref[:, dk:2 * dk])
    v_ref[...] = _dot(xb, w_ref[:, 2 * dk:2 * dk + dv])
    rs_ref[...] = _silu(_dot(xb, w_ref[:, 2 * dk + dv:2 * dk + 2 * dv]))
    low = _dot(xb, w_ref[:, 2 * dk + 2 * dv:])
    la_ref[...] = _gla_log_decay(low, w2_ref, b2_ref)


def _gla_proj_sample(x, w, w2, b2, *, dk, dv):
    n, d = x.shape
    full = lambda s: pl.BlockSpec(s, lambda i: (0,) * len(s))
    return pl.pallas_call(
        functools.partial(_gla_proj_sample_kernel, dk=dk, dv=dv),
        grid=(1,),
        in_specs=[full((n, d)), full(w.shape), full(w2.shape), full(b2.shape)],
        out_specs=(full((n, dk)), full((n, dk)), full((n, dv)), full((n, dv)), full((n, dk))),
        out_shape=(jax.ShapeDtypeStruct((n, dk), F32), jax.ShapeDtypeStruct((n, dk), F32),
                   jax.ShapeDtypeStruct((n, dv), F32), jax.ShapeDtypeStruct((n, dv), F32),
                   jax.ShapeDtypeStruct((n, dk), F32)),
        compiler_params=_params(("arbitrary",)),
        name="gla_proj_sample",
    )(x, w, w2, b2)


def _gla_rec_kernel(qi_ref, ki_ref, ks_ref, v_ref, al_ref, o_ref, st_out_ref, st_ref):
    t = pl.program_id(2)

    @pl.when(t == 0)
    def _():
        st_ref[...] = jnp.zeros_like(st_ref)

    c = GLA_CHUNK
    causal = lax.broadcasted_iota(jnp.int32, (c, c), 0) >= lax.broadcasted_iota(jnp.int32, (c, c), 1)
    st = st_ref[...]
    for i in range(qi_ref.shape[0] // c):
        sl = slice(i * c, (i + 1) * c)
        qi, ki, ks, vv = qi_ref[sl, :], ki_ref[sl, :], ks_ref[sl, :], v_ref[sl, :]
        att = jnp.where(causal, _dot_nt(qi, ki), 0.0)
        o = _dot(att.astype(BF16), vv) + _dot_nt(qi, st.astype(BF16))
        st = al_ref[i:i + 1, :] * st + _dot_tn(vv, ks)
        o_ref[sl, :] = o * lax.rsqrt(jnp.mean(o * o, axis=-1, keepdims=True) + LN_EPS)
    st_ref[...] = st

    @pl.when(t == pl.num_programs(2) - 1)
    def _():
        st_out_ref[...] = st


def _gla_rec(qi, ki, ks, v, al, *, batch, seq, dkh, dvh):
    ct = 256
    nct = seq // ct
    cpt = ct // GLA_CHUNK
    tok = lambda b, h, t: (b * nct + t, h)
    return pl.pallas_call(
        _gla_rec_kernel,
        grid=(batch, GLA_H, nct),
        in_specs=[pl.BlockSpec((ct, dkh), tok), pl.BlockSpec((ct, dkh), tok), pl.BlockSpec((ct, dkh), tok),
                  pl.BlockSpec((ct, dvh), tok), pl.BlockSpec((cpt, dkh), tok)],
        out_specs=(pl.BlockSpec((ct, dvh), tok),
                   pl.BlockSpec((None, None, dvh, dkh), lambda b, h, t: (b, h, 0, 0))),
        out_shape=(jax.ShapeDtypeStruct((batch * seq, GLA_H * dvh), F32),
                   jax.ShapeDtypeStruct((batch, GLA_H, dvh, dkh), F32)),
        scratch_shapes=[pltpu.VMEM((dvh, dkh), F32)],
        compiler_params=_params(("arbitrary", "arbitrary", "arbitrary")),
        name="gla_rec",
    )(qi, ki, ks, v, al)


def _gla_step_kernel(q_ref, k_ref, la_ref, v_ref, s0_ref, o_ref, s_ref, *, scale):
    la = la_ref[...]
    a = jnp.exp(la)
    k = k_ref[...]
    qi = q_ref[...] * a * scale
    ki = k * jnp.exp(-la)
    v = v_ref[...]
    s0 = s0_ref[...]
    att = jnp.sum(qi * ki, axis=0, keepdims=True)
    o = att * v + jnp.sum(qi * s0, axis=0, keepdims=True)
    s_ref[...] = a * s0 + k * v
    o_ref[...] = o * lax.rsqrt(jnp.mean(o * o, axis=-1, keepdims=True) + LN_EPS)


def _gla_step(q, k, la, v, s0, *, scale):
    bsz, h, dkh, dvh = s0.shape
    col = pl.BlockSpec((None, None, dkh, 1), lambda b, i: (b, i, 0, 0))
    rowv = pl.BlockSpec((None, None, 1, dvh), lambda b, i: (b, i, 0, 0))
    mat = pl.BlockSpec((None, None, dkh, dvh), lambda b, i: (b, i, 0, 0))
    return pl.pallas_call(
        functools.partial(_gla_step_kernel, scale=scale),
        grid=(bsz, h),
        in_specs=[col, col, col, rowv, mat],
        out_specs=(rowv, mat),
        out_shape=(jax.ShapeDtypeStruct((bsz, h, 1, dvh), F32),
                   jax.ShapeDtypeStruct((bsz, h, dkh, dvh), F32)),
        compiler_params=_params(("arbitrary", "arbitrary")),
        name="gla_step",
    )(q, k, la, v, s0)


def _rope_tables(pos):
    half = NSA_D // 2
    inv = ROPE_THETA ** (-jnp.arange(half, dtype=F32) / half)
    ang = pos.astype(F32)[:, None] * inv[None, :]
    cos, sin = jnp.cos(ang), jnp.sin(ang)
    zero = jnp.zeros_like(sin)
    cos_t = jnp.tile(cos, (1, LANES // half))
    sin_lo = jnp.tile(jnp.concatenate([-sin, zero], axis=1), (1, LANES // NSA_D))
    sin_hi = jnp.tile(jnp.concatenate([zero, sin], axis=1), (1, LANES // NSA_D))
    return cos_t, sin_lo, sin_hi


def _prep_nsa_weights(w_in, b_gate, pe, w1, w2):
    d = w_in.shape[0]
    qc = NSA_H * NSA_D
    kc = NSA_G * NSA_D
    gate_off = qc + 6 * kc
    wg = w_in[:, gate_off:].reshape(d, 3, NSA_G, NSA_R).transpose(0, 2, 1, 3).reshape(d, NSA_G, 3 * NSA_R)
    wg = jnp.pad(wg, ((0, 0), (0, 0), (0, LANES - 3 * NSA_R))).reshape(d, NSA_G * LANES)
    bg = b_gate.reshape(3, NSA_G, NSA_R).transpose(1, 0, 2).reshape(NSA_G, 3 * NSA_R)
    bg = jnp.pad(bg, ((0, 0), (0, LANES - 3 * NSA_R))).reshape(1, NSA_G * LANES)
    w = jnp.concatenate([w_in[:, :gate_off], wg], axis=1).astype(BF16)
    eye = jnp.eye(NSA_G, dtype=F32)
    pe_x = jnp.tile(pe.transpose(1, 0, 2), (1, 1, NSA_G))
    w1_bd = jnp.einsum('lcde,gh->clgdhe', w1, eye).reshape(2, CMP_BLK, kc, kc).astype(BF16)
    w2_dup = jnp.concatenate([w2, w2], axis=-1)
    w2_bd = jnp.einsum('cef,gh->cgehf', w2_dup, eye).reshape(2, kc, 2 * kc).astype(BF16)
    return w, bg, pe_x, w1_bd, w2_bd


def kernel(x_prompt, x_sample, cache_cmp_kv, cache_slc_kv, cache_win_kv, state_gla, page_table,
           nsa_w_in, nsa_b_gate, nsa_pe_cmp, nsa_w_cmp1, nsa_w_cmp2, nsa_w_out,
           gla_w_in, gla_w_gate2, gla_b_gate2, gla_norm_g, gla_w_out,
           ffn_w_gu, ffn_w_down, ln_g, ln_b):
    batch, seq, d = x_prompt.shape
    bsz, dec_seq, _ = x_sample.shape
    depth = ffn_w_gu.shape[0]
    n_pool = cache_cmp_kv.shape[1]
    n_pages = page_table.shape[1]
    past_len = n_pages * PAGE
    kc = NSA_G * NSA_D
    assert dec_seq == 1 and depth == 2 and d == NSA_H * NSA_D
    assert seq % 512 == 0 and seq >= WINDOW and past_len % SEL_BLK == 0
    assert cache_win_kv.shape[2] == WINDOW and past_len // SEL_BLK >= N_SEL
    alpha = (2.0 * depth) ** 0.25
    tm = 512
    nbp = past_len // CMP_BLK

    xp = x_prompt.reshape(batch * seq, d)
    xs = x_sample.reshape(bsz, d)
    ln_g = ln_g.reshape(depth, 2, 1, d)
    ln_b = ln_b.reshape(depth, 2, 1, d)
    w_gu = ffn_w_gu.astype(BF16)
    w_dn = ffn_w_down.astype(BF16)

    w0, bg0, pe_x, w1_bd, w2_bd = _prep_nsa_weights(nsa_w_in[0], nsa_b_gate[0], nsa_pe_cmp[0],
                                                    nsa_w_cmp1[0], nsa_w_cmp2[0])
    w_out0 = nsa_w_out[0].astype(BF16)
    tabs_p = _rope_tables(jnp.arange(seq, dtype=jnp.int32))
    tabs_s = _rope_tables(jnp.full((bsz,), past_len, dtype=jnp.int32))

    seq_tiles = seq // tm
    (qraw, qrot, cmp_p, slc_p, win_p, skd, svd, wkd, wvd, gates_p) = _nsa_proj(
        xp, w0, bg0, *tabs_p, tm=tm, seq_tiles=seq_tiles, win_rows=batch * WINDOW,
        win_map=lambda i: (i // seq_tiles, 0))
    ckv_p = _compress_prompt(cmp_p, pe_x, w1_bd, w2_bd, batch=batch, seq=seq)
    o_p = _nsa_attn_prompt(qraw, qrot, ckv_p, skd, svd, wkd, wvd, gates_p, batch=batch, seq=seq)
    yp = _proj_ln(o_p, w_out0, xp, ln_g[0, 0], ln_b[0, 0], tm=tm, alpha=alpha)

    (qraw_s, qrot_s, cmp_s, slc_s, win_s, _, _, _, _, gates_s) = _nsa_proj(
        xs, w0, bg0, *tabs_s, tm=bsz, seq_tiles=1, win_rows=bsz, win_map=lambda i: (i, 0))
    ckv_s = _compress_pages(cache_cmp_kv[0].reshape(n_pool, PAGE, 2 * kc), page_table, pe_x, w1_bd, w2_bd)
    eye_g = jnp.eye(NSA_G, dtype=BF16)
    q4 = qraw_s.reshape(bsz, NSA_G, NSA_R, NSA_D)
    q_dup = jnp.einsum('bgrd,gh->bgrhd', q4, eye_g)
    q_dup = jnp.pad(q_dup, ((0, 0),) * 4 + ((0, NSA_D),)).reshape(bsz, NSA_H, NSA_G * LANES)
    q4r = qrot_s.reshape(bsz, NSA_G, NSA_R, NSA_D)
    q_cmp = jnp.einsum('bgrd,gh->bgrhd', q4r, eye_g).reshape(bsz, NSA_H, kc)
    ocmp_s, idx4 = _nsa_sample_select(q_dup, ckv_s, bsz=bsz, nbp=nbp)
    idx = idx4[:, :, :N_SEL - 1, 0].reshape(bsz, NSA_G * (N_SEL - 1))
    gates_s3 = gates_s.reshape(bsz, NSA_G, LANES)[:, :, :3 * NSA_R].reshape(bsz, NSA_G, 3, NSA_R)
    gates_s3 = gates_s3.transpose(0, 1, 3, 2).reshape(bsz, NSA_H, 3)
    o_s, win_buf_s = _nsa_sample_attn(
        idx, page_table, cache_slc_kv[0].reshape(n_pool * (PAGE // SEL_BLK), SEL_BLK, 2 * kc),
        q_cmp, slc_s.reshape(bsz, 1, 2 * kc), cache_win_kv[0].reshape(bsz, WINDOW, 2 * kc),
        win_s.reshape(bsz, 1, 2 * kc), gates_s3, ocmp_s)
    ys = _proj_ln(o_s.reshape(bsz, d), w_out0, xs, ln_g[0, 0], ln_b[0, 0], tm=bsz, alpha=alpha)

    yp = _ffn(yp, w_gu[0], w_dn[0], ln_g[0, 1], ln_b[0, 1], tm=tm, alpha=alpha)
    ys = _ffn(ys, w_gu[0], w_dn[0], ln_g[0, 1], ln_b[0, 1], tm=bsz, alpha=alpha)

    dk = gla_w_gate2.shape[2]
    dv = gla_w_out.shape[1]
    dkh, dvh = dk // GLA_H, dv // GLA_H
    gw = jnp.pad(gla_w_in[0], ((0, 0), (0, LANES - GLA_RANK))).astype(BF16)
    gw2 = jnp.pad(gla_w_gate2[0], ((0, LANES - GLA_RANK), (0, 0))).astype(BF16)
    gb2 = gla_b_gate2[0].reshape(1, dk)
    g_out = gla_w_out[0].astype(BF16)
    ng = gla_norm_g[0].reshape(1, dv)

    qi, ki, ks, vb, rs_p, al = _gla_proj_prompt(yp, gw, gw2, gb2, tm=tm, dk=dk, dv=dv)
    on_p, st_p = _gla_rec(qi, ki, ks, vb, al, batch=batch, seq=seq, dkh=dkh, dvh=dvh)
    yp = _gla_proj_ln(on_p, rs_p, ng, g_out, yp, ln_g[1, 0], ln_b[1, 0], tm=tm, alpha=alpha)

    q_s, k_s, v_s, rs_s, la_s = _gla_proj_sample(ys, gw, gw2, gb2, dk=dk, dv=dv)
    colv = lambda a: a.reshape(bsz, GLA_H, dkh, 1)
    on_s, st_s = _gla_step(colv(q_s), colv(k_s), colv(la_s), v_s.reshape(bsz, GLA_H, 1, dvh),
                           state_gla[0], scale=dkh ** -0.5)
    ys = _gla_proj_ln(on_s.reshape(bsz, dv), rs_s, ng, g_out, ys, ln_g[1, 0], ln_b[1, 0], tm=bsz, alpha=alpha)

    yp = _ffn(yp, w_gu[1], w_dn[1], ln_g[1, 1], ln_b[1, 1], tm=tm, alpha=alpha)
    ys = _ffn(ys, w_gu[1], w_dn[1], ln_g[1, 1], ln_b[1, 1], tm=bsz, alpha=alpha)

    kv6 = lambda a, rows: a.reshape(1, rows[0], rows[1], 2, NSA_G, NSA_D)
    return (yp.reshape(batch, seq, d), ys.reshape(bsz, 1, d),
            kv6(cmp_p, (batch, seq)), kv6(slc_p, (batch, seq)), kv6(win_p, (batch, WINDOW)),
            jnp.swapaxes(st_p, 2, 3)[None],
            kv6(cmp_s, (bsz, 1)), kv6(slc_s, (bsz, 1)), kv6(win_buf_s, (bsz, WINDOW)),
            st_s[None])
```

```python
import functools

import jax
import jax.numpy as jnp
from jax import lax
from jax.experimental import pallas as pl
from jax.experimental.pallas import tpu as pltpu

PAGE = 128
NSA_H = 16
NSA_D = 64
NSA_G = 4
NSA_R = NSA_H // NSA_G
CMP_BLK = 64
SEL_BLK = 64
SEL_SHIFT = 6
R_SHIFT = 2
LANE_SHIFT = 7
N_SEL = 16
WINDOW = 512
ATTN_SCALE = NSA_D ** -0.5
ROPE_THETA = 10000.0
NEG_INF = -1e30
SEL_FORCE = 1e4

GLA_H = 4
GLA_RANK = 16
GLA_TAU = 16.0
GLA_CHUNK = 32
LN_EPS = 1e-5

LANES = 128
HALF = LANES // 2
VMEM_LIMIT = 56 * 1024 * 1024

F32 = jnp.float32
BF16 = jnp.bfloat16

NT_DIMS = (((1,), (1,)), ((), ()))
TN_DIMS = (((0,), (0,)), ((), ()))


def _params(sem):
    return pltpu.CompilerParams(dimension_semantics=sem, vmem_limit_bytes=VMEM_LIMIT)


def _dot(a, b):
    return jnp.dot(a, b, preferred_element_type=F32)


def _dot_nt(a, b):
    return lax.dot_general(a, b, NT_DIMS, preferred_element_type=F32)


def _dot_tn(a, b):
    return lax.dot_general(a, b, TN_DIMS, preferred_element_type=F32)


def _layer_norm(z, g, b):
    mu = jnp.mean(z, axis=-1, keepdims=True)
    zc = z - mu
    var = jnp.mean(zc * zc, axis=-1, keepdims=True)
    return zc * lax.rsqrt(var + LN_EPS) * g + b


def _silu(x):
    return x * jax.nn.sigmoid(x)


def _rope(x, cos_t, sin_lo, sin_hi):
    out = []
    for p in range(x.shape[1] // LANES):
        blk = x[:, p * LANES:(p + 1) * LANES]
        x_up = pltpu.roll(blk, LANES - NSA_D // 2, 1)
        x_dn = pltpu.roll(blk, NSA_D // 2, 1)
        out.append(blk * cos_t + x_up * sin_lo + x_dn * sin_hi)
    return out[0] if len(out) == 1 else jnp.concatenate(out, axis=1)


def _dup_heads(x):
    rows, c = x.shape
    lane = lax.broadcasted_iota(jnp.int32, (rows, LANES), 1)
    lo = lane < HALF
    out = []
    for p in range(c // LANES):
        blk = x[:, p * LANES:(p + 1) * LANES]
        sw = pltpu.roll(blk, HALF, 1)
        out.append(jnp.where(lo, blk, sw))
        out.append(jnp.where(lo, sw, blk))
    return jnp.concatenate(out, axis=1)


def _nsa_proj_kernel(x_ref, w_ref, bg_ref, cos_ref, slo_ref, shi_ref,
                     qraw_ref, qrot_ref, cmp_ref, slc_ref, win_ref,
                     skd_ref, svd_ref, wkd_ref, wvd_ref, gates_ref):
    qc = NSA_H * NSA_D
    kc = NSA_G * NSA_D
    xb = x_ref[...].astype(BF16)
    cos_t, slo, shi = cos_ref[...], slo_ref[...], shi_ref[...]

    q = _dot(xb, w_ref[:, 0:qc])
    qraw_ref[...] = (q * ATTN_SCALE).astype(BF16)
    qrot_ref[...] = (_rope(q, cos_t, slo, shi) * ATTN_SCALE).astype(BF16)

    cmp_ref[...] = _dot(xb, w_ref[:, qc:qc + 2 * kc])

    slc = _dot(xb, w_ref[:, qc + 2 * kc:qc + 4 * kc])
    sk = _rope(slc[:, :kc], cos_t, slo, shi)
    sv = slc[:, kc:]
    slc_ref[:, :kc] = sk
    slc_ref[:, kc:] = sv
    skd_ref[...] = _dup_heads(sk).astype(BF16)
    svd_ref[...] = _dup_heads(sv).astype(BF16)

    win = _dot(xb, w_ref[:, qc + 4 * kc:qc + 6 * kc])
    wk = _rope(win[:, :kc], cos_t, slo, shi)
    wv = win[:, kc:]
    win_ref[:, :kc] = wk
    win_ref[:, kc:] = wv
    wkd_ref[...] = _dup_heads(wk).astype(BF16)
    wvd_ref[...] = _dup_heads(wv).astype(BF16)

    gz = _dot(xb, w_ref[:, qc + 6 * kc:qc + 6 * kc + NSA_G * LANES]) + bg_ref[...]
    gates_ref[...] = jax.nn.sigmoid(gz)


def _nsa_proj(x, w, bg, cos_t, slo, shi, *, tm, seq_tiles, win_rows, win_map):
    n, d = x.shape
    qc = NSA_H * NSA_D
    kc2 = 2 * NSA_G * NSA_D
    kd = NSA_G * LANES
    row = lambda i: (i, 0)
    tab = lambda i: (i % seq_tiles, 0)
    const = lambda i: (0, 0)
    out_shape = (
        jax.ShapeDtypeStruct((n, qc), BF16), jax.ShapeDtypeStruct((n, qc), BF16),
        jax.ShapeDtypeStruct((n, kc2), F32), jax.ShapeDtypeStruct((n, kc2), F32),
        jax.ShapeDtypeStruct((win_rows, kc2), F32),
        jax.ShapeDtypeStruct((n, kd), BF16), jax.ShapeDtypeStruct((n, kd), BF16),
        jax.ShapeDtypeStruct((n, kd), BF16), jax.ShapeDtypeStruct((n, kd), BF16),
        jax.ShapeDtypeStruct((n, kd), F32),
    )
    out_specs = (
        pl.BlockSpec((tm, qc), row), pl.BlockSpec((tm, qc), row),
        pl.BlockSpec((tm, kc2), row), pl.BlockSpec((tm, kc2), row),
        pl.BlockSpec((tm, kc2), win_map),
        pl.BlockSpec((tm, kd), row), pl.BlockSpec((tm, kd), row),
        pl.BlockSpec((tm, kd), row), pl.BlockSpec((tm, kd), row),
        pl.BlockSpec((tm, kd), row),
    )
    return pl.pallas_call(
        _nsa_proj_kernel,
        grid=(n // tm,),
        in_specs=[pl.BlockSpec((tm, d), row), pl.BlockSpec(w.shape, const),
                  pl.BlockSpec(bg.shape, const),
                  pl.BlockSpec((tm, LANES), tab), pl.BlockSpec((tm, LANES), tab),
                  pl.BlockSpec((tm, LANES), tab)],
        out_specs=out_specs,
        out_shape=out_shape,
        compiler_params=_params(("arbitrary",)),
        name="nsa_proj",
    )(x, w, bg, cos_t, slo, shi)


def _compress_body(tok_refs, pe_ref, w1_ref, w2_ref, out_ref):
    m = tok_refs[0].shape[0] // CMP_BLK
    acc = jnp.zeros((m, w1_ref.shape[2]), F32)
    for l in range(CMP_BLK):
        xl = jnp.concatenate([r[pl.ds(l, m, stride=CMP_BLK), :] for r in tok_refs], axis=1)
        xl = xl + pe_ref[l:l + 1, :]
        acc = acc + _dot(xl.astype(BF16), w1_ref[l])
    h = _silu(acc)
    out_ref[...] = _dot(h.astype(BF16), w2_ref[...])


def _compress_prompt_kernel(tok0_ref, tok1_ref, pe_ref, w1_ref, w2_ref, out_ref):
    _compress_body((tok0_ref, tok1_ref), pe_ref, w1_ref, w2_ref, out_ref)


def _compress_prompt(cmp_kv, pe, w1, w2, *, batch, seq):
    kc = NSA_G * NSA_D
    nb = seq // CMP_BLK
    return pl.pallas_call(
        _compress_prompt_kernel,
        grid=(2, batch),
        in_specs=[pl.BlockSpec((seq, LANES), lambda c, b: (b, 2 * c)),
                  pl.BlockSpec((seq, LANES), lambda c, b: (b, 2 * c + 1)),
                  pl.BlockSpec((None, CMP_BLK, kc), lambda c, b: (c, 0, 0)),
                  pl.BlockSpec((None, CMP_BLK, kc, kc), lambda c, b: (c, 0, 0, 0)),
                  pl.BlockSpec((None, kc, 2 * kc), lambda c, b: (c, 0, 0))],
        out_specs=pl.BlockSpec((None, nb, 2 * kc), lambda c, b: (c, b, 0)),
        out_shape=jax.ShapeDtypeStruct((2, batch * nb, 2 * kc), F32),
        compiler_params=_params(("arbitrary", "arbitrary")),
        name="compress_prompt",
    )(cmp_kv, cmp_kv, pe, w1, w2)


def _compress_pages_kernel(pt_ref, *refs, pages):
    del pt_ref
    page_refs = refs[:pages]
    pe_ref, w1_ref, w2_ref, out_ref, buf_ref = refs[pages:]
    rows_pp = NSA_G * NSA_D
    for k in range(pages):
        buf_ref[k * rows_pp:(k + 1) * rows_pp, :] = page_refs[k][...]
    m = pages * NSA_G
    acc = jnp.zeros((m, LANES), F32)
    for dd in range(NSA_D // 2):
        x = jnp.concatenate([buf_ref[pl.ds(2 * dd, m, stride=NSA_D), :],
                             buf_ref[pl.ds(2 * dd + 1, m, stride=NSA_D), :]], axis=1)
        x = x + pe_ref[dd:dd + 1, :]
        acc = acc + _dot(x.astype(BF16), w1_ref[dd])
    out_ref[...] = _dot(_silu(acc).astype(BF16), w2_ref[...])


def _compress_pages(cache, page_table, pe, w1, w2):
    bsz, n_pages = page_table.shape
    pages = min(64, n_pages)
    groups = n_pages // pages
    gd = NSA_G * NSA_D
    assert PAGE == 2 * CMP_BLK and PAGE == LANES

    def page_map(k):
        return lambda c, b, h, pt: (pt[b, h * pages + k], c, 0, 0)

    in_specs = [pl.BlockSpec((None, None, gd, PAGE), page_map(k)) for k in range(pages)]
    in_specs += [pl.BlockSpec((None,) + pe.shape[1:], lambda c, b, h, pt: (c, 0, 0)),
                 pl.BlockSpec((None,) + w1.shape[1:], lambda c, b, h, pt: (c, 0, 0, 0)),
                 pl.BlockSpec((None,) + w2.shape[1:], lambda c, b, h, pt: (c, 0, 0))]
    grid_spec = pltpu.PrefetchScalarGridSpec(
        num_scalar_prefetch=1,
        grid=(2, bsz, groups),
        in_specs=in_specs,
        out_specs=pl.BlockSpec((None, pages * NSA_G, LANES), lambda c, b, h, pt: (c, b * groups + h, 0)),
        scratch_shapes=[pltpu.VMEM((pages * gd, PAGE), F32)],
    )
    return pl.pallas_call(
        functools.partial(_compress_pages_kernel, pages=pages),
        grid_spec=grid_spec,
        out_shape=jax.ShapeDtypeStruct((2, bsz * n_pages * NSA_G, LANES), F32),
        compiler_params=_params(("arbitrary", "arbitrary", "arbitrary")),
        name="compress_pages",
    )(page_table, *([cache] * pages), pe, w1, w2)


def _attend_tile(carry, q_all, k_tile, v_tile, bias, tq):
    ms, ls, acc = carry
    s_all = _dot_nt(q_all, k_tile)
    new_ms, new_ls, ps, alphas = [], [], [], []
    for r in range(NSA_R):
        sb = s_all[r * tq:(r + 1) * tq] + bias
        m_new = jnp.maximum(ms[r], jnp.max(sb, axis=-1, keepdims=True))
        p = jnp.exp(sb - m_new)
        alpha = jnp.exp(ms[r] - m_new)
        new_ls.append(alpha * ls[r] + jnp.sum(p, axis=-1, keepdims=True))
        new_ms.append(m_new)
        alphas.append(jnp.broadcast_to(alpha, (tq, LANES)))
        ps.append(p.astype(BF16))
    acc = jnp.concatenate(alphas, axis=0) * acc + _dot(jnp.concatenate(ps, axis=0), v_tile)
    return tuple(new_ms), tuple(new_ls), acc


def _rank_blocks(score_ref, rank_ref, last_blk):
    nb, tq = score_ref.shape
    sub = lax.broadcasted_iota(jnp.int32, (8, tq), 0)
    rank_ref[...] = jnp.zeros((nb, tq), F32)
    for gi in range(nb // 8):
        @pl.when(gi * 8 <= last_blk)
        def _(gi=gi):
            rows8 = score_ref[gi * 8:(gi + 1) * 8, :]
            tops = [jnp.broadcast_to(rows8[u:u + 1, :], (8, tq)) for u in range(8)]
            for v in range(nb // 8):
                sv = score_ref[v * 8:(v + 1) * 8, :]
                cnt = jnp.zeros((8, tq), F32)
                for u in range(8):
                    if v > gi:
                        before = tops[u] >= sv
                    elif v < gi:
                        before = tops[u] > sv
                    else:
                        before = (tops[u] > sv) | ((tops[u] == sv) & (sub > u))
                    cnt = cnt + jnp.where(before, 1.0, 0.0)
                rank_ref[v * 8:(v + 1) * 8, :] += cnt


def _nsa_attn_kernel(qraw_ref, qrot_ref, ck_ref, cv_ref, sk_ref, sv_ref, wk_ref, wv_ref,
                     gates_ref, out_ref, score_ref, rank_ref, *, tq, kt, n_sel):
    nb = ck_ref.shape[0]
    rows = NSA_R * tq
    q0 = pl.program_id(2) * tq

    lane = lax.broadcasted_iota(jnp.int32, (tq, LANES), 1)
    lo = lane < HALF

    def stack(ref):
        qa = ref[:, 0:LANES].astype(F32)
        qb = ref[:, LANES:2 * LANES].astype(F32)
        parts = [jnp.where(lo, qa, 0.0), jnp.where(lo, 0.0, qa),
                 jnp.where(lo, qb, 0.0), jnp.where(lo, 0.0, qb)]
        return jnp.concatenate(parts, axis=0).astype(BF16)

    q_raw = stack(qraw_ref)
    q_rot = stack(qrot_ref)
    ck = ck_ref[...].astype(BF16)
    cv = cv_ref[...].astype(BF16)

    sc = _dot_nt(q_raw, ck)
    t_r = q0 + (lax.broadcasted_iota(jnp.int32, (rows, nb), 0) & (tq - 1))
    n_r = lax.broadcasted_iota(jnp.int32, (rows, nb), 1)
    mask_c = (n_r + 1) * CMP_BLK - 1 <= t_r
    sc = jnp.where(mask_c, sc, NEG_INF)
    pc = jnp.where(mask_c, jnp.exp(sc - jnp.max(sc, axis=-1, keepdims=True)), 0.0)
    pc = pc / jnp.maximum(jnp.sum(pc, axis=-1, keepdims=True), 1e-30)
    o_cmp = _dot(pc.astype(BF16), cv)

    st = _dot_nt(ck, q_raw)
    t_c = q0 + (lax.broadcasted_iota(jnp.int32, (nb, rows), 1) & (tq - 1))
    n_c = lax.broadcasted_iota(jnp.int32, (nb, rows), 0)
    mask_t = (n_c + 1) * CMP_BLK - 1 <= t_c
    st = jnp.where(mask_t, st, NEG_INF)
    pt = jnp.where(mask_t, jnp.exp(st - jnp.max(st, axis=0, keepdims=True)), 0.0)
    pt = pt / jnp.maximum(jnp.sum(pt, axis=0, keepdims=True), 1e-30)
    imp = pt[:, 0:tq]
    for r in range(1, NSA_R):
        imp = imp + pt[:, r * tq:(r + 1) * tq]
    blk = lax.broadcasted_iota(jnp.int32, (nb, tq), 0)
    cur = (q0 + lax.broadcasted_iota(jnp.int32, (nb, tq), 1)) >> SEL_SHIFT
    forced = (blk == 0) | (blk == cur) | (blk == cur - 1)
    valid = blk <= cur
    score = jnp.where(forced, SEL_FORCE, jnp.where(valid, imp, -1.0))
    score_ref[...] = score
    _rank_blocks(score_ref, rank_ref, (q0 + tq - 1) >> SEL_SHIFT)
    sel_t = jnp.where((rank_ref[...] < n_sel) & valid, 1.0, 0.0).astype(BF16)
    eye = jnp.where(lax.broadcasted_iota(jnp.int32, (nb, nb), 0)
                    == lax.broadcasted_iota(jnp.int32, (nb, nb), 1), 1.0, 0.0).astype(BF16)
    sel_q = _dot_tn(sel_t, eye).astype(BF16)

    tq_pos = q0 + lax.broadcasted_iota(jnp.int32, (tq, kt), 0)
    k_off = lax.broadcasted_iota(jnp.int32, (tq, kt), 1)
    e_blk = lax.broadcasted_iota(jnp.int32, (nb, kt), 0)
    e_key = lax.broadcasted_iota(jnp.int32, (nb, kt), 1)

    init = (tuple(jnp.full((tq, 1), NEG_INF, F32) for _ in range(NSA_R)),
            tuple(jnp.zeros((tq, 1), F32) for _ in range(NSA_R)),
            jnp.zeros((rows, LANES), F32))

    def sel_step(j, carry):
        k0 = pl.multiple_of(j * kt, kt)
        expand = jnp.where(e_blk == ((k0 + e_key) >> SEL_SHIFT), 1.0, 0.0).astype(BF16)
        picked = _dot(sel_q, expand)
        bias = jnp.where(k0 + k_off <= tq_pos, (picked - 1.0) * (-NEG_INF), NEG_INF)
        return _attend_tile(carry, q_rot, sk_ref[pl.ds(k0, kt), :], sv_ref[pl.ds(k0, kt), :], bias, tq)

    _, l_s, acc_s = lax.fori_loop(0, (q0 + tq + kt - 1) // kt, sel_step, init)

    def win_step(j, carry):
        k0 = pl.multiple_of(j * kt, kt)
        diff = tq_pos - (k0 + k_off)
        bias = jnp.where((diff >= 0) & (diff <= WINDOW), 0.0, NEG_INF)
        return _attend_tile(carry, q_rot, wk_ref[pl.ds(k0, kt), :], wv_ref[pl.ds(k0, kt), :], bias, tq)

    w_lo = jnp.maximum(q0 - WINDOW, 0) // kt
    _, l_w, acc_w = lax.fori_loop(w_lo, (q0 + tq + kt - 1) // kt, win_step, init)

    g = gates_ref[...]
    heads = []
    for r in range(NSA_R):
        sl = slice(r * tq, (r + 1) * tq)
        heads.append(g[:, r:r + 1] * o_cmp[sl]
                     + g[:, NSA_R + r:NSA_R + r + 1] * (acc_s[sl] / jnp.maximum(l_s[r], 1e-30))
                     + g[:, 2 * NSA_R + r:2 * NSA_R + r + 1] * (acc_w[sl] / jnp.maximum(l_w[r], 1e-30)))
    out_ref[:, 0:LANES] = jnp.where(lo, heads[0], heads[1]).astype(BF16)
    out_ref[:, LANES:2 * LANES] = jnp.where(lo, heads[2], heads[3]).astype(BF16)


def _nsa_attn_prompt(qraw, qrot, ckv, skd, svd, wkd, wvd, gates, *, batch, seq):
    tq, kt = 128, 256
    assert kt % tq == 0 and WINDOW % kt == 0 and seq % kt == 0
    nqt = seq // tq
    nb = seq // CMP_BLK
    gc = NSA_R * NSA_D
    q_map = lambda b, g, t: (b * nqt + t, g)
    kv_map = lambda b, g, t: (b, g)
    return pl.pallas_call(
        functools.partial(_nsa_attn_kernel, tq=tq, kt=kt, n_sel=min(N_SEL, seq // SEL_BLK)),
        grid=(batch, NSA_G, nqt),
        in_specs=[pl.BlockSpec((tq, gc), q_map), pl.BlockSpec((tq, gc), q_map),
                  pl.BlockSpec((None, nb, LANES), lambda b, g, t: (0, b, g)),
                  pl.BlockSpec((None, nb, LANES), lambda b, g, t: (1, b, g)),
                  pl.BlockSpec((seq, LANES), kv_map), pl.BlockSpec((seq, LANES), kv_map),
                  pl.BlockSpec((seq, LANES), kv_map), pl.BlockSpec((seq, LANES), kv_map),
                  pl.BlockSpec((tq, LANES), q_map)],
        out_specs=pl.BlockSpec((tq, gc), q_map),
        out_shape=jax.ShapeDtypeStruct((batch * seq, NSA_H * NSA_D), BF16),
        scratch_shapes=[pltpu.VMEM((nb, tq), F32), pltpu.VMEM((nb, tq), F32)],
        compiler_params=_params(("arbitrary", "arbitrary", "arbitrary")),
        name="nsa_attn_prompt",
    )(qraw, qrot, ckv, ckv, skd, svd, wkd, wvd, gates)


def _nsa_sample_select_kernel(qe_ref, qo_ref, ck_ref, cv_ref, ocmp_ref, idx_ref):
    npg = ck_ref.shape[0] // NSA_G
    nbp = 2 * npg
    qe, qo = qe_ref[...], qo_ref[...]
    head = lax.broadcasted_iota(jnp.int32, (NSA_H, LANES), 0)

    def block_of(j):
        return jnp.where(j < npg, 2 * j, 2 * (j - npg) + 1)

    n_row = block_of(lax.broadcasted_iota(jnp.int32, (1, nbp), 1))
    n_col = block_of(lax.broadcasted_iota(jnp.int32, (nbp, 1), 0))
    forced = (n_row == 0) | (n_row == nbp - 1)
    diag = lax.broadcasted_iota(jnp.int32, (nbp, nbp), 0) == lax.broadcasted_iota(jnp.int32, (nbp, nbp), 1)
    k_idx = lax.broadcasted_iota(jnp.int32, (N_SEL, nbp), 0).astype(F32)
    n_f = jnp.broadcast_to(n_row, (N_SEL, nbp)).astype(F32)
    ocmp = jnp.zeros((NSA_H, LANES), F32)
    for g in range(NSA_G):
        ck = ck_ref[pl.ds(g, npg, stride=NSA_G), :].astype(BF16)
        cv = cv_ref[pl.ds(g, npg, stride=NSA_G), :].astype(BF16)
        s = jnp.concatenate([_dot_nt(qe, ck), _dot_nt(qo, ck)], axis=1)
        p = jnp.exp(s - jnp.max(s, axis=-1, keepdims=True))
        p = p / jnp.maximum(jnp.sum(p, axis=-1, keepdims=True), 1e-30)
        o_e = _dot(p[:, :npg].astype(BF16), cv)
        o_o = _dot(p[:, npg:].astype(BF16), cv)
        ocmp = jnp.where((head >> R_SHIFT) == g, o_e + pltpu.roll(o_o, HALF, 1), ocmp)

        imp = p[NSA_R * g:NSA_R * g + 1, :]
        for r in range(1, NSA_R):
            imp = imp + p[NSA_R * g + r:NSA_R * g + r + 1, :]
        row = jnp.where(forced, SEL_FORCE, imp)
        col = jnp.sum(jnp.where(diag, row, 0.0), axis=1, keepdims=True)
        beats = (col > row) | ((col == row) & (n_col < n_row))
        rank = jnp.sum(jnp.where(beats, 1.0, 0.0), axis=0, keepdims=True)
        onehot = jnp.where(rank == k_idx, n_f, 0.0)
        idx_ref[g] = jnp.sum(onehot, axis=1, keepdims=True).astype(jnp.int32)
    ocmp_ref[...] = ocmp


def _nsa_sample_select(q_e, q_o, ckv, *, bsz):
    rows = ckv.shape[1] // bsz
    return pl.pallas_call(
        _nsa_sample_select_kernel,
        grid=(bsz,),
        in_specs=[pl.BlockSpec((None, NSA_H, LANES), lambda b: (b, 0, 0)),
                  pl.BlockSpec((None, NSA_H, LANES), lambda b: (b, 0, 0)),
                  pl.BlockSpec((None, rows, LANES), lambda b: (0, b, 0)),
                  pl.BlockSpec((None, rows, LANES), lambda b: (1, b, 0))],
        out_specs=(pl.BlockSpec((None, NSA_H, LANES), lambda b: (b, 0, 0)),
                   pl.BlockSpec((None, NSA_G, N_SEL, 1), lambda b: (b, 0, 0, 0))),
        out_shape=(jax.ShapeDtypeStruct((bsz, NSA_H, LANES), F32),
                   jax.ShapeDtypeStruct((bsz, NSA_G, N_SEL, 1), jnp.int32)),
        compiler_params=_params(("arbitrary",)),
        name="nsa_sample_select",
    )(q_e, q_o, ckv, ckv)


def _nsa_sample_attn_kernel(idx_ref, pt_ref, *refs, n_past):
    del pt_ref
    slabs = refs[:NSA_G * n_past]
    (q64_ref, q_ref, snew_ref, wcache_ref, wnew_ref, gates_ref, ocmp_ref,
     out_ref, wout_ref, wbuf_ref) = refs[NSA_G * n_past:]
    kc = NSA_G * NSA_D
    b = pl.program_id(0)
    q64 = q64_ref[...]
    q = q_ref[...]
    head_row = lax.broadcasted_iota(jnp.int32, (NSA_H, kc), 0)
    head_lane = lax.broadcasted_iota(jnp.int32, (NSA_H, kc), 1)
    own = (head_lane >> SEL_SHIFT) == (head_row >> R_SHIFT)
    fold = jnp.where((lax.broadcasted_iota(jnp.int32, (kc, NSA_D), 0) & (NSA_D - 1))
                     == lax.broadcasted_iota(jnp.int32, (kc, NSA_D), 1), 1.0, 0.0)
    hi = lax.Precision.HIGHEST

    def rounded(x):
        return x.astype(BF16).astype(F32)

    snew = snew_ref[...]
    s_new = jnp.sum(q.astype(F32) * rounded(snew[:, :kc]), axis=-1, keepdims=True)
    v_new = jnp.dot(jnp.where(own, rounded(snew[:, kc:]), 0.0), fold, precision=hi,
                    preferred_element_type=F32)
    half_of_lane = lax.broadcasted_iota(jnp.int32, (1, PAGE), 1) >> SEL_SHIFT
    row64 = lax.broadcasted_iota(jnp.int32, (NSA_H, NSA_D), 0)
    f_sel = jnp.zeros((NSA_H, NSA_D), F32)
    for g in range(NSA_G):
        mine = slabs[g * n_past:(g + 1) * n_past]
        k_cat = jnp.concatenate([r[0] for r in mine], axis=1).astype(BF16)
        v_cat = jnp.concatenate([r[1] for r in mine], axis=1).astype(BF16)
        bias = jnp.concatenate(
            [jnp.where(half_of_lane == (idx_ref[b, g * n_past + k] & 1), 0.0, NEG_INF)
             for k in range(n_past)], axis=1)
        sb = _dot(q64, k_cat) + bias
        m = jnp.maximum(jnp.max(sb, axis=-1, keepdims=True), s_new)
        p = jnp.exp(sb - m)
        p_new = jnp.exp(s_new - m)
        l = jnp.sum(p, axis=-1, keepdims=True) + p_new
        o = _dot_nt(p.astype(BF16), v_cat) + rounded(p_new) * v_new
        f_sel = jnp.where((row64 >> R_SHIFT) == g, o / jnp.maximum(l, 1e-30), f_sel)

    wb = wcache_ref.shape[0]
    wbuf_ref[0:wb, :] = wcache_ref[...]
    wbuf_ref[wb:wb + 1, :] = wnew_ref[...]
    wbuf_ref[wb + 1:, :] = jnp.zeros((wbuf_ref.shape[0] - wb - 1, 2 * kc), F32)
    s = _dot_nt(q, wbuf_ref[:, :kc].astype(BF16))
    ok = lax.broadcasted_iota(jnp.int32, s.shape, 1) < wb + 1
    s = jnp.where(ok, s, NEG_INF)
    p = jnp.where(ok, jnp.exp(s - jnp.max(s, axis=-1, keepdims=True)), 0.0)
    p = p / jnp.maximum(jnp.sum(p, axis=-1, keepdims=True), 1e-30)
    o_win = _dot(p.astype(BF16), wbuf_ref[:, kc:].astype(BF16))
    f_win = jnp.dot(jnp.where(own, o_win, 0.0), fold, precision=hi, preferred_element_type=F32)
    wout_ref[...] = wbuf_ref[1:wb + 1, :]

    gts = gates_ref[...]
    out_ref[...] = (gts[:, 0:1] * ocmp_ref[:, 0:NSA_D] + gts[:, 1:2] * f_sel + gts[:, 2:3] * f_win)


def _nsa_sample_attn(idx, page_table, slc_cache, q64, q_cmp, slc_new, win_cache, win_new, gates, ocmp):
    bsz = page_table.shape[0]
    n_past = idx.shape[1] // NSA_G
    kc = NSA_G * NSA_D
    wb = win_cache.shape[1]
    assert PAGE == 2 * SEL_BLK

    def slab_map(j):
        def f(b, idx_ref, pt_ref):
            return (pt_ref[b, idx_ref[b, j] // 2], 0, j // n_past, 0, 0)
        return f

    pad = lambda n: -(-n // LANES) * LANES
    in_specs = [pl.BlockSpec((None, 2, None, NSA_D, PAGE), slab_map(j)) for j in range(NSA_G * n_past)]
    in_specs += [pl.BlockSpec((None, NSA_H, NSA_D), lambda b, i, p: (b, 0, 0)),
                 pl.BlockSpec((None, NSA_H, kc), lambda b, i, p: (b, 0, 0)),
                 pl.BlockSpec((None, 1, 2 * kc), lambda b, i, p: (b, 0, 0)),
                 pl.BlockSpec((None, wb, 2 * kc), lambda b, i, p: (b, 0, 0)),
                 pl.BlockSpec((None, 1, 2 * kc), lambda b, i, p: (b, 0, 0)),
                 pl.BlockSpec((None, NSA_H, 3), lambda b, i, p: (b, 0, 0)),
                 pl.BlockSpec((None, NSA_H, LANES), lambda b, i, p: (b, 0, 0))]
    grid_spec = pltpu.PrefetchScalarGridSpec(
        num_scalar_prefetch=2,
        grid=(bsz,),
        in_specs=in_specs,
        out_specs=(pl.BlockSpec((None, NSA_H, NSA_D), lambda b, i, p: (b, 0, 0)),
                   pl.BlockSpec((None, wb, 2 * kc), lambda b, i, p: (b, 0, 0))),
        scratch_shapes=[pltpu.VMEM((pad(wb + 1), 2 * kc), F32)],
    )
    return pl.pallas_call(
        functools.partial(_nsa_sample_attn_kernel, n_past=n_past),
        grid_spec=grid_spec,
        out_shape=(jax.ShapeDtypeStruct((bsz, NSA_H, NSA_D), F32),
                   jax.ShapeDtypeStruct((bsz, wb, 2 * kc), F32)),
        compiler_params=_params(("arbitrary",)),
        name="nsa_sample_attn",
    )(idx, page_table, *([slc_cache] * (NSA_G * n_past)), q64, q_cmp, slc_new, win_cache, win_new, gates, ocmp)


def _proj_ln_kernel(a_ref, w_ref, x_ref, g_ref, b_ref, out_ref, *, alpha):
    y = _dot(a_ref[...].astype(BF16), w_ref[...])
    out_ref[...] = _layer_norm(alpha * x_ref[...] + y, g_ref[...], b_ref[...])


def _gla_proj_ln_kernel(o_ref, rs_ref, ng_ref, w_ref, x_ref, g_ref, b_ref, out_ref, *, alpha):
    a = (o_ref[...] * ng_ref[...] * rs_ref[...]).astype(BF16)
    y = _dot(a, w_ref[...])
    out_ref[...] = _layer_norm(alpha * x_ref[...] + y, g_ref[...], b_ref[...])


def _proj_ln(a, w, x, g, b, *, tm, alpha):
    n, d = x.shape
    row = lambda i: (i, 0)
    const = lambda i: (0, 0)
    return pl.pallas_call(
        functools.partial(_proj_ln_kernel, alpha=alpha),
        grid=(n // tm,),
        in_specs=[pl.BlockSpec((tm, a.shape[1]), row), pl.BlockSpec(w.shape, const),
                  pl.BlockSpec((tm, d), row), pl.BlockSpec((1, d), const), pl.BlockSpec((1, d), const)],
        out_specs=pl.BlockSpec((tm, d), row),
        out_shape=jax.ShapeDtypeStruct((n, d), F32),
        compiler_params=_params(("arbitrary",)),
        name="proj_ln",
    )(a, w, x, g, b)


def _gla_proj_ln(o, rs, ng, w, x, g, b, *, tm, alpha):
    n, d = x.shape
    row = lambda i: (i, 0)
    const = lambda i: (0, 0)
    return pl.pallas_call(
        functools.partial(_gla_proj_ln_kernel, alpha=alpha),
        grid=(n // tm,),
        in_specs=[pl.BlockSpec((tm, o.shape[1]), row), pl.BlockSpec((tm, o.shape[1]), row),
                  pl.BlockSpec((1, o.shape[1]), const), pl.BlockSpec(w.shape, const),
                  pl.BlockSpec((tm, d), row), pl.BlockSpec((1, d), const), pl.BlockSpec((1, d), const)],
        out_specs=pl.BlockSpec((tm, d), row),
        out_shape=jax.ShapeDtypeStruct((n, d), F32),
        compiler_params=_params(("arbitrary",)),
        name="gla_proj_ln",
    )(o, rs, ng, w, x, g, b)


def _ffn_kernel(x_ref, wg_ref, wu_ref, wd_ref, g_ref, b_ref, out_ref, acc_ref, xb_ref, *, alpha):
    k = pl.program_id(1)

    @pl.when(k == 0)
    def _():
        xb_ref[...] = x_ref[...].astype(BF16)
        acc_ref[...] = jnp.zeros_like(acc_ref)

    xb = xb_ref[...]
    h = _silu(_dot(xb, wg_ref[...])) * _dot(xb, wu_ref[...])
    acc_ref[...] += _dot(h.astype(BF16), wd_ref[...])

    @pl.when(k == pl.num_programs(1) - 1)
    def _():
        out_ref[...] = _layer_norm(alpha * x_ref[...] + acc_ref[...], g_ref[...], b_ref[...])


def _ffn(x, w_gu, w_down, g, b, *, tm, alpha):
    n, d = x.shape
    d_ff = w_down.shape[0]
    splits = 2
    tf = d_ff // splits
    assert tf * splits == d_ff and tf % LANES == 0
    return pl.pallas_call(
        functools.partial(_ffn_kernel, alpha=alpha),
        grid=(n // tm, splits),
        in_specs=[pl.BlockSpec((tm, d), lambda i, k: (i, 0)),
                  pl.BlockSpec((d, tf), lambda i, k: (0, k)),
                  pl.BlockSpec((d, tf), lambda i, k: (0, splits + k)),
                  pl.BlockSpec((tf, d), lambda i, k: (k, 0)),
                  pl.BlockSpec((1, d), lambda i, k: (0, 0)),
                  pl.BlockSpec((1, d), lambda i, k: (0, 0))],
        out_specs=pl.BlockSpec((tm, d), lambda i, k: (i, 0)),
        out_shape=jax.ShapeDtypeStruct((n, d), F32),
        scratch_shapes=[pltpu.VMEM((tm, d), F32), pltpu.VMEM((tm, d), BF16)],
        compiler_params=_params(("arbitrary", "arbitrary")),
        name="ffn",
    )(x, w_gu, w_gu, w_down, g, b)


def _gla_log_decay(low, w2_ref, b2_ref):
    z = _dot(low.astype(BF16), w2_ref[...]) + b2_ref[...]
    return jax.nn.log_sigmoid(z) / GLA_TAU


def _gla_proj_prompt_kernel(x_ref, w_ref, w2_ref, b2_ref, qi_ref, ki_ref, ks_ref, v_ref, rs_ref,
                            al_ref, *, dk, dv, scale):
    tm = x_ref.shape[0]
    xb = x_ref[...].astype(BF16)
    q = _dot(xb, w_ref[:, 0:dk])
    k = _dot(xb, w_ref[:, dk:2 * dk])
    v_ref[...] = _dot(xb, w_ref[:, 2 * dk:2 * dk + dv]).astype(BF16)
    rs_ref[...] = _silu(_dot(xb, w_ref[:, 2 * dk + dv:2 * dk + 2 * dv]))
    low = _dot(xb, w_ref[:, 2 * dk + 2 * dv:])
    la = _gla_log_decay(low, w2_ref, b2_ref)

    rin = lax.broadcasted_iota(jnp.int32, (tm, dk), 0) & (GLA_CHUNK - 1)
    lb = la
    s = 1
    while s < GLA_CHUNK:
        lb = lb + jnp.where(rin >= s, pltpu.roll(lb, s, 0), 0.0)
        s *= 2
    lb3 = lb.reshape(tm // GLA_CHUNK, GLA_CHUNK, dk)
    last = lb3[:, GLA_CHUNK - 1:GLA_CHUNK, :]
    qi_ref[...] = (q * jnp.exp(lb) * scale).astype(BF16)
    ki_ref[...] = (k * jnp.exp(-lb)).astype(BF16)
    ks_ref[...] = (k * jnp.exp(last - lb3).reshape(tm, dk)).astype(BF16)
    al_ref[...] = jnp.exp(last.reshape(tm // GLA_CHUNK, dk))


def _gla_proj_prompt(x, w, w2, b2, *, tm, dk, dv):
    n, d = x.shape
    row = lambda i: (i, 0)
    const = lambda i: (0, 0)
    return pl.pallas_call(
        functools.partial(_gla_proj_prompt_kernel, dk=dk, dv=dv, scale=(dk // GLA_H) ** -0.5),
        grid=(n // tm,),
        in_specs=[pl.BlockSpec((tm, d), row), pl.BlockSpec(w.shape, const),
                  pl.BlockSpec(w2.shape, const), pl.BlockSpec(b2.shape, const)],
        out_specs=(pl.BlockSpec((tm, dk), row), pl.BlockSpec((tm, dk), row), pl.BlockSpec((tm, dk), row),
                   pl.BlockSpec((tm, dv), row), pl.BlockSpec((tm, dv), row),
                   pl.BlockSpec((tm // GLA_CHUNK, dk), row)),
        out_shape=(jax.ShapeDtypeStruct((n, dk), BF16), jax.ShapeDtypeStruct((n, dk), BF16),
                   jax.ShapeDtypeStruct((n, dk), BF16), jax.ShapeDtypeStruct((n, dv), BF16),
                   jax.ShapeDtypeStruct((n, dv), F32),
                   jax.ShapeDtypeStruct((n // GLA_CHUNK, dk), F32)),
        compiler_params=_params(("arbitrary",)),
        name="gla_proj_prompt",
    )(x, w, w2, b2)


def _gla_proj_sample_kernel(x_ref, w_ref, w2_ref, b2_ref, q_ref, k_ref, v_ref, rs_ref, la_ref, *, dk, dv):
    xb = x_ref[...].astype(BF16)
    q_ref[...] = _dot(xb, w_ref[:, 0:dk])
    k_ref[...] = _dot(xb, w_ref[:, dk:2 * dk])
    v_ref[...] = _dot(xb, w_ref[:, 2 * dk:2 * dk + dv])
    rs_ref[...] = _silu(_dot(xb, w_ref[:, 2 * dk + dv:2 * dk + 2 * dv]))
    low = _dot(xb, w_ref[:, 2 * dk + 2 * dv:])
    la_ref[...] = _gla_log_decay(low, w2_ref, b2_ref)


def _gla_proj_sample(x, w, w2, b2, *, dk, dv):
    n, d = x.shape
    full = lambda s: pl.BlockSpec(s, lambda i: (0,) * len(s))
    return pl.pallas_call(
        functools.partial(_gla_proj_sample_kernel, dk=dk, dv=dv),
        grid=(1,),
        in_specs=[full((n, d)), full(w.shape), full(w2.shape), full(b2.shape)],
        out_specs=(full((n, dk)), full((n, dk)), full((n, dv)), full((n, dv)), full((n, dk))),
        out_shape=(jax.ShapeDtypeStruct((n, dk), F32), jax.ShapeDtypeStruct((n, dk), F32),
                   jax.ShapeDtypeStruct((n, dv), F32), jax.ShapeDtypeStruct((n, dv), F32),
                   jax.ShapeDtypeStruct((n, dk), F32)),
        compiler_params=_params(("arbitrary",)),
        name="gla_proj_sample",
    )(x, w, w2, b2)


def _gla_rec_kernel(qi_ref, ki_ref, ks_ref, v_ref, al_ref, o_ref, st_out_ref, st_ref):
    t = pl.program_id(2)

    @pl.when(t == 0)
    def _():
        st_ref[...] = jnp.zeros_like(st_ref)

    c = GLA_CHUNK
    causal = lax.broadcasted_iota(jnp.int32, (c, c), 0) >= lax.broadcasted_iota(jnp.int32, (c, c), 1)
    st = st_ref[...]
    for i in range(qi_ref.shape[0] // c):
        sl = slice(i * c, (i + 1) * c)
        qi, ki, ks, vv = qi_ref[sl, :], ki_ref[sl, :], ks_ref[sl, :], v_ref[sl, :]
        att = jnp.where(causal, _dot_nt(qi, ki), 0.0)
        o = _dot(att.astype(BF16), vv) + _dot_nt(qi, st.astype(BF16))
        st = al_ref[i:i + 1, :] * st + _dot_tn(vv, ks)
        o_ref[sl, :] = o * lax.rsqrt(jnp.mean(o * o, axis=-1, keepdims=True) + LN_EPS)
    st_ref[...] = st

    @pl.when(t == pl.num_programs(2) - 1)
    def _():
        st_out_ref[...] = st


def _gla_rec(qi, ki, ks, v, al, *, batch, seq, dkh, dvh):
    ct = 256
    nct = seq // ct
    cpt = ct // GLA_CHUNK
    tok = lambda b, h, t: (b * nct + t, h)
    return pl.pallas_call(
        _gla_rec_kernel,
        grid=(batch, GLA_H, nct),
        in_specs=[pl.BlockSpec((ct, dkh), tok), pl.BlockSpec((ct, dkh), tok), pl.BlockSpec((ct, dkh), tok),
                  pl.BlockSpec((ct, dvh), tok), pl.BlockSpec((cpt, dkh), tok)],
        out_specs=(pl.BlockSpec((ct, dvh), tok),
                   pl.BlockSpec((None, None, dvh, dkh), lambda b, h, t: (b, h, 0, 0))),
        out_shape=(jax.ShapeDtypeStruct((batch * seq, GLA_H * dvh), F32),
                   jax.ShapeDtypeStruct((batch, GLA_H, dvh, dkh), F32)),
        scratch_shapes=[pltpu.VMEM((dvh, dkh), F32)],
        compiler_params=_params(("arbitrary", "arbitrary", "arbitrary")),
        name="gla_rec",
    )(qi, ki, ks, v, al)


def _gla_step_kernel(q_ref, k_ref, la_ref, v_ref, s0_ref, o_ref, s_ref, *, scale):
    la = la_ref[...]
    a = jnp.exp(la)
    k = k_ref[...]
    qi = q_ref[...] * a * scale
    ki = k * jnp.exp(-la)
    v = v_ref[...]
    s0 = s0_ref[...]
    att = jnp.sum(qi * ki, axis=0, keepdims=True)
    o = att * v + jnp.sum(qi * s0, axis=0, keepdims=True)
    s_ref[...] = a * s0 + k * v
    o_ref[...] = o * lax.rsqrt(jnp.mean(o * o, axis=-1, keepdims=True) + LN_EPS)


def _gla_step(q, k, la, v, s0, *, scale):
    bsz, h, dkh, dvh = s0.shape
    col = pl.BlockSpec((None, None, dkh, 1), lambda b, i: (b, i, 0, 0))
    rowv = pl.BlockSpec((None, None, 1, dvh), lambda b, i: (b, i, 0, 0))
    mat = pl.BlockSpec((None, None, dkh, dvh), lambda b, i: (b, i, 0, 0))
    return pl.pallas_call(
        functools.partial(_gla_step_kernel, scale=scale),
        grid=(bsz, h),
        in_specs=[col, col, col, rowv, mat],
        out_specs=(rowv, mat),
        out_shape=(jax.ShapeDtypeStruct((bsz, h, 1, dvh), F32),
                   jax.ShapeDtypeStruct((bsz, h, dkh, dvh), F32)),
        compiler_params=_params(("arbitrary", "arbitrary")),
        name="gla_step",
    )(q, k, la, v, s0)


def _rope_tables(pos):
    half = NSA_D // 2
    inv = ROPE_THETA ** (-jnp.arange(half, dtype=F32) / half)
    ang = pos.astype(F32)[:, None] * inv[None, :]
    cos, sin = jnp.cos(ang), jnp.sin(ang)
    zero = jnp.zeros_like(sin)
    cos_t = jnp.tile(cos, (1, LANES // half))
    sin_lo = jnp.tile(jnp.concatenate([-sin, zero], axis=1), (1, LANES // NSA_D))
    sin_hi = jnp.tile(jnp.concatenate([zero, sin], axis=1), (1, LANES // NSA_D))
    return cos_t, sin_lo, sin_hi


def _prep_nsa_weights(w_in, b_gate, pe, w1, w2):
    d = w_in.shape[0]
    qc = NSA_H * NSA_D
    kc = NSA_G * NSA_D
    gate_off = qc + 6 * kc
    wg = w_in[:, gate_off:].reshape(d, 3, NSA_G, NSA_R).transpose(0, 2, 1, 3).reshape(d, NSA_G, 3 * NSA_R)
    wg = jnp.pad(wg, ((0, 0), (0, 0), (0, LANES - 3 * NSA_R))).reshape(d, NSA_G * LANES)
    bg = b_gate.reshape(3, NSA_G, NSA_R).transpose(1, 0, 2).reshape(NSA_G, 3 * NSA_R)
    bg = jnp.pad(bg, ((0, 0), (0, LANES - 3 * NSA_R))).reshape(1, NSA_G * LANES)
    w = jnp.concatenate([w_in[:, :gate_off], wg], axis=1).astype(BF16)
    eye = jnp.eye(NSA_G, dtype=F32)
    pe_x = jnp.tile(pe.transpose(1, 0, 2), (1, 1, NSA_G))
    w1_bd = jnp.einsum('lcde,gh->clgdhe', w1, eye).reshape(2, CMP_BLK, kc, kc).astype(BF16)
    w2_dup = jnp.concatenate([w2, w2], axis=-1)
    w2_bd = jnp.einsum('cef,gh->cgehf', w2_dup, eye).reshape(2, kc, 2 * kc).astype(BF16)
    return w, bg, pe_x, w1_bd, w2_bd


def _prep_page_compress_weights(pe, w1, w2):
    nd = NSA_D // 2
    eye = jnp.eye(PAGE // CMP_BLK, dtype=F32)
    pe_n = jnp.broadcast_to(pe.transpose(1, 2, 0).reshape(2, nd, 2, 1, CMP_BLK),
                            (2, nd, 2, PAGE // CMP_BLK, CMP_BLK)).reshape(2, nd, 2 * PAGE)
    w1c = w1.transpose(1, 2, 0, 3).reshape(2, nd, 2, CMP_BLK, NSA_D)
    w1_n = jnp.einsum('cqple,hk->cqphlke', w1c, eye).reshape(2, nd, 2 * PAGE, PAGE).astype(BF16)
    w2_n = jnp.einsum('cef,hk->chekf', w2, eye).reshape(2, PAGE, PAGE).astype(BF16)
    return pe_n, w1_n, w2_n


def kernel(x_prompt, x_sample, cache_cmp_kv, cache_slc_kv, cache_win_kv, state_gla, page_table,
           nsa_w_in, nsa_b_gate, nsa_pe_cmp, nsa_w_cmp1, nsa_w_cmp2, nsa_w_out,
           gla_w_in, gla_w_gate2, gla_b_gate2, gla_norm_g, gla_w_out,
           ffn_w_gu, ffn_w_down, ln_g, ln_b):
    batch, seq, d = x_prompt.shape
    bsz, dec_seq, _ = x_sample.shape
    depth = ffn_w_gu.shape[0]
    n_pool = cache_cmp_kv.shape[1]
    n_pages = page_table.shape[1]
    past_len = n_pages * PAGE
    kc = NSA_G * NSA_D
    assert dec_seq == 1 and depth == 2 and d == NSA_H * NSA_D
    assert seq % 512 == 0 and seq >= WINDOW and past_len % SEL_BLK == 0
    assert cache_win_kv.shape[2] == WINDOW and past_len // SEL_BLK >= N_SEL
    alpha = (2.0 * depth) ** 0.25
    tm = 512
    nbp = past_len // CMP_BLK

    xp = x_prompt.reshape(batch * seq, d)
    xs = x_sample.reshape(bsz, d)
    ln_g = ln_g.reshape(depth, 2, 1, d)
    ln_b = ln_b.reshape(depth, 2, 1, d)
    w_gu = ffn_w_gu.astype(BF16)
    w_dn = ffn_w_down.astype(BF16)

    w0, bg0, pe_x, w1_bd, w2_bd = _prep_nsa_weights(nsa_w_in[0], nsa_b_gate[0], nsa_pe_cmp[0],
                                                    nsa_w_cmp1[0], nsa_w_cmp2[0])
    w_out0 = nsa_w_out[0].astype(BF16)
    tabs_p = _rope_tables(jnp.arange(seq, dtype=jnp.int32))
    tabs_s = _rope_tables(jnp.full((bsz,), past_len, dtype=jnp.int32))

    seq_tiles = seq // tm
    (qraw, qrot, cmp_p, slc_p, win_p, skd, svd, wkd, wvd, gates_p) = _nsa_proj(
        xp, w0, bg0, *tabs_p, tm=tm, seq_tiles=seq_tiles, win_rows=batch * WINDOW,
        win_map=lambda i: (i // seq_tiles, 0))
    ckv_p = _compress_prompt(cmp_p, pe_x, w1_bd, w2_bd, batch=batch, seq=seq)
    o_p = _nsa_attn_prompt(qraw, qrot, ckv_p, skd, svd, wkd, wvd, gates_p, batch=batch, seq=seq)
    yp = _proj_ln(o_p, w_out0, xp, ln_g[0, 0], ln_b[0, 0], tm=tm, alpha=alpha)

    (qraw_s, qrot_s, cmp_s, slc_s, win_s, _, _, _, _, gates_s) = _nsa_proj(
        xs, w0, bg0, *tabs_s, tm=bsz, seq_tiles=1, win_rows=bsz, win_map=lambda i: (i, 0))
    pages_fm = lambda c: jnp.transpose(c[0], (0, 2, 3, 4, 1))
    pe_n, w1_n, w2_n = _prep_page_compress_weights(nsa_pe_cmp[0], nsa_w_cmp1[0], nsa_w_cmp2[0])
    ckv_s = _compress_pages(pages_fm(cache_cmp_kv).reshape(n_pool, 2, kc, PAGE), page_table, pe_n, w1_n, w2_n)
    q16 = qraw_s.reshape(bsz, NSA_H, NSA_D)
    zeros16 = jnp.zeros_like(q16)
    q_e = jnp.concatenate([q16, zeros16], axis=-1)
    q_o = jnp.concatenate([zeros16, q16], axis=-1)
    ocmp_s, idx4 = _nsa_sample_select(q_e, q_o, ckv_s, bsz=bsz)
    idx = idx4[:, :, :N_SEL - 1, 0].reshape(bsz, NSA_G * (N_SEL - 1))
    q4r = qrot_s.reshape(bsz, NSA_G, NSA_R, NSA_D)
    q_cmp = jnp.einsum('bgrd,gh->bgrhd', q4r, jnp.eye(NSA_G, dtype=BF16)).reshape(bsz, NSA_H, kc)
    gates_s3 = gates_s.reshape(bsz, NSA_G, LANES)[:, :, :3 * NSA_R].reshape(bsz, NSA_G, 3, NSA_R)
    gates_s3 = gates_s3.transpose(0, 1, 3, 2).reshape(bsz, NSA_H, 3)
    o_s, win_buf_s = _nsa_sample_attn(
        idx, page_table, pages_fm(cache_slc_kv), qrot_s.reshape(bsz, NSA_H, NSA_D), q_cmp,
        slc_s.reshape(bsz, 1, 2 * kc), cache_win_kv[0].reshape(bsz, WINDOW, 2 * kc),
        win_s.reshape(bsz, 1, 2 * kc), gates_s3, ocmp_s)
    ys = _proj_ln(o_s.reshape(bsz, d), w_out0, xs, ln_g[0, 0], ln_b[0, 0], tm=bsz, alpha=alpha)

    yp = _ffn(yp, w_gu[0], w_dn[0], ln_g[0, 1], ln_b[0, 1], tm=tm, alpha=alpha)
    ys = _ffn(ys, w_gu[0], w_dn[0], ln_g[0, 1], ln_b[0, 1], tm=bsz, alpha=alpha)

    dk = gla_w_gate2.shape[2]
    dv = gla_w_out.shape[1]
    dkh, dvh = dk // GLA_H, dv // GLA_H
    gw = jnp.pad(gla_w_in[0], ((0, 0), (0, LANES - GLA_RANK))).astype(BF16)
    gw2 = jnp.pad(gla_w_gate2[0], ((0, LANES - GLA_RANK), (0, 0))).astype(BF16)
    gb2 = gla_b_gate2[0].reshape(1, dk)
    g_out = gla_w_out[0].astype(BF16)
    ng = gla_norm_g[0].reshape(1, dv)

    qi, ki, ks, vb, rs_p, al = _gla_proj_prompt(yp, gw, gw2, gb2, tm=tm, dk=dk, dv=dv)
    on_p, st_p = _gla_rec(qi, ki, ks, vb, al, batch=batch, seq=seq, dkh=dkh, dvh=dvh)
    yp = _gla_proj_ln(on_p, rs_p, ng, g_out, yp, ln_g[1, 0], ln_b[1, 0], tm=tm, alpha=alpha)

    q_s, k_s, v_s, rs_s, la_s = _gla_proj_sample(ys, gw, gw2, gb2, dk=dk, dv=dv)
    colv = lambda a: a.reshape(bsz, GLA_H, dkh, 1)
    on_s, st_s = _gla_step(colv(q_s), colv(k_s), colv(la_s), v_s.reshape(bsz, GLA_H, 1, dvh),
                           state_gla[0], scale=dkh ** -0.5)
    ys = _gla_proj_ln(on_s.reshape(bsz, dv), rs_s, ng, g_out, ys, ln_g[1, 0], ln_b[1, 0], tm=bsz, alpha=alpha)

    yp = _ffn(yp, w_gu[1], w_dn[1], ln_g[1, 1], ln_b[1, 1], tm=tm, alpha=alpha)
    ys = _ffn(ys, w_gu[1], w_dn[1], ln_g[1, 1], ln_b[1, 1], tm=bsz, alpha=alpha)

    kv6 = lambda a, rows: a.reshape(1, rows[0], rows[1], 2, NSA_G, NSA_D)
    return (yp.reshape(batch, seq, d), ys.reshape(bsz, 1, d),
            kv6(cmp_p, (batch, seq)), kv6(slc_p, (batch, seq)), kv6(win_p, (batch, WINDOW)),
            jnp.swapaxes(st_p, 2, 3)[None],
            kv6(cmp_s, (bsz, 1)), kv6(slc_s, (bsz, 1)), kv6(win_buf_s, (bsz, WINDOW)),
            st_s[None])
```

```python
import functools

import jax
import jax.numpy as jnp
from jax import lax
from jax.experimental import pallas as pl
from jax.experimental.pallas import tpu as pltpu

PAGE = 128
NSA_H = 16
NSA_D = 64
NSA_G = 4
NSA_R = NSA_H // NSA_G
CMP_BLK = 64
SEL_BLK = 64
SEL_SHIFT = 6
R_SHIFT = 2
LANE_SHIFT = 7
N_SEL = 16
WINDOW = 512
ATTN_SCALE = NSA_D ** -0.5
LOG2E = 1.4426950408889634
ROPE_THETA = 10000.0
NEG_INF = -1e30
SEL_FORCE = 1e4

GLA_H = 4
GLA_RANK = 16
GLA_TAU = 16.0
GLA_CHUNK = 32
LN_EPS = 1e-5

LANES = 128
HALF = LANES // 2
ONES_ROWS = 16
ATTN_TQ = 128
ATTN_KT = 256
ATTN_UNROLL = 4
VMEM_LIMIT = 56 * 1024 * 1024

F32 = jnp.float32
BF16 = jnp.bfloat16

NT_DIMS = (((1,), (1,)), ((), ()))
TN_DIMS = (((0,), (0,)), ((), ()))


def _params(sem):
    return pltpu.CompilerParams(dimension_semantics=sem, vmem_limit_bytes=VMEM_LIMIT)


def _dot(a, b):
    return jnp.dot(a, b, preferred_element_type=F32)


def _dot_nt(a, b):
    return lax.dot_general(a, b, NT_DIMS, preferred_element_type=F32)


def _dot_tn(a, b):
    return lax.dot_general(a, b, TN_DIMS, preferred_element_type=F32)


def _layer_norm(z, g, b):
    mu = jnp.mean(z, axis=-1, keepdims=True)
    zc = z - mu
    var = jnp.mean(zc * zc, axis=-1, keepdims=True)
    return zc * lax.rsqrt(var + LN_EPS) * g + b


def _silu(x):
    return x * jax.nn.sigmoid(x)


def _rope(x, cos_t, sin_lo, sin_hi):
    out = []
    for p in range(x.shape[1] // LANES):
        blk = x[:, p * LANES:(p + 1) * LANES]
        x_up = pltpu.roll(blk, LANES - NSA_D // 2, 1)
        x_dn = pltpu.roll(blk, NSA_D // 2, 1)
        out.append(blk * cos_t + x_up * sin_lo + x_dn * sin_hi)
    return out[0] if len(out) == 1 else jnp.concatenate(out, axis=1)


def _dup_heads(x):
    rows, c = x.shape
    lane = lax.broadcasted_iota(jnp.int32, (rows, LANES), 1)
    lo = lane < HALF
    out = []
    for p in range(c // LANES):
        blk = x[:, p * LANES:(p + 1) * LANES]
        sw = pltpu.roll(blk, HALF, 1)
        out.append(jnp.where(lo, blk, sw))
        out.append(jnp.where(lo, sw, blk))
    return jnp.concatenate(out, axis=1)


def _nsa_proj_kernel(x_ref, w_ref, bg_ref, cos_ref, slo_ref, shi_ref, *refs, attn_kt):
    qc = NSA_H * NSA_D
    kc = NSA_G * NSA_D
    if attn_kt is None:
        qraw_ref, qrot_ref, cmp_ref, slc_ref, win_ref, gates_ref = refs
    else:
        (wvt_ref, qraw_ref, qrot_ref, cmp_ref, slc_ref, win_ref, gates_ref,
         skd_ref, wkd_ref, svt_ref, wvt_out_ref) = refs
    xb = x_ref[...].astype(BF16)
    cos_t, slo, shi = cos_ref[...], slo_ref[...], shi_ref[...]

    q = _dot(xb, w_ref[:, 0:qc])
    qraw_ref[...] = (q * ATTN_SCALE).astype(BF16)
    qrot_ref[...] = (_rope(q, cos_t, slo, shi) * (ATTN_SCALE * LOG2E)).astype(BF16)

    cmp_ref[...] = _dot(xb, w_ref[:, qc:qc + 2 * kc])

    slc = _dot(xb, w_ref[:, qc + 2 * kc:qc + 4 * kc])
    sk = _rope(slc[:, :kc], cos_t, slo, shi)
    slc_ref[:, :kc] = sk
    slc_ref[:, kc:] = slc[:, kc:]

    win = _dot(xb, w_ref[:, qc + 4 * kc:qc + 6 * kc])
    wk = _rope(win[:, :kc], cos_t, slo, shi)
    win_ref[:, :kc] = wk
    win_ref[:, kc:] = win[:, kc:]

    gz = _dot(xb, w_ref[:, qc + 6 * kc:qc + 6 * kc + NSA_G * LANES]) + bg_ref[...]
    gates_ref[...] = jax.nn.sigmoid(gz)

    if attn_kt is not None:
        skd_ref[...] = _dup_heads(sk).astype(BF16)
        wkd_ref[...] = _dup_heads(wk).astype(BF16)
        for src, dst in ((0, svt_ref), (1, wvt_out_ref)):
            vt = _dot_nt(wvt_ref[src], xb).astype(BF16)
            for t in range(x_ref.shape[0] // attn_kt):
                dst[t] = vt[:, t * attn_kt:(t + 1) * attn_kt]


def _nsa_proj(x, w, bg, cos_t, slo, shi, wvt=None, *, tm, seq_tiles, win_rows, win_map, attn_kt=None):
    n, d = x.shape
    qc = NSA_H * NSA_D
    kc2 = 2 * NSA_G * NSA_D
    kd = NSA_G * LANES
    row = lambda i: (i, 0)
    tab = lambda i: (i % seq_tiles, 0)
    const = lambda i: (0, 0)
    in_specs = [pl.BlockSpec((tm, d), row), pl.BlockSpec(w.shape, const), pl.BlockSpec(bg.shape, const),
                pl.BlockSpec((tm, LANES), tab), pl.BlockSpec((tm, LANES), tab), pl.BlockSpec((tm, LANES), tab)]
    out_shape = [jax.ShapeDtypeStruct((n, qc), BF16), jax.ShapeDtypeStruct((n, qc), BF16),
                 jax.ShapeDtypeStruct((n, kc2), F32), jax.ShapeDtypeStruct((n, kc2), F32),
                 jax.ShapeDtypeStruct((win_rows, kc2), F32), jax.ShapeDtypeStruct((n, kd), F32)]
    out_specs = [pl.BlockSpec((tm, qc), row), pl.BlockSpec((tm, qc), row),
                 pl.BlockSpec((tm, kc2), row), pl.BlockSpec((tm, kc2), row),
                 pl.BlockSpec((tm, kc2), win_map), pl.BlockSpec((tm, kd), row)]
    args = [x, w, bg, cos_t, slo, shi]
    if attn_kt is not None:
        assert tm % attn_kt == 0
        in_specs.append(pl.BlockSpec(wvt.shape, lambda i: (0, 0, 0)))
        args.append(wvt)
        tiles = tm // attn_kt
        out_shape += [jax.ShapeDtypeStruct((n, kd), BF16), jax.ShapeDtypeStruct((n, kd), BF16),
                      jax.ShapeDtypeStruct((n // attn_kt, kc2 // 2, attn_kt), BF16),
                      jax.ShapeDtypeStruct((n // attn_kt, kc2 // 2, attn_kt), BF16)]
        out_specs += [pl.BlockSpec((tm, kd), row), pl.BlockSpec((tm, kd), row),
                      pl.BlockSpec((tiles, kc2 // 2, attn_kt), lambda i: (i, 0, 0)),
                      pl.BlockSpec((tiles, kc2 // 2, attn_kt), lambda i: (i, 0, 0))]
    return pl.pallas_call(
        functools.partial(_nsa_proj_kernel, attn_kt=attn_kt),
        grid=(n // tm,),
        in_specs=in_specs,
        out_specs=tuple(out_specs),
        out_shape=tuple(out_shape),
        compiler_params=_params(("arbitrary",)),
        name="nsa_proj",
    )(*args)


def _compress_body(tok_refs, pe_ref, w1_ref, w2_ref, out_ref):
    m = tok_refs[0].shape[0] // CMP_BLK
    acc = jnp.zeros((m, w1_ref.shape[2]), F32)
    for l in range(CMP_BLK):
        xl = jnp.concatenate([r[pl.ds(l, m, stride=CMP_BLK), :] for r in tok_refs], axis=1)
        xl = xl + pe_ref[l:l + 1, :]
        acc = acc + _dot(xl.astype(BF16), w1_ref[l])
    h = _silu(acc)
    out_ref[...] = _dot(h.astype(BF16), w2_ref[...])


def _compress_prompt_kernel(tok0_ref, tok1_ref, pe_ref, w1_ref, w2_ref, out_ref):
    _compress_body((tok0_ref, tok1_ref), pe_ref, w1_ref, w2_ref, out_ref)


def _compress_prompt(cmp_kv, pe, w1, w2, *, batch, seq):
    kc = NSA_G * NSA_D
    nb = seq // CMP_BLK
    return pl.pallas_call(
        _compress_prompt_kernel,
        grid=(2, batch),
        in_specs=[pl.BlockSpec((seq, LANES), lambda c, b: (b, 2 * c)),
                  pl.BlockSpec((seq, LANES), lambda c, b: (b, 2 * c + 1)),
                  pl.BlockSpec((None, CMP_BLK, kc), lambda c, b: (c, 0, 0)),
                  pl.BlockSpec((None, CMP_BLK, kc, kc), lambda c, b: (c, 0, 0, 0)),
                  pl.BlockSpec((None, kc, 2 * kc), lambda c, b: (c, 0, 0))],
        out_specs=pl.BlockSpec((None, nb, 2 * kc), lambda c, b: (c, b, 0)),
        out_shape=jax.ShapeDtypeStruct((2, batch * nb, 2 * kc), F32),
        compiler_params=_params(("arbitrary", "arbitrary")),
        name="compress_prompt",
    )(cmp_kv, cmp_kv, pe, w1, w2)


def _compress_pages_kernel(pt_ref, *refs, pages):
    del pt_ref
    page_refs = refs[:pages]
    pe_ref, w1_ref, w2_ref, out_ref, buf_ref = refs[pages:]
    rows_pp = NSA_G * NSA_D
    for k in range(pages):
        buf_ref[k * rows_pp:(k + 1) * rows_pp, :] = page_refs[k][...]
    m = pages * NSA_G
    acc = jnp.zeros((m, LANES), F32)
    for dd in range(NSA_D // 2):
        x = jnp.concatenate([buf_ref[pl.ds(2 * dd, m, stride=NSA_D), :],
                             buf_ref[pl.ds(2 * dd + 1, m, stride=NSA_D), :]], axis=1)
        x = x + pe_ref[dd:dd + 1, :]
        acc = acc + _dot(x.astype(BF16), w1_ref[dd])
    out_ref[...] = _dot(_silu(acc).astype(BF16), w2_ref[...])


def _compress_pages(cache, page_table, pe, w1, w2):
    bsz, n_pages = page_table.shape
    pages = min(64, n_pages)
    groups = n_pages // pages
    gd = NSA_G * NSA_D
    assert PAGE == 2 * CMP_BLK and PAGE == LANES

    def page_map(k):
        return lambda c, b, h, pt: (pt[b, h * pages + k], c, 0, 0)

    in_specs = [pl.BlockSpec((None, None, gd, PAGE), page_map(k)) for k in range(pages)]
    in_specs += [pl.BlockSpec((None,) + pe.shape[1:], lambda c, b, h, pt: (c, 0, 0)),
                 pl.BlockSpec((None,) + w1.shape[1:], lambda c, b, h, pt: (c, 0, 0, 0)),
                 pl.BlockSpec((None,) + w2.shape[1:], lambda c, b, h, pt: (c, 0, 0))]
    grid_spec = pltpu.PrefetchScalarGridSpec(
        num_scalar_prefetch=1,
        grid=(2, bsz, groups),
        in_specs=in_specs,
        out_specs=pl.BlockSpec((None, pages * NSA_G, LANES), lambda c, b, h, pt: (c, b * groups + h, 0)),
        scratch_shapes=[pltpu.VMEM((pages * gd, PAGE), F32)],
    )
    return pl.pallas_call(
        functools.partial(_compress_pages_kernel, pages=pages),
        grid_spec=grid_spec,
        out_shape=jax.ShapeDtypeStruct((2, bsz * n_pages * NSA_G, LANES), F32),
        compiler_params=_params(("arbitrary", "arbitrary", "arbitrary")),
        name="compress_pages",
    )(page_table, *([cache] * pages), pe, w1, w2)


def _softmax_tile(carry, s_t, vt_tile, bias, tq):
    ms, acc = carry
    new_ms, ps, alphas = [], [], []
    for r in range(NSA_R):
        sb = s_t[:, r * tq:(r + 1) * tq] + bias
        m_new = jnp.maximum(ms[r], jnp.max(sb, axis=0, keepdims=True))
        ps.append(jnp.exp2(sb - m_new).astype(BF16))
        alphas.append(jnp.exp2(ms[r] - m_new))
        new_ms.append(m_new)
    acc = jnp.concatenate(alphas, axis=1) * acc + _dot(vt_tile, jnp.concatenate(ps, axis=1))
    return tuple(new_ms), acc


def _rank_blocks(score_ref, rank_ref, last_blk):
    nb, tq = score_ref.shape
    sub = lax.broadcasted_iota(jnp.int32, (8, tq), 0)
    rank_ref[...] = jnp.zeros((nb, tq), F32)
    for gi in range(nb // 8):
        @pl.when(gi * 8 <= last_blk)
        def _(gi=gi):
            rows8 = score_ref[gi * 8:(gi + 1) * 8, :]
            tops = [jnp.broadcast_to(rows8[u:u + 1, :], (8, tq)) for u in range(8)]
            for v in range(nb // 8):
                sv = score_ref[v * 8:(v + 1) * 8, :]
                cnt = jnp.zeros((8, tq), F32)
                for u in range(8):
                    if v > gi:
                        before = tops[u] >= sv
                    elif v < gi:
                        before = tops[u] > sv
                    else:
                        before = (tops[u] > sv) | ((tops[u] == sv) & (sub > u))
                    cnt = cnt + jnp.where(before, 1.0, 0.0)
                rank_ref[v * 8:(v + 1) * 8, :] += cnt


def _nsa_attn_kernel(qraw_ref, qrot_ref, ck_ref, cv_ref, sk_ref, svt_ref, wk_ref, wvt_ref,
                     gates_ref, out_ref, score_ref, rank_ref, *, tq, kt, n_sel, unroll):
    nb = ck_ref.shape[0]
    rows = NSA_R * tq
    q0 = pl.program_id(2) * tq

    lane = lax.broadcasted_iota(jnp.int32, (tq, LANES), 1)
    lo = lane < HALF

    def stack(ref):
        qa = ref[:, 0:LANES].astype(F32)
        qb = ref[:, LANES:2 * LANES].astype(F32)
        parts = [jnp.where(lo, qa, 0.0), jnp.where(lo, 0.0, qa),
                 jnp.where(lo, qb, 0.0), jnp.where(lo, 0.0, qb)]
        return jnp.concatenate(parts, axis=0).astype(BF16)

    q_raw = stack(qraw_ref)
    q_rot = stack(qrot_ref)
    ck = ck_ref[...].astype(BF16)
    cv = cv_ref[...].astype(BF16)

    st = _dot_nt(ck, q_raw)
    t_c = q0 + (lax.broadcasted_iota(jnp.int32, (nb, rows), 1) & (tq - 1))
    n_c = lax.broadcasted_iota(jnp.int32, (nb, rows), 0)
    mask_t = (n_c + 1) * CMP_BLK - 1 <= t_c
    st = jnp.where(mask_t, st, NEG_INF)
    pt = jnp.where(mask_t, jnp.exp(st - jnp.max(st, axis=0, keepdims=True)), 0.0)
    pt = pt / jnp.maximum(jnp.sum(pt, axis=0, keepdims=True), 1e-30)
    o_cmp = _dot_tn(cv[:, 0:NSA_D], pt.astype(BF16))

    imp = pt[:, 0:tq]
    for r in range(1, NSA_R):
        imp = imp + pt[:, r * tq:(r + 1) * tq]
    blk = lax.broadcasted_iota(jnp.int32, (nb, tq), 0)
    cur = (q0 + lax.broadcasted_iota(jnp.int32, (nb, tq), 1)) >> SEL_SHIFT
    forced = (blk == 0) | (blk == cur) | (blk == cur - 1)
    valid = blk <= cur
    score_ref[...] = jnp.where(forced, SEL_FORCE, jnp.where(valid, imp, -1.0))
    _rank_blocks(score_ref, rank_ref, (q0 + tq - 1) >> SEL_SHIFT)
    score_ref[...] = jnp.where((rank_ref[...] < n_sel) & valid, 0.0, NEG_INF)

    key_row = lax.broadcasted_iota(jnp.int32, (kt, tq), 0)
    t_lane = q0 + lax.broadcasted_iota(jnp.int32, (kt, tq), 1)
    ones = jnp.ones((ONES_ROWS, kt), BF16)

    init = (tuple(jnp.full((1, tq), NEG_INF, F32) for _ in range(NSA_R)),
            jnp.zeros((NSA_D + ONES_ROWS, rows), F32))

    hi = (q0 + tq + kt - 1) // kt
    n_tiles = sk_ref.shape[0] // kt

    def tile_loop(lo_tile, k_ref, vt_ref, make_bias):
        def body(i, carry):
            staged = []
            for u in range(unroll):
                j = lo_tile + i * unroll + u
                jc = jnp.minimum(j, n_tiles - 1)
                k0 = pl.multiple_of(jc * kt, kt)
                k_pos = jnp.where(j < hi, k0, 2 * n_tiles * kt) + key_row
                staged.append((_dot_nt(k_ref[pl.ds(k0, kt), :], q_rot), make_bias(k0, k_pos), jc))
            for s_t, bias, jc in staged:
                vt = jnp.concatenate([vt_ref[jc], ones], axis=0)
                carry = _softmax_tile(carry, s_t, vt, bias, tq)
            return carry
        return lax.fori_loop(0, (hi - lo_tile + unroll - 1) // unroll, body, init)[1]

    def sel_bias(k0, k_pos):
        blk0 = k0 // SEL_BLK
        picked = jnp.concatenate(
            [jnp.broadcast_to(score_ref[pl.ds(blk0 + i, 1), :], (SEL_BLK, tq)) for i in range(kt // SEL_BLK)],
            axis=0)
        return jnp.where(k_pos <= t_lane, picked, NEG_INF)

    acc_s = tile_loop(0, sk_ref, svt_ref, sel_bias)

    def win_bias(k0, k_pos):
        diff = t_lane - k_pos
        return jnp.where((diff >= 0) & (diff <= WINDOW), 0.0, NEG_INF)

    acc_w = tile_loop(jnp.maximum(q0 - WINDOW, 0) // kt, wk_ref, wvt_ref, win_bias)

    gt = gates_ref[...].T
    o_s = acc_s[0:NSA_D, :] / jnp.maximum(acc_s[NSA_D:NSA_D + 1, :], 1e-30)
    o_w = acc_w[0:NSA_D, :] / jnp.maximum(acc_w[NSA_D:NSA_D + 1, :], 1e-30)
    heads = []
    for r in range(NSA_R):
        sl = slice(r * tq, (r + 1) * tq)
        heads.append(gt[r:r + 1, :] * o_cmp[:, sl] + gt[NSA_R + r:NSA_R + r + 1, :] * o_s[:, sl]
                     + gt[2 * NSA_R + r:2 * NSA_R + r + 1, :] * o_w[:, sl])
    for pair in range(NSA_R // 2):
        both = jnp.concatenate(heads[2 * pair:2 * pair + 2], axis=0)
        out_ref[:, pair * LANES:(pair + 1) * LANES] = both.T.astype(BF16)


def _nsa_attn_prompt(qraw, qrot, ckv, skd, svt, wkd, wvt, gates, *, batch, seq, tq, kt):
    assert tq == LANES and kt % tq == 0 and WINDOW % kt == 0 and seq % kt == 0
    nqt = seq // tq
    nb = seq // CMP_BLK
    gc = NSA_R * NSA_D
    q_map = lambda b, g, t: (b * nqt + t, g)
    kv_map = lambda b, g, t: (b, g)
    vt_map = lambda b, g, t: (b, g, 0)
    return pl.pallas_call(
        functools.partial(_nsa_attn_kernel, tq=tq, kt=kt, n_sel=min(N_SEL, seq // SEL_BLK),
                          unroll=ATTN_UNROLL),
        grid=(batch, NSA_G, nqt),
        in_specs=[pl.BlockSpec((tq, gc), q_map), pl.BlockSpec((tq, gc), q_map),
                  pl.BlockSpec((None, nb, LANES), lambda b, g, t: (0, b, g)),
                  pl.BlockSpec((None, nb, LANES), lambda b, g, t: (1, b, g)),
                  pl.BlockSpec((seq, LANES), kv_map), pl.BlockSpec((seq // kt, NSA_D, kt), vt_map),
                  pl.BlockSpec((seq, LANES), kv_map), pl.BlockSpec((seq // kt, NSA_D, kt), vt_map),
                  pl.BlockSpec((tq, LANES), q_map)],
        out_specs=pl.BlockSpec((tq, gc), q_map),
        out_shape=jax.ShapeDtypeStruct((batch * seq, NSA_H * NSA_D), BF16),
        scratch_shapes=[pltpu.VMEM((nb, tq), F32), pltpu.VMEM((nb, tq), F32)],
        compiler_params=_params(("arbitrary", "arbitrary", "arbitrary")),
        name="nsa_attn_prompt",
    )(qraw, qrot, ckv, ckv, skd, svt, wkd, wvt, gates)


def _nsa_sample_select_kernel(qe_ref, qo_ref, ck_ref, cv_ref, ocmp_ref, idx_ref):
    npg = ck_ref.shape[0] // NSA_G
    nbp = 2 * npg
    qe, qo = qe_ref[...], qo_ref[...]
    head = lax.broadcasted_iota(jnp.int32, (NSA_H, LANES), 0)

    def block_of(j):
        return jnp.where(j < npg, 2 * j, 2 * (j - npg) + 1)

    n_row = block_of(lax.broadcasted_iota(jnp.int32, (1, nbp), 1))
    n_col = block_of(lax.broadcasted_iota(jnp.int32, (nbp, 1), 0))
    forced = (n_row == 0) | (n_row == nbp - 1)
    diag = lax.broadcasted_iota(jnp.int32, (nbp, nbp), 0) == lax.broadcasted_iota(jnp.int32, (nbp, nbp), 1)
    k_idx = lax.broadcasted_iota(jnp.int32, (N_SEL, nbp), 0).astype(F32)
    n_f = jnp.broadcast_to(n_row, (N_SEL, nbp)).astype(F32)
    ocmp = jnp.zeros((NSA_H, LANES), F32)
    for g in range(NSA_G):
        ck = ck_ref[pl.ds(g, npg, stride=NSA_G), :].astype(BF16)
        cv = cv_ref[pl.ds(g, npg, stride=NSA_G), :].astype(BF16)
        s = jnp.concatenate([_dot_nt(qe, ck), _dot_nt(qo, ck)], axis=1)
        p = jnp.exp(s - jnp.max(s, axis=-1, keepdims=True))
        p = p / jnp.maximum(jnp.sum(p, axis=-1, keepdims=True), 1e-30)
        o_e = _dot(p[:, :npg].astype(BF16), cv)
        o_o = _dot(p[:, npg:].astype(BF16), cv)
        ocmp = jnp.where((head >> R_SHIFT) == g, o_e + pltpu.roll(o_o, HALF, 1), ocmp)

        imp = p[NSA_R * g:NSA_R * g + 1, :]
        for r in range(1, NSA_R):
            imp = imp + p[NSA_R * g + r:NSA_R * g + r + 1, :]
        row = jnp.where(forced, SEL_FORCE, imp)
        col = jnp.sum(jnp.where(diag, row, 0.0), axis=1, keepdims=True)
        beats = (col > row) | ((col == row) & (n_col < n_row))
        rank = jnp.sum(jnp.where(beats, 1.0, 0.0), axis=0, keepdims=True)
        onehot = jnp.where(rank == k_idx, n_f, 0.0)
        idx_ref[g] = jnp.sum(onehot, axis=1, keepdims=True).astype(jnp.int32)
    ocmp_ref[...] = ocmp


def _nsa_sample_select(q_e, q_o, ckv, *, bsz):
    rows = ckv.shape[1] // bsz
    return pl.pallas_call(
        _nsa_sample_select_kernel,
        grid=(bsz,),
        in_specs=[pl.BlockSpec((None, NSA_H, LANES), lambda b: (b, 0, 0)),
                  pl.BlockSpec((None, NSA_H, LANES), lambda b: (b, 0, 0)),
                  pl.BlockSpec((None, rows, LANES), lambda b: (0, b, 0)),
                  pl.BlockSpec((None, rows, LANES), lambda b: (1, b, 0))],
        out_specs=(pl.BlockSpec((None, NSA_H, LANES), lambda b: (b, 0, 0)),
                   pl.BlockSpec((None, NSA_G, N_SEL, 1), lambda b: (b, 0, 0, 0))),
        out_shape=(jax.ShapeDtypeStruct((bsz, NSA_H, LANES), F32),
                   jax.ShapeDtypeStruct((bsz, NSA_G, N_SEL, 1), jnp.int32)),
        compiler_params=_params(("arbitrary",)),
        name="nsa_sample_select",
    )(q_e, q_o, ckv, ckv)


def _nsa_sample_attn_kernel(idx_ref, pt_ref, *refs, n_past):
    del pt_ref
    slabs = refs[:NSA_G * n_past]
    (q64_ref, q_ref, snew_ref, wcache_ref, wnew_ref, gates_ref, ocmp_ref,
     out_ref, wout_ref, wbuf_ref) = refs[NSA_G * n_past:]
    kc = NSA_G * NSA_D
    b = pl.program_id(0)
    q64 = q64_ref[...]
    q = q_ref[...]
    head_row = lax.broadcasted_iota(jnp.int32, (NSA_H, kc), 0)
    head_lane = lax.broadcasted_iota(jnp.int32, (NSA_H, kc), 1)
    own = (head_lane >> SEL_SHIFT) == (head_row >> R_SHIFT)
    fold = jnp.where((lax.broadcasted_iota(jnp.int32, (kc, NSA_D), 0) & (NSA_D - 1))
                     == lax.broadcasted_iota(jnp.int32, (kc, NSA_D), 1), 1.0, 0.0)
    hi = lax.Precision.HIGHEST

    def rounded(x):
        return x.astype(BF16).astype(F32)

    snew = snew_ref[...]
    s_new = jnp.sum(q.astype(F32) * rounded(snew[:, :kc]), axis=-1, keepdims=True)
    v_new = jnp.dot(jnp.where(own, rounded(snew[:, kc:]), 0.0), fold, precision=hi,
                    preferred_element_type=F32)
    half_of_lane = lax.broadcasted_iota(jnp.int32, (1, PAGE), 1) >> SEL_SHIFT
    row64 = lax.broadcasted_iota(jnp.int32, (NSA_H, NSA_D), 0)
    f_sel = jnp.zeros((NSA_H, NSA_D), F32)
    for g in range(NSA_G):
        mine = slabs[g * n_past:(g + 1) * n_past]
        k_cat = jnp.concatenate([r[0] for r in mine], axis=1).astype(BF16)
        v_cat = jnp.concatenate([r[1] for r in mine], axis=1).astype(BF16)
        bias = jnp.concatenate(
            [jnp.where(half_of_lane == (idx_ref[b, g * n_past + k] & 1), 0.0, NEG_INF)
             for k in range(n_past)], axis=1)
        sb = _dot(q64, k_cat) + bias
        m = jnp.maximum(jnp.max(sb, axis=-1, keepdims=True), s_new)
        p = jnp.exp2(sb - m)
        p_new = jnp.exp2(s_new - m)
        l = jnp.sum(p, axis=-1, keepdims=True) + p_new
        o = _dot_nt(p.astype(BF16), v_cat) + rounded(p_new) * v_new
        f_sel = jnp.where((row64 >> R_SHIFT) == g, o / jnp.maximum(l, 1e-30), f_sel)

    wb = wcache_ref.shape[0]
    wbuf_ref[0:wb, :] = wcache_ref[...]
    wbuf_ref[wb:wb + 1, :] = wnew_ref[...]
    wbuf_ref[wb + 1:, :] = jnp.zeros((wbuf_ref.shape[0] - wb - 1, 2 * kc), F32)
    s = _dot_nt(q, wbuf_ref[:, :kc].astype(BF16))
    ok = lax.broadcasted_iota(jnp.int32, s.shape, 1) < wb + 1
    s = jnp.where(ok, s, NEG_INF)
    p = jnp.where(ok, jnp.exp2(s - jnp.max(s, axis=-1, keepdims=True)), 0.0)
    p = p / jnp.maximum(jnp.sum(p, axis=-1, keepdims=True), 1e-30)
    o_win = _dot(p.astype(BF16), wbuf_ref[:, kc:].astype(BF16))
    f_win = jnp.dot(jnp.where(own, o_win, 0.0), fold, precision=hi, preferred_element_type=F32)
    wout_ref[...] = wbuf_ref[1:wb + 1, :]

    gts = gates_ref[...]
    out_ref[...] = (gts[:, 0:1] * ocmp_ref[:, 0:NSA_D] + gts[:, 1:2] * f_sel + gts[:, 2:3] * f_win)


def _nsa_sample_attn(idx, page_table, slc_cache, q64, q_cmp, slc_new, win_cache, win_new, gates, ocmp):
    bsz = page_table.shape[0]
    n_past = idx.shape[1] // NSA_G
    kc = NSA_G * NSA_D
    wb = win_cache.shape[1]
    assert PAGE == 2 * SEL_BLK

    def slab_map(j):
        def f(b, idx_ref, pt_ref):
            return (pt_ref[b, idx_ref[b, j] // 2], 0, j // n_past, 0, 0)
        return f

    pad = lambda n: -(-n // LANES) * LANES
    in_specs = [pl.BlockSpec((None, 2, None, NSA_D, PAGE), slab_map(j)) for j in range(NSA_G * n_past)]
    in_specs += [pl.BlockSpec((None, NSA_H, NSA_D), lambda b, i, p: (b, 0, 0)),
                 pl.BlockSpec((None, NSA_H, kc), lambda b, i, p: (b, 0, 0)),
                 pl.BlockSpec((None, 1, 2 * kc), lambda b, i, p: (b, 0, 0)),
                 pl.BlockSpec((None, wb, 2 * kc), lambda b, i, p: (b, 0, 0)),
                 pl.BlockSpec((None, 1, 2 * kc), lambda b, i, p: (b, 0, 0)),
                 pl.BlockSpec((None, NSA_H, 3), lambda b, i, p: (b, 0, 0)),
                 pl.BlockSpec((None, NSA_H, LANES), lambda b, i, p: (b, 0, 0))]
    grid_spec = pltpu.PrefetchScalarGridSpec(
        num_scalar_prefetch=2,
        grid=(bsz,),
        in_specs=in_specs,
        out_specs=(pl.BlockSpec((None, NSA_H, NSA_D), lambda b, i, p: (b, 0, 0)),
                   pl.BlockSpec((None, wb, 2 * kc), lambda b, i, p: (b, 0, 0))),
        scratch_shapes=[pltpu.VMEM((pad(wb + 1), 2 * kc), F32)],
    )
    return pl.pallas_call(
        functools.partial(_nsa_sample_attn_kernel, n_past=n_past),
        grid_spec=grid_spec,
        out_shape=(jax.ShapeDtypeStruct((bsz, NSA_H, NSA_D), F32),
                   jax.ShapeDtypeStruct((bsz, wb, 2 * kc), F32)),
        compiler_params=_params(("arbitrary",)),
        name="nsa_sample_attn",
    )(idx, page_table, *([slc_cache] * (NSA_G * n_past)), q64, q_cmp, slc_new, win_cache, win_new, gates, ocmp)


def _proj_ln_kernel(a_ref, w_ref, x_ref, g_ref, b_ref, out_ref, *, alpha):
    y = _dot(a_ref[...].astype(BF16), w_ref[...])
    out_ref[...] = _layer_norm(alpha * x_ref[...] + y, g_ref[...], b_ref[...])


def _gla_proj_ln_kernel(o_ref, rs_ref, ng_ref, w_ref, x_ref, g_ref, b_ref, out_ref, *, alpha):
    a = (o_ref[...] * ng_ref[...] * rs_ref[...]).astype(BF16)
    y = _dot(a, w_ref[...])
    out_ref[...] = _layer_norm(alpha * x_ref[...] + y, g_ref[...], b_ref[...])


def _proj_ln(a, w, x, g, b, *, tm, alpha):
    n, d = x.shape
    row = lambda i: (i, 0)
    const = lambda i: (0, 0)
    return pl.pallas_call(
        functools.partial(_proj_ln_kernel, alpha=alpha),
        grid=(n // tm,),
        in_specs=[pl.BlockSpec((tm, a.shape[1]), row), pl.BlockSpec(w.shape, const),
                  pl.BlockSpec((tm, d), row), pl.BlockSpec((1, d), const), pl.BlockSpec((1, d), const)],
        out_specs=pl.BlockSpec((tm, d), row),
        out_shape=jax.ShapeDtypeStruct((n, d), F32),
        compiler_params=_params(("arbitrary",)),
        name="proj_ln",
    )(a, w, x, g, b)


def _gla_proj_ln(o, rs, ng, w, x, g, b, *, tm, alpha):
    n, d = x.shape
    row = lambda i: (i, 0)
    const = lambda i: (0, 0)
    return pl.pallas_call(
        functools.partial(_gla_proj_ln_kernel, alpha=alpha),
        grid=(n // tm,),
        in_specs=[pl.BlockSpec((tm, o.shape[1]), row), pl.BlockSpec((tm, o.shape[1]), row),
                  pl.BlockSpec((1, o.shape[1]), const), pl.BlockSpec(w.shape, const),
                  pl.BlockSpec((tm, d), row), pl.BlockSpec((1, d), const), pl.BlockSpec((1, d), const)],
        out_specs=pl.BlockSpec((tm, d), row),
        out_shape=jax.ShapeDtypeStruct((n, d), F32),
        compiler_params=_params(("arbitrary",)),
        name="gla_proj_ln",
    )(o, rs, ng, w, x, g, b)


def _ffn_kernel(x_ref, wg_ref, wu_ref, wd_ref, g_ref, b_ref, out_ref, acc_ref, xb_ref, *, alpha):
    k = pl.program_id(1)

    @pl.when(k == 0)
    def _():
        xb_ref[...] = x_ref[...].astype(BF16)
        acc_ref[...] = jnp.zeros_like(acc_ref)

    xb = xb_ref[...]
    h = _silu(_dot(xb, wg_ref[...])) * _dot(xb, wu_ref[...])
    acc_ref[...] += _dot(h.astype(BF16), wd_ref[...])

    @pl.when(k == pl.num_programs(1) - 1)
    def _():
        out_ref[...] = _layer_norm(alpha * x_ref[...] + acc_ref[...], g_ref[...], b_ref[...])


def _ffn(x, w_gu, w_down, g, b, *, tm, alpha):
    n, d = x.shape
    d_ff = w_down.shape[0]
    splits = 2
    tf = d_ff // splits
    assert tf * splits == d_ff and tf % LANES == 0
    return pl.pallas_call(
        functools.partial(_ffn_kernel, alpha=alpha),
        grid=(n // tm, splits),
        in_specs=[pl.BlockSpec((tm, d), lambda i, k: (i, 0)),
                  pl.BlockSpec((d, tf), lambda i, k: (0, k)),
                  pl.BlockSpec((d, tf), lambda i, k: (0, splits + k)),
                  pl.BlockSpec((tf, d), lambda i, k: (k, 0)),
                  pl.BlockSpec((1, d), lambda i, k: (0, 0)),
                  pl.BlockSpec((1, d), lambda i, k: (0, 0))],
        out_specs=pl.BlockSpec((tm, d), lambda i, k: (i, 0)),
        out_shape=jax.ShapeDtypeStruct((n, d), F32),
        scratch_shapes=[pltpu.VMEM((tm, d), F32), pltpu.VMEM((tm, d), BF16)],
        compiler_params=_params(("arbitrary", "arbitrary")),
        name="ffn",
    )(x, w_gu, w_gu, w_down, g, b)


def _gla_log_decay(low, w2_ref, b2_ref):
    z = _dot(low.astype(BF16), w2_ref[...]) + b2_ref[...]
    return jax.nn.log_sigmoid(z) / GLA_TAU


def _gla_proj_prompt_kernel(x_ref, w_ref, w2_ref, b2_ref, qi_ref, ki_ref, ks_ref, v_ref, rs_ref,
                            al_ref, *, dk, dv, scale):
    tm = x_ref.shape[0]
    xb = x_ref[...].astype(BF16)
    q = _dot(xb, w_ref[:, 0:dk])
    k = _dot(xb, w_ref[:, dk:2 * dk])
    v_ref[...] = _dot(xb, w_ref[:, 2 * dk:2 * dk + dv]).astype(BF16)
    rs_ref[...] = _silu(_dot(xb, w_ref[:, 2 * dk + dv:2 * dk + 2 * dv]))
    low = _dot(xb, w_ref[:, 2 * dk + 2 * dv:])
    la = _gla_log_decay(low, w2_ref, b2_ref)

    rin = lax.broadcasted_iota(jnp.int32, (tm, dk), 0) & (GLA_CHUNK - 1)
    lb = la
    s = 1
    while s < GLA_CHUNK:
        lb = lb + jnp.where(rin >= s, pltpu.roll(lb, s, 0), 0.0)
        s *= 2
    lb3 = lb.reshape(tm // GLA_CHUNK, GLA_CHUNK, dk)
    last = lb3[:, GLA_CHUNK - 1:GLA_CHUNK, :]
    qi_ref[...] = (q * jnp.exp(lb) * scale).astype(BF16)
    ki_ref[...] = (k * jnp.exp(-lb)).astype(BF16)
    ks_ref[...] = (k * jnp.exp(last - lb3).reshape(tm, dk)).astype(BF16)
    al_ref[...] = jnp.exp(last.reshape(tm // GLA_CHUNK, dk))


def _gla_proj_prompt(x, w, w2, b2, *, tm, dk, dv):
    n, d = x.shape
    row = lambda i: (i, 0)
    const = lambda i: (0, 0)
    return pl.pallas_call(
        functools.partial(_gla_proj_prompt_kernel, dk=dk, dv=dv, scale=(dk // GLA_H) ** -0.5),
        grid=(n // tm,),
        in_specs=[pl.BlockSpec((tm, d), row), pl.BlockSpec(w.shape, const),
                  pl.BlockSpec(w2.shape, const), pl.BlockSpec(b2.shape, const)],
        out_specs=(pl.BlockSpec((tm, dk), row), pl.BlockSpec((tm, dk), row), pl.BlockSpec((tm, dk), row),
                   pl.BlockSpec((tm, dv), row), pl.BlockSpec((tm, dv), row),
                   pl.BlockSpec((tm // GLA_CHUNK, dk), row)),
        out_shape=(jax.ShapeDtypeStruct((n, dk), BF16), jax.ShapeDtypeStruct((n, dk), BF16),
                   jax.ShapeDtypeStruct((n, dk), BF16), jax.ShapeDtypeStruct((n, dv), BF16),
                   jax.ShapeDtypeStruct((n, dv), F32),
                   jax.ShapeDtypeStruct((n // GLA_CHUNK, dk), F32)),
        compiler_params=_params(("arbitrary",)),
        name="gla_proj_prompt",
    )(x, w, w2, b2)


def _gla_proj_sample_kernel(x_ref, w_ref, w2_ref, b2_ref, q_ref, k_ref, v_ref, rs_ref, la_ref, *, dk, dv):
    xb = x_ref[...].astype(BF16)
    q_ref[...] = _dot(xb, w_ref[:, 0:dk])
    k_ref[...] = _dot(xb, w_ref[:, dk:2 * dk])
    v_ref[...] = _dot(xb, w_ref[:, 2 * dk:2 * dk + dv])
    rs_ref[...] = _silu(_dot(xb, w_ref[:, 2 * dk + dv:2 * dk + 2 * dv]))
    low = _dot(xb, w_ref[:, 2 * dk + 2 * dv:])
    la_ref[...] = _gla_log_decay(low, w2_ref, b2_ref)


def _gla_proj_sample(x, w, w2, b2, *, dk, dv):
    n, d = x.shape
    full = lambda s: pl.BlockSpec(s, lambda i: (0,) * len(s))
    return pl.pallas_call(
        functools.partial(_gla_proj_sample_kernel, dk=dk, dv=dv),
        grid=(1,),
        in_specs=[full((n, d)), full(w.shape), full(w2.shape), full(b2.shape)],
        out_specs=(full((n, dk)), full((n, dk)), full((n, dv)), full((n, dv)), full((n, dk))),
        out_shape=(jax.ShapeDtypeStruct((n, dk), F32), jax.ShapeDtypeStruct((n, dk), F32),
                   jax.ShapeDtypeStruct((n, dv), F32), jax.ShapeDtypeStruct((n, dv), F32),
                   jax.ShapeDtypeStruct((n, dk), F32)),
        compiler_params=_params(("arbitrary",)),
        name="gla_proj_sample",
    )(x, w, w2, b2)


def _gla_rec_kernel(qi_ref, ki_ref, ks_ref, v_ref, al_ref, o_ref, st_out_ref, st_ref):
    t = pl.program_id(2)

    @pl.when(t == 0)
    def _():
        st_ref[...] = jnp.zeros_like(st_ref)

    c = GLA_CHUNK
    causal = lax.broadcasted_iota(jnp.int32, (c, c), 0) >= lax.broadcasted_iota(jnp.int32, (c, c), 1)
    st = st_ref[...]
    for i in range(qi_ref.shape[0] // c):
        sl = slice(i * c, (i + 1) * c)
        qi, ki, ks, vv = qi_ref[sl, :], ki_ref[sl, :], ks_ref[sl, :], v_ref[sl, :]
        att = jnp.where(causal, _dot_nt(qi, ki), 0.0)
        o = _dot(att.astype(BF16), vv) + _dot_nt(qi, st.astype(BF16))
        st = al_ref[i:i + 1, :] * st + _dot_tn(vv, ks)
        o_ref[sl, :] = o * lax.rsqrt(jnp.mean(o * o, axis=-1, keepdims=True) + LN_EPS)
    st_ref[...] = st

    @pl.when(t == pl.num_programs(2) - 1)
    def _():
        st_out_ref[...] = st


def _gla_rec(qi, ki, ks, v, al, *, batch, seq, dkh, dvh):
    ct = 256
    nct = seq // ct
    cpt = ct // GLA_CHUNK
    tok = lambda b, h, t: (b * nct + t, h)
    return pl.pallas_call(
        _gla_rec_kernel,
        grid=(batch, GLA_H, nct),
        in_specs=[pl.BlockSpec((ct, dkh), tok), pl.BlockSpec((ct, dkh), tok), pl.BlockSpec((ct, dkh), tok),
                  pl.BlockSpec((ct, dvh), tok), pl.BlockSpec((cpt, dkh), tok)],
        out_specs=(pl.BlockSpec((ct, dvh), tok),
                   pl.BlockSpec((None, None, dvh, dkh), lambda b, h, t: (b, h, 0, 0))),
        out_shape=(jax.ShapeDtypeStruct((batch * seq, GLA_H * dvh), F32),
                   jax.ShapeDtypeStruct((batch, GLA_H, dvh, dkh), F32)),
        scratch_shapes=[pltpu.VMEM((dvh, dkh), F32)],
        compiler_params=_params(("arbitrary", "arbitrary", "arbitrary")),
        name="gla_rec",
    )(qi, ki, ks, v, al)


def _gla_step_kernel(q_ref, k_ref, la_ref, v_ref, s0_ref, o_ref, s_ref, *, scale):
    la = la_ref[...]
    a = jnp.exp(la)
    k = k_ref[...]
    qi = q_ref[...] * a * scale
    ki = k * jnp.exp(-la)
    v = v_ref[...]
    s0 = s0_ref[...]
    att = jnp.sum(qi * ki, axis=0, keepdims=True)
    o = att * v + jnp.sum(qi * s0, axis=0, keepdims=True)
    s_ref[...] = a * s0 + k * v
    o_ref[...] = o * lax.rsqrt(jnp.mean(o * o, axis=-1, keepdims=True) + LN_EPS)


def _gla_step(q, k, la, v, s0, *, scale):
    bsz, h, dkh, dvh = s0.shape
    col = pl.BlockSpec((None, None, dkh, 1), lambda b, i: (b, i, 0, 0))
    rowv = pl.BlockSpec((None, None, 1, dvh), lambda b, i: (b, i, 0, 0))
    mat = pl.BlockSpec((None, None, dkh, dvh), lambda b, i: (b, i, 0, 0))
    return pl.pallas_call(
        functools.partial(_gla_step_kernel, scale=scale),
        grid=(bsz, h),
        in_specs=[col, col, col, rowv, mat],
        out_specs=(rowv, mat),
        out_shape=(jax.ShapeDtypeStruct((bsz, h, 1, dvh), F32),
                   jax.ShapeDtypeStruct((bsz, h, dkh, dvh), F32)),
        compiler_params=_params(("arbitrary", "arbitrary")),
        name="gla_step",
    )(q, k, la, v, s0)


def _rope_tables(pos):
    half = NSA_D // 2
    inv = ROPE_THETA ** (-jnp.arange(half, dtype=F32) / half)
    ang = pos.astype(F32)[:, None] * inv[None, :]
    cos, sin = jnp.cos(ang), jnp.sin(ang)
    zero = jnp.zeros_like(sin)
    cos_t = jnp.tile(cos, (1, LANES // half))
    sin_lo = jnp.tile(jnp.concatenate([-sin, zero], axis=1), (1, LANES // NSA_D))
    sin_hi = jnp.tile(jnp.concatenate([zero, sin], axis=1), (1, LANES // NSA_D))
    return cos_t, sin_lo, sin_hi


def _prep_nsa_weights(w_in, b_gate, pe, w1, w2):
    d = w_in.shape[0]
    qc = NSA_H * NSA_D
    kc = NSA_G * NSA_D
    gate_off = qc + 6 * kc
    wg = w_in[:, gate_off:].reshape(d, 3, NSA_G, NSA_R).transpose(0, 2, 1, 3).reshape(d, NSA_G, 3 * NSA_R)
    wg = jnp.pad(wg, ((0, 0), (0, 0), (0, LANES - 3 * NSA_R))).reshape(d, NSA_G * LANES)
    bg = b_gate.reshape(3, NSA_G, NSA_R).transpose(1, 0, 2).reshape(NSA_G, 3 * NSA_R)
    bg = jnp.pad(bg, ((0, 0), (0, LANES - 3 * NSA_R))).reshape(1, NSA_G * LANES)
    w = jnp.concatenate([w_in[:, :gate_off], wg], axis=1).astype(BF16)
    eye = jnp.eye(NSA_G, dtype=F32)
    pe_x = jnp.tile(pe.transpose(1, 0, 2), (1, 1, NSA_G))
    w1_bd = jnp.einsum('lcde,gh->clgdhe', w1, eye).reshape(2, CMP_BLK, kc, kc).astype(BF16)
    w2_dup = jnp.concatenate([w2, w2], axis=-1)
    w2_bd = jnp.einsum('cef,gh->cgehf', w2_dup, eye).reshape(2, kc, 2 * kc).astype(BF16)
    return w, bg, pe_x, w1_bd, w2_bd


def _prep_vt_weights(w_in):
    qc = NSA_H * NSA_D
    kc = NSA_G * NSA_D
    out = []
    for branch in (1, 2):
        lo = qc + branch * 2 * kc + kc
        out.append(w_in[:, lo:lo + kc].T)
    return jnp.stack(out).astype(BF16)


def _prep_page_compress_weights(pe, w1, w2):
    nd = NSA_D // 2
    eye = jnp.eye(PAGE // CMP_BLK, dtype=F32)
    pe_n = jnp.broadcast_to(pe.transpose(1, 2, 0).reshape(2, nd, 2, 1, CMP_BLK),
                            (2, nd, 2, PAGE // CMP_BLK, CMP_BLK)).reshape(2, nd, 2 * PAGE)
    w1c = w1.transpose(1, 2, 0, 3).reshape(2, nd, 2, CMP_BLK, NSA_D)
    w1_n = jnp.einsum('cqple,hk->cqphlke', w1c, eye).reshape(2, nd, 2 * PAGE, PAGE).astype(BF16)
    w2_n = jnp.einsum('cef,hk->chekf', w2, eye).reshape(2, PAGE, PAGE).astype(BF16)
    return pe_n, w1_n, w2_n


def kernel(x_prompt, x_sample, cache_cmp_kv, cache_slc_kv, cache_win_kv, state_gla, page_table,
           nsa_w_in, nsa_b_gate, nsa_pe_cmp, nsa_w_cmp1, nsa_w_cmp2, nsa_w_out,
           gla_w_in, gla_w_gate2, gla_b_gate2, gla_norm_g, gla_w_out,
           ffn_w_gu, ffn_w_down, ln_g, ln_b):
    batch, seq, d = x_prompt.shape
    bsz, dec_seq, _ = x_sample.shape
    depth = ffn_w_gu.shape[0]
    n_pool = cache_cmp_kv.shape[1]
    n_pages = page_table.shape[1]
    past_len = n_pages * PAGE
    kc = NSA_G * NSA_D
    assert dec_seq == 1 and depth == 2 and d == NSA_H * NSA_D
    assert seq % 512 == 0 and seq >= WINDOW and past_len % SEL_BLK == 0
    assert cache_win_kv.shape[2] == WINDOW and past_len // SEL_BLK >= N_SEL
    alpha = (2.0 * depth) ** 0.25
    tm = 512
    nbp = past_len // CMP_BLK

    xp = x_prompt.reshape(batch * seq, d)
    xs = x_sample.reshape(bsz, d)
    ln_g = ln_g.reshape(depth, 2, 1, d)
    ln_b = ln_b.reshape(depth, 2, 1, d)
    w_gu = ffn_w_gu.astype(BF16)
    w_dn = ffn_w_down.astype(BF16)

    w0, bg0, pe_x, w1_bd, w2_bd = _prep_nsa_weights(nsa_w_in[0], nsa_b_gate[0], nsa_pe_cmp[0],
                                                    nsa_w_cmp1[0], nsa_w_cmp2[0])
    w_out0 = nsa_w_out[0].astype(BF16)
    tabs_p = _rope_tables(jnp.arange(seq, dtype=jnp.int32))
    tabs_s = _rope_tables(jnp.full((bsz,), past_len, dtype=jnp.int32))

    seq_tiles = seq // tm
    (qraw, qrot, cmp_p, slc_p, win_p, gates_p, skd, wkd, svt, wvt) = _nsa_proj(
        xp, w0, bg0, *tabs_p, _prep_vt_weights(nsa_w_in[0]), tm=tm, seq_tiles=seq_tiles,
        win_rows=batch * WINDOW, win_map=lambda i: (i // seq_tiles, 0), attn_kt=ATTN_KT)
    ckv_p = _compress_prompt(cmp_p, pe_x, w1_bd, w2_bd, batch=batch, seq=seq)
    o_p = _nsa_attn_prompt(qraw, qrot, ckv_p, skd, svt, wkd, wvt, gates_p, batch=batch, seq=seq,
                           tq=ATTN_TQ, kt=ATTN_KT)
    yp = _proj_ln(o_p, w_out0, xp, ln_g[0, 0], ln_b[0, 0], tm=tm, alpha=alpha)

    (qraw_s, qrot_s, cmp_s, slc_s, win_s, gates_s) = _nsa_proj(
        xs, w0, bg0, *tabs_s, tm=bsz, seq_tiles=1, win_rows=bsz, win_map=lambda i: (i, 0))
    pages_fm = lambda c: jnp.transpose(c[0], (0, 2, 3, 4, 1))
    pe_n, w1_n, w2_n = _prep_page_compress_weights(nsa_pe_cmp[0], nsa_w_cmp1[0], nsa_w_cmp2[0])
    ckv_s = _compress_pages(pages_fm(cache_cmp_kv).reshape(n_pool, 2, kc, PAGE), page_table, pe_n, w1_n, w2_n)
    q16 = qraw_s.reshape(bsz, NSA_H, NSA_D)
    zeros16 = jnp.zeros_like(q16)
    q_e = jnp.concatenate([q16, zeros16], axis=-1)
    q_o = jnp.concatenate([zeros16, q16], axis=-1)
    ocmp_s, idx4 = _nsa_sample_select(q_e, q_o, ckv_s, bsz=bsz)
    idx = idx4[:, :, :N_SEL - 1, 0].reshape(bsz, NSA_G * (N_SEL - 1))
    q4r = qrot_s.reshape(bsz, NSA_G, NSA_R, NSA_D)
    q_cmp = jnp.einsum('bgrd,gh->bgrhd', q4r, jnp.eye(NSA_G, dtype=BF16)).reshape(bsz, NSA_H, kc)
    gates_s3 = gates_s.reshape(bsz, NSA_G, LANES)[:, :, :3 * NSA_R].reshape(bsz, NSA_G, 3, NSA_R)
    gates_s3 = gates_s3.transpose(0, 1, 3, 2).reshape(bsz, NSA_H, 3)
    o_s, win_buf_s = _nsa_sample_attn(
        idx, page_table, pages_fm(cache_slc_kv), qrot_s.reshape(bsz, NSA_H, NSA_D), q_cmp,
        slc_s.reshape(bsz, 1, 2 * kc), cache_win_kv[0].reshape(bsz, WINDOW, 2 * kc),
        win_s.reshape(bsz, 1, 2 * kc), gates_s3, ocmp_s)
    ys = _proj_ln(o_s.reshape(bsz, d), w_out0, xs, ln_g[0, 0], ln_b[0, 0], tm=bsz, alpha=alpha)

    yp = _ffn(yp, w_gu[0], w_dn[0], ln_g[0, 1], ln_b[0, 1], tm=tm, alpha=alpha)
    ys = _ffn(ys, w_gu[0], w_dn[0], ln_g[0, 1], ln_b[0, 1], tm=bsz, alpha=alpha)

    dk = gla_w_gate2.shape[2]
    dv = gla_w_out.shape[1]
    dkh, dvh = dk // GLA_H, dv // GLA_H
    gw = jnp.pad(gla_w_in[0], ((0, 0), (0, LANES - GLA_RANK))).astype(BF16)
    gw2 = jnp.pad(gla_w_gate2[0], ((0, LANES - GLA_RANK), (0, 0))).astype(BF16)
    gb2 = gla_b_gate2[0].reshape(1, dk)
    g_out = gla_w_out[0].astype(BF16)
    ng = gla_norm_g[0].reshape(1, dv)

    qi, ki, ks, vb, rs_p, al = _gla_proj_prompt(yp, gw, gw2, gb2, tm=tm, dk=dk, dv=dv)
    on_p, st_p = _gla_rec(qi, ki, ks, vb, al, batch=batch, seq=seq, dkh=dkh, dvh=dvh)
    yp = _gla_proj_ln(on_p, rs_p, ng, g_out, yp, ln_g[1, 0], ln_b[1, 0], tm=tm, alpha=alpha)

    q_s, k_s, v_s, rs_s, la_s = _gla_proj_sample(ys, gw, gw2, gb2, dk=dk, dv=dv)
    colv = lambda a: a.reshape(bsz, GLA_H, dkh, 1)
    on_s, st_s = _gla_step(colv(q_s), colv(k_s), colv(la_s), v_s.reshape(bsz, GLA_H, 1, dvh),
                           state_gla[0], scale=dkh ** -0.5)
    ys = _gla_proj_ln(on_s.reshape(bsz, dv), rs_s, ng, g_out, ys, ln_g[1, 0], ln_b[1, 0], tm=bsz, alpha=alpha)

    yp = _ffn(yp, w_gu[1], w_dn[1], ln_g[1, 1], ln_b[1, 1], tm=tm, alpha=alpha)
    ys = _ffn(ys, w_gu[1], w_dn[1], ln_g[1, 1], ln_b[1, 1], tm=bsz, alpha=alpha)

    kv6 = lambda a, rows: a.reshape(1, rows[0], rows[1], 2, NSA_G, NSA_D)
    return (yp.reshape(batch, seq, d), ys.reshape(bsz, 1, d),
            kv6(cmp_p, (batch, seq)), kv6(slc_p, (batch, seq)), kv6(win_p, (batch, WINDOW)),
            jnp.swapaxes(st_p, 2, 3)[None],
            kv6(cmp_s, (bsz, 1)), kv6(slc_s, (bsz, 1)), kv6(win_buf_s, (bsz, WINDOW)),
            st_s[None])
```

```python
import functools

import jax
import jax.numpy as jnp
from jax import lax
from jax.experimental import pallas as pl
from jax.experimental.pallas import tpu as pltpu

PAGE = 128
NSA_H = 16
NSA_D = 64
NSA_G = 4
NSA_R = NSA_H // NSA_G
CMP_BLK = 64
SEL_BLK = 64
SEL_SHIFT = 6
R_SHIFT = 2
LANE_SHIFT = 7
N_SEL = 16
WINDOW = 512
ATTN_SCALE = NSA_D ** -0.5
LOG2E = 1.4426950408889634
ROPE_THETA = 10000.0
NEG_INF = -1e30
SEL_FORCE = 1e4

GLA_H = 4
GLA_RANK = 16
GLA_TAU = 16.0
GLA_CHUNK = 32
GLA_CHUNK_SHIFT = 5
LN_EPS = 1e-5

LANES = 128
HALF = LANES // 2
ONES_ROWS = 16
ATTN_TQ = 128
ATTN_KT = 256
ATTN_UNROLL = 4
VMEM_LIMIT = 56 * 1024 * 1024

F32 = jnp.float32
BF16 = jnp.bfloat16

NT_DIMS = (((1,), (1,)), ((), ()))
TN_DIMS = (((0,), (0,)), ((), ()))


def _params(sem):
    return pltpu.CompilerParams(dimension_semantics=sem, vmem_limit_bytes=VMEM_LIMIT)


def _dot(a, b):
    return jnp.dot(a, b, preferred_element_type=F32)


def _dot_nt(a, b):
    return lax.dot_general(a, b, NT_DIMS, preferred_element_type=F32)


def _dot_tn(a, b):
    return lax.dot_general(a, b, TN_DIMS, preferred_element_type=F32)


def _layer_norm(z, g, b):
    mu = jnp.mean(z, axis=-1, keepdims=True)
    zc = z - mu
    var = jnp.mean(zc * zc, axis=-1, keepdims=True)
    return zc * lax.rsqrt(var + LN_EPS) * g + b


def _silu(x):
    return x * jax.nn.sigmoid(x)


def _rope(x, cos_t, sin_lo, sin_hi):
    out = []
    for p in range(x.shape[1] // LANES):
        blk = x[:, p * LANES:(p + 1) * LANES]
        x_up = pltpu.roll(blk, LANES - NSA_D // 2, 1)
        x_dn = pltpu.roll(blk, NSA_D // 2, 1)
        out.append(blk * cos_t + x_up * sin_lo + x_dn * sin_hi)
    return out[0] if len(out) == 1 else jnp.concatenate(out, axis=1)


def _dup_heads(x):
    rows, c = x.shape
    lane = lax.broadcasted_iota(jnp.int32, (rows, LANES), 1)
    lo = lane < HALF
    out = []
    for p in range(c // LANES):
        blk = x[:, p * LANES:(p + 1) * LANES]
        sw = pltpu.roll(blk, HALF, 1)
        out.append(jnp.where(lo, blk, sw))
        out.append(jnp.where(lo, sw, blk))
    return jnp.concatenate(out, axis=1)


def _nsa_proj_kernel(x_ref, w_ref, bg_ref, cos_ref, slo_ref, shi_ref, *refs, attn_kt):
    qc = NSA_H * NSA_D
    kc = NSA_G * NSA_D
    if attn_kt is None:
        qraw_ref, qrot_ref, cmp_ref, slc_ref, win_ref, gates_ref = refs
    else:
        (wvt_ref, qraw_ref, qrot_ref, cmp_ref, slc_ref, win_ref, gates_ref,
         skd_ref, wkd_ref, svt_ref, wvt_out_ref) = refs
    xb = x_ref[...].astype(BF16)
    cos_t, slo, shi = cos_ref[...], slo_ref[...], shi_ref[...]

    q = _dot(xb, w_ref[:, 0:qc])
    qraw_ref[...] = (q * ATTN_SCALE).astype(BF16)
    qrot_ref[...] = (_rope(q, cos_t, slo, shi) * (ATTN_SCALE * LOG2E)).astype(BF16)

    cmp_ref[...] = _dot(xb, w_ref[:, qc:qc + 2 * kc])

    slc = _dot(xb, w_ref[:, qc + 2 * kc:qc + 4 * kc])
    sk = _rope(slc[:, :kc], cos_t, slo, shi)
    slc_ref[:, :kc] = sk
    slc_ref[:, kc:] = slc[:, kc:]

    win = _dot(xb, w_ref[:, qc + 4 * kc:qc + 6 * kc])
    wk = _rope(win[:, :kc], cos_t, slo, shi)
    win_ref[:, :kc] = wk
    win_ref[:, kc:] = win[:, kc:]

    gz = _dot(xb, w_ref[:, qc + 6 * kc:qc + 6 * kc + NSA_G * LANES]) + bg_ref[...]
    gates_ref[...] = jax.nn.sigmoid(gz)

    if attn_kt is not None:
        skd_ref[...] = _dup_heads(sk).astype(BF16)
        wkd_ref[...] = _dup_heads(wk).astype(BF16)
        for src, dst in ((0, svt_ref), (1, wvt_out_ref)):
            vt = _dot_nt(wvt_ref[src], xb).astype(BF16)
            for t in range(x_ref.shape[0] // attn_kt):
                dst[t] = vt[:, t * attn_kt:(t + 1) * attn_kt]


def _nsa_proj(x, w, bg, cos_t, slo, shi, wvt=None, *, tm, seq_tiles, win_rows, win_map, attn_kt=None):
    n, d = x.shape
    qc = NSA_H * NSA_D
    kc2 = 2 * NSA_G * NSA_D
    kd = NSA_G * LANES
    row = lambda i: (i, 0)
    tab = lambda i: (i % seq_tiles, 0)
    const = lambda i: (0, 0)
    in_specs = [pl.BlockSpec((tm, d), row), pl.BlockSpec(w.shape, const), pl.BlockSpec(bg.shape, const),
                pl.BlockSpec((tm, LANES), tab), pl.BlockSpec((tm, LANES), tab), pl.BlockSpec((tm, LANES), tab)]
    out_shape = [jax.ShapeDtypeStruct((n, qc), BF16), jax.ShapeDtypeStruct((n, qc), BF16),
                 jax.ShapeDtypeStruct((n, kc2), F32), jax.ShapeDtypeStruct((n, kc2), F32),
                 jax.ShapeDtypeStruct((win_rows, kc2), F32), jax.ShapeDtypeStruct((n, kd), F32)]
    out_specs = [pl.BlockSpec((tm, qc), row), pl.BlockSpec((tm, qc), row),
                 pl.BlockSpec((tm, kc2), row), pl.BlockSpec((tm, kc2), row),
                 pl.BlockSpec((tm, kc2), win_map), pl.BlockSpec((tm, kd), row)]
    args = [x, w, bg, cos_t, slo, shi]
    if attn_kt is not None:
        assert tm % attn_kt == 0
        in_specs.append(pl.BlockSpec(wvt.shape, lambda i: (0, 0, 0)))
        args.append(wvt)
        tiles = tm // attn_kt
        out_shape += [jax.ShapeDtypeStruct((n, kd), BF16), jax.ShapeDtypeStruct((n, kd), BF16),
                      jax.ShapeDtypeStruct((n // attn_kt, kc2 // 2, attn_kt), BF16),
                      jax.ShapeDtypeStruct((n // attn_kt, kc2 // 2, attn_kt), BF16)]
        out_specs += [pl.BlockSpec((tm, kd), row), pl.BlockSpec((tm, kd), row),
                      pl.BlockSpec((tiles, kc2 // 2, attn_kt), lambda i: (i, 0, 0)),
                      pl.BlockSpec((tiles, kc2 // 2, attn_kt), lambda i: (i, 0, 0))]
    return pl.pallas_call(
        functools.partial(_nsa_proj_kernel, attn_kt=attn_kt),
        grid=(n // tm,),
        in_specs=in_specs,
        out_specs=tuple(out_specs),
        out_shape=tuple(out_shape),
        compiler_params=_params(("arbitrary",)),
        name="nsa_proj",
    )(*args)


def _compress_body(tok_refs, pe_ref, w1_ref, w2_ref, out_ref):
    m = tok_refs[0].shape[0] // CMP_BLK
    acc = jnp.zeros((m, w1_ref.shape[2]), F32)
    for l in range(CMP_BLK):
        xl = jnp.concatenate([r[pl.ds(l, m, stride=CMP_BLK), :] for r in tok_refs], axis=1)
        xl = xl + pe_ref[l:l + 1, :]
        acc = acc + _dot(xl.astype(BF16), w1_ref[l])
    h = _silu(acc)
    out_ref[...] = _dot(h.astype(BF16), w2_ref[...])


def _compress_prompt_kernel(tok0_ref, tok1_ref, pe_ref, w1_ref, w2_ref, out_ref):
    _compress_body((tok0_ref, tok1_ref), pe_ref, w1_ref, w2_ref, out_ref)


def _compress_prompt(cmp_kv, pe, w1, w2, *, batch, seq):
    kc = NSA_G * NSA_D
    nb = seq // CMP_BLK
    return pl.pallas_call(
        _compress_prompt_kernel,
        grid=(2, batch),
        in_specs=[pl.BlockSpec((seq, LANES), lambda c, b: (b, 2 * c)),
                  pl.BlockSpec((seq, LANES), lambda c, b: (b, 2 * c + 1)),
                  pl.BlockSpec((None, CMP_BLK, kc), lambda c, b: (c, 0, 0)),
                  pl.BlockSpec((None, CMP_BLK, kc, kc), lambda c, b: (c, 0, 0, 0)),
                  pl.BlockSpec((None, kc, 2 * kc), lambda c, b: (c, 0, 0))],
        out_specs=pl.BlockSpec((None, nb, 2 * kc), lambda c, b: (c, b, 0)),
        out_shape=jax.ShapeDtypeStruct((2, batch * nb, 2 * kc), F32),
        compiler_params=_params(("arbitrary", "arbitrary")),
        name="compress_prompt",
    )(cmp_kv, cmp_kv, pe, w1, w2)


def _compress_pages_kernel(pt_ref, *refs, pages):
    del pt_ref
    page_refs = refs[:pages]
    pe_ref, w1_ref, w2_ref, out_ref, buf_ref = refs[pages:]
    rows_pp = NSA_G * NSA_D
    for k in range(pages):
        buf_ref[k * rows_pp:(k + 1) * rows_pp, :] = page_refs[k][...]
    m = pages * NSA_G
    acc = jnp.zeros((m, LANES), F32)
    for dd in range(NSA_D // 2):
        x = jnp.concatenate([buf_ref[pl.ds(2 * dd, m, stride=NSA_D), :],
                             buf_ref[pl.ds(2 * dd + 1, m, stride=NSA_D), :]], axis=1)
        x = x + pe_ref[dd:dd + 1, :]
        acc = acc + _dot(x.astype(BF16), w1_ref[dd])
    out_ref[...] = _dot(_silu(acc).astype(BF16), w2_ref[...])


def _compress_pages(cache, page_table, pe, w1, w2):
    bsz, n_pages = page_table.shape
    pages = min(64, n_pages)
    groups = n_pages // pages
    gd = NSA_G * NSA_D
    assert PAGE == 2 * CMP_BLK and PAGE == LANES

    def page_map(k):
        return lambda c, b, h, pt: (pt[b, h * pages + k], c, 0, 0)

    in_specs = [pl.BlockSpec((None, None, gd, PAGE), page_map(k)) for k in range(pages)]
    in_specs += [pl.BlockSpec((None,) + pe.shape[1:], lambda c, b, h, pt: (c, 0, 0)),
                 pl.BlockSpec((None,) + w1.shape[1:], lambda c, b, h, pt: (c, 0, 0, 0)),
                 pl.BlockSpec((None,) + w2.shape[1:], lambda c, b, h, pt: (c, 0, 0))]
    grid_spec = pltpu.PrefetchScalarGridSpec(
        num_scalar_prefetch=1,
        grid=(2, bsz, groups),
        in_specs=in_specs,
        out_specs=pl.BlockSpec((None, pages * NSA_G, LANES), lambda c, b, h, pt: (c, b * groups + h, 0)),
        scratch_shapes=[pltpu.VMEM((pages * gd, PAGE), F32)],
    )
    return pl.pallas_call(
        functools.partial(_compress_pages_kernel, pages=pages),
        grid_spec=grid_spec,
        out_shape=jax.ShapeDtypeStruct((2, bsz * n_pages * NSA_G, LANES), F32),
        compiler_params=_params(("arbitrary", "arbitrary", "arbitrary")),
        name="compress_pages",
    )(page_table, *([cache] * pages), pe, w1, w2)


def _softmax_tile(carry, s_t, vt_tile, bias, tq):
    ms, acc = carry
    new_ms, ps, alphas = [], [], []
    for r in range(NSA_R):
        sb = s_t[:, r * tq:(r + 1) * tq] + bias
        m_new = jnp.maximum(ms[r], jnp.max(sb, axis=0, keepdims=True))
        ps.append(jnp.exp2(sb - m_new).astype(BF16))
        alphas.append(jnp.exp2(ms[r] - m_new))
        new_ms.append(m_new)
    acc = jnp.concatenate(alphas, axis=1) * acc + _dot(vt_tile, jnp.concatenate(ps, axis=1))
    return tuple(new_ms), acc


def _rank_blocks(score_ref, rank_ref, last_blk):
    nb, tq = score_ref.shape
    sub = lax.broadcasted_iota(jnp.int32, (8, tq), 0)
    rank_ref[...] = jnp.zeros((nb, tq), F32)
    for gi in range(nb // 8):
        @pl.when(gi * 8 <= last_blk)
        def _(gi=gi):
            rows8 = score_ref[gi * 8:(gi + 1) * 8, :]
            tops = [jnp.broadcast_to(rows8[u:u + 1, :], (8, tq)) for u in range(8)]
            for v in range(nb // 8):
                sv = score_ref[v * 8:(v + 1) * 8, :]
                cnt = jnp.zeros((8, tq), F32)
                for u in range(8):
                    if v > gi:
                        before = tops[u] >= sv
                    elif v < gi:
                        before = tops[u] > sv
                    else:
                        before = (tops[u] > sv) | ((tops[u] == sv) & (sub > u))
                    cnt = cnt + jnp.where(before, 1.0, 0.0)
                rank_ref[v * 8:(v + 1) * 8, :] += cnt


def _nsa_attn_kernel(qraw_ref, qrot_ref, ck_ref, cv_ref, sk_ref, svt_ref, wk_ref, wvt_ref,
                     gates_ref, out_ref, score_ref, rank_ref, *, tq, kt, n_sel, unroll):
    nb = ck_ref.shape[0]
    rows = NSA_R * tq
    q0 = pl.program_id(2) * tq

    lane = lax.broadcasted_iota(jnp.int32, (tq, LANES), 1)
    lo = lane < HALF

    def stack(ref):
        qa = ref[:, 0:LANES].astype(F32)
        qb = ref[:, LANES:2 * LANES].astype(F32)
        parts = [jnp.where(lo, qa, 0.0), jnp.where(lo, 0.0, qa),
                 jnp.where(lo, qb, 0.0), jnp.where(lo, 0.0, qb)]
        return jnp.concatenate(parts, axis=0).astype(BF16)

    q_raw = stack(qraw_ref)
    q_rot = stack(qrot_ref)
    ck = ck_ref[...].astype(BF16)
    cv = cv_ref[...].astype(BF16)

    st = _dot_nt(ck, q_raw)
    t_c = q0 + (lax.broadcasted_iota(jnp.int32, (nb, rows), 1) & (tq - 1))
    n_c = lax.broadcasted_iota(jnp.int32, (nb, rows), 0)
    mask_t = (n_c + 1) * CMP_BLK - 1 <= t_c
    st = jnp.where(mask_t, st, NEG_INF)
    pt = jnp.where(mask_t, jnp.exp(st - jnp.max(st, axis=0, keepdims=True)), 0.0)
    pt = pt / jnp.maximum(jnp.sum(pt, axis=0, keepdims=True), 1e-30)
    o_cmp = _dot_tn(cv[:, 0:NSA_D], pt.astype(BF16))

    imp = pt[:, 0:tq]
    for r in range(1, NSA_R):
        imp = imp + pt[:, r * tq:(r + 1) * tq]
    blk = lax.broadcasted_iota(jnp.int32, (nb, tq), 0)
    cur = (q0 + lax.broadcasted_iota(jnp.int32, (nb, tq), 1)) >> SEL_SHIFT
    forced = (blk == 0) | (blk == cur) | (blk == cur - 1)
    valid = blk <= cur
    score_ref[...] = jnp.where(forced, SEL_FORCE, jnp.where(valid, imp, -1.0))
    _rank_blocks(score_ref, rank_ref, (q0 + tq - 1) >> SEL_SHIFT)
    score_ref[...] = jnp.where((rank_ref[...] < n_sel) & valid, 0.0, NEG_INF)

    key_row = lax.broadcasted_iota(jnp.int32, (kt, tq), 0)
    t_lane = q0 + lax.broadcasted_iota(jnp.int32, (kt, tq), 1)
    ones = jnp.ones((ONES_ROWS, kt), BF16)

    init = (tuple(jnp.full((1, tq), NEG_INF, F32) for _ in range(NSA_R)),
            jnp.zeros((NSA_D + ONES_ROWS, rows), F32))

    hi = (q0 + tq + kt - 1) // kt
    n_tiles = sk_ref.shape[0] // kt

    def tile_loop(lo_tile, k_ref, vt_ref, make_bias, unroll):
        def body(i, carry):
            staged = []
            for u in range(unroll):
                j = lo_tile + i * unroll + u
                jc = jnp.minimum(j, n_tiles - 1)
                k0 = pl.multiple_of(jc * kt, kt)
                k_pos = jnp.where(j < hi, k0, 2 * n_tiles * kt) + key_row
                staged.append((_dot_nt(k_ref[pl.ds(k0, kt), :], q_rot), make_bias(k0, k_pos), jc))
            for s_t, bias, jc in staged:
                vt = jnp.concatenate([vt_ref[jc], ones], axis=0)
                carry = _softmax_tile(carry, s_t, vt, bias, tq)
            return carry
        return lax.fori_loop(0, (hi - lo_tile + unroll - 1) // unroll, body, init)[1]

    def sel_bias(k0, k_pos):
        blk0 = k0 // SEL_BLK
        picked = jnp.concatenate(
            [jnp.broadcast_to(score_ref[pl.ds(blk0 + i, 1), :], (SEL_BLK, tq)) for i in range(kt // SEL_BLK)],
            axis=0)
        return jnp.where(k_pos <= t_lane, picked, NEG_INF)

    acc_s = tile_loop(0, sk_ref, svt_ref, sel_bias, unroll)

    def win_bias(k0, k_pos):
        diff = t_lane - k_pos
        return jnp.where((diff >= 0) & (diff <= WINDOW), 0.0, NEG_INF)

    acc_w = tile_loop(jnp.maximum(q0 - WINDOW, 0) // kt, wk_ref, wvt_ref, win_bias, WINDOW // kt + 1)

    gt = gates_ref[...].T
    o_s = acc_s[0:NSA_D, :] / jnp.maximum(acc_s[NSA_D:NSA_D + 1, :], 1e-30)
    o_w = acc_w[0:NSA_D, :] / jnp.maximum(acc_w[NSA_D:NSA_D + 1, :], 1e-30)
    heads = []
    for r in range(NSA_R):
        sl = slice(r * tq, (r + 1) * tq)
        heads.append(gt[r:r + 1, :] * o_cmp[:, sl] + gt[NSA_R + r:NSA_R + r + 1, :] * o_s[:, sl]
                     + gt[2 * NSA_R + r:2 * NSA_R + r + 1, :] * o_w[:, sl])
    for pair in range(NSA_R // 2):
        both = jnp.concatenate(heads[2 * pair:2 * pair + 2], axis=0)
        out_ref[:, pair * LANES:(pair + 1) * LANES] = both.T.astype(BF16)


def _nsa_attn_prompt(qraw, qrot, ckv, skd, svt, wkd, wvt, gates, *, batch, seq, tq, kt):
    assert tq == LANES and kt % tq == 0 and WINDOW % kt == 0 and seq % kt == 0
    nqt = seq // tq
    nb = seq // CMP_BLK
    gc = NSA_R * NSA_D
    q_map = lambda b, g, t: (b * nqt + t, g)
    kv_map = lambda b, g, t: (b, g)
    vt_map = lambda b, g, t: (b, g, 0)
    return pl.pallas_call(
        functools.partial(_nsa_attn_kernel, tq=tq, kt=kt, n_sel=min(N_SEL, seq // SEL_BLK),
                          unroll=ATTN_UNROLL),
        grid=(batch, NSA_G, nqt),
        in_specs=[pl.BlockSpec((tq, gc), q_map), pl.BlockSpec((tq, gc), q_map),
                  pl.BlockSpec((None, nb, LANES), lambda b, g, t: (0, b, g)),
                  pl.BlockSpec((None, nb, LANES), lambda b, g, t: (1, b, g)),
                  pl.BlockSpec((seq, LANES), kv_map), pl.BlockSpec((seq // kt, NSA_D, kt), vt_map),
                  pl.BlockSpec((seq, LANES), kv_map), pl.BlockSpec((seq // kt, NSA_D, kt), vt_map),
                  pl.BlockSpec((tq, LANES), q_map)],
        out_specs=pl.BlockSpec((tq, gc), q_map),
        out_shape=jax.ShapeDtypeStruct((batch * seq, NSA_H * NSA_D), BF16),
        scratch_shapes=[pltpu.VMEM((nb, tq), F32), pltpu.VMEM((nb, tq), F32)],
        compiler_params=_params(("arbitrary", "arbitrary", "arbitrary")),
        name="nsa_attn_prompt",
    )(qraw, qrot, ckv, ckv, skd, svt, wkd, wvt, gates)


def _nsa_sample_select_kernel(qe_ref, qo_ref, ck_ref, cv_ref, ocmp_ref, idx_ref):
    npg = ck_ref.shape[0] // NSA_G
    nbp = 2 * npg
    qe, qo = qe_ref[...], qo_ref[...]
    head = lax.broadcasted_iota(jnp.int32, (NSA_H, LANES), 0)

    def block_of(j):
        return jnp.where(j < npg, 2 * j, 2 * (j - npg) + 1)

    n_row = block_of(lax.broadcasted_iota(jnp.int32, (1, nbp), 1))
    n_col = block_of(lax.broadcasted_iota(jnp.int32, (nbp, 1), 0))
    forced = (n_row == 0) | (n_row == nbp - 1)
    diag = lax.broadcasted_iota(jnp.int32, (nbp, nbp), 0) == lax.broadcasted_iota(jnp.int32, (nbp, nbp), 1)
    k_idx = lax.broadcasted_iota(jnp.int32, (N_SEL, nbp), 0).astype(F32)
    n_f = jnp.broadcast_to(n_row, (N_SEL, nbp)).astype(F32)
    ocmp = jnp.zeros((NSA_H, LANES), F32)
    for g in range(NSA_G):
        ck = ck_ref[pl.ds(g, npg, stride=NSA_G), :].astype(BF16)
        cv = cv_ref[pl.ds(g, npg, stride=NSA_G), :].astype(BF16)
        s = jnp.concatenate([_dot_nt(qe, ck), _dot_nt(qo, ck)], axis=1)
        p = jnp.exp(s - jnp.max(s, axis=-1, keepdims=True))
        p = p / jnp.maximum(jnp.sum(p, axis=-1, keepdims=True), 1e-30)
        o_e = _dot(p[:, :npg].astype(BF16), cv)
        o_o = _dot(p[:, npg:].astype(BF16), cv)
        ocmp = jnp.where((head >> R_SHIFT) == g, o_e + pltpu.roll(o_o, HALF, 1), ocmp)

        imp = p[NSA_R * g:NSA_R * g + 1, :]
        for r in range(1, NSA_R):
            imp = imp + p[NSA_R * g + r:NSA_R * g + r + 1, :]
        row = jnp.where(forced, SEL_FORCE, imp)
        col = jnp.sum(jnp.where(diag, row, 0.0), axis=1, keepdims=True)
        beats = (col > row) | ((col == row) & (n_col < n_row))
        rank = jnp.sum(jnp.where(beats, 1.0, 0.0), axis=0, keepdims=True)
        onehot = jnp.where(rank == k_idx, n_f, 0.0)
        idx_ref[g] = jnp.sum(onehot, axis=1, keepdims=True).astype(jnp.int32)
    ocmp_ref[...] = ocmp


def _nsa_sample_select(q_e, q_o, ckv, *, bsz):
    rows = ckv.shape[1] // bsz
    return pl.pallas_call(
        _nsa_sample_select_kernel,
        grid=(bsz,),
        in_specs=[pl.BlockSpec((None, NSA_H, LANES), lambda b: (b, 0, 0)),
                  pl.BlockSpec((None, NSA_H, LANES), lambda b: (b, 0, 0)),
                  pl.BlockSpec((None, rows, LANES), lambda b: (0, b, 0)),
                  pl.BlockSpec((None, rows, LANES), lambda b: (1, b, 0))],
        out_specs=(pl.BlockSpec((None, NSA_H, LANES), lambda b: (b, 0, 0)),
                   pl.BlockSpec((None, NSA_G, N_SEL, 1), lambda b: (b, 0, 0, 0))),
        out_shape=(jax.ShapeDtypeStruct((bsz, NSA_H, LANES), F32),
                   jax.ShapeDtypeStruct((bsz, NSA_G, N_SEL, 1), jnp.int32)),
        compiler_params=_params(("arbitrary",)),
        name="nsa_sample_select",
    )(q_e, q_o, ckv, ckv)


def _nsa_sample_attn_kernel(idx_ref, pt_ref, *refs, n_past):
    del pt_ref
    slabs = refs[:NSA_G * n_past]
    (q64_ref, q_ref, snew_ref, wcache_ref, wnew_ref, gates_ref, ocmp_ref,
     out_ref, wout_ref, wbuf_ref) = refs[NSA_G * n_past:]
    kc = NSA_G * NSA_D
    b = pl.program_id(0)
    q64 = q64_ref[...]
    q = q_ref[...]
    head_row = lax.broadcasted_iota(jnp.int32, (NSA_H, kc), 0)
    head_lane = lax.broadcasted_iota(jnp.int32, (NSA_H, kc), 1)
    own = (head_lane >> SEL_SHIFT) == (head_row >> R_SHIFT)
    fold = jnp.where((lax.broadcasted_iota(jnp.int32, (kc, NSA_D), 0) & (NSA_D - 1))
                     == lax.broadcasted_iota(jnp.int32, (kc, NSA_D), 1), 1.0, 0.0)
    hi = lax.Precision.HIGHEST

    def rounded(x):
        return x.astype(BF16).astype(F32)

    snew = snew_ref[...]
    s_new = jnp.sum(q.astype(F32) * rounded(snew[:, :kc]), axis=-1, keepdims=True)
    v_new = jnp.dot(jnp.where(own, rounded(snew[:, kc:]), 0.0), fold, precision=hi,
                    preferred_element_type=F32)
    half_of_lane = lax.broadcasted_iota(jnp.int32, (1, PAGE), 1) >> SEL_SHIFT
    row64 = lax.broadcasted_iota(jnp.int32, (NSA_H, NSA_D), 0)
    f_sel = jnp.zeros((NSA_H, NSA_D), F32)
    for g in range(NSA_G):
        mine = slabs[g * n_past:(g + 1) * n_past]
        k_cat = jnp.concatenate([r[0] for r in mine], axis=1).astype(BF16)
        v_cat = jnp.concatenate([r[1] for r in mine], axis=1).astype(BF16)
        bias = jnp.concatenate(
            [jnp.where(half_of_lane == (idx_ref[b, g * n_past + k] & 1), 0.0, NEG_INF)
             for k in range(n_past)], axis=1)
        sb = _dot(q64, k_cat) + bias
        m = jnp.maximum(jnp.max(sb, axis=-1, keepdims=True), s_new)
        p = jnp.exp2(sb - m)
        p_new = jnp.exp2(s_new - m)
        l = jnp.sum(p, axis=-1, keepdims=True) + p_new
        o = _dot_nt(p.astype(BF16), v_cat) + rounded(p_new) * v_new
        f_sel = jnp.where((row64 >> R_SHIFT) == g, o / jnp.maximum(l, 1e-30), f_sel)

    wb = wcache_ref.shape[0]
    wbuf_ref[0:wb, :] = wcache_ref[...]
    wbuf_ref[wb:wb + 1, :] = wnew_ref[...]
    wbuf_ref[wb + 1:, :] = jnp.zeros((wbuf_ref.shape[0] - wb - 1, 2 * kc), F32)
    s = _dot_nt(q, wbuf_ref[:, :kc].astype(BF16))
    ok = lax.broadcasted_iota(jnp.int32, s.shape, 1) < wb + 1
    s = jnp.where(ok, s, NEG_INF)
    p = jnp.where(ok, jnp.exp2(s - jnp.max(s, axis=-1, keepdims=True)), 0.0)
    p = p / jnp.maximum(jnp.sum(p, axis=-1, keepdims=True), 1e-30)
    o_win = _dot(p.astype(BF16), wbuf_ref[:, kc:].astype(BF16))
    f_win = jnp.dot(jnp.where(own, o_win, 0.0), fold, precision=hi, preferred_element_type=F32)
    wout_ref[...] = wbuf_ref[1:wb + 1, :]

    gts = gates_ref[...]
    out_ref[...] = (gts[:, 0:1] * ocmp_ref[:, 0:NSA_D] + gts[:, 1:2] * f_sel + gts[:, 2:3] * f_win)


def _nsa_sample_attn(idx, page_table, slc_cache, q64, q_cmp, slc_new, win_cache, win_new, gates, ocmp):
    bsz = page_table.shape[0]
    n_past = idx.shape[1] // NSA_G
    kc = NSA_G * NSA_D
    wb = win_cache.shape[1]
    assert PAGE == 2 * SEL_BLK

    def slab_map(j):
        def f(b, idx_ref, pt_ref):
            return (pt_ref[b, idx_ref[b, j] // 2], 0, j // n_past, 0, 0)
        return f

    pad = lambda n: -(-n // LANES) * LANES
    in_specs = [pl.BlockSpec((None, 2, None, NSA_D, PAGE), slab_map(j)) for j in range(NSA_G * n_past)]
    in_specs += [pl.BlockSpec((None, NSA_H, NSA_D), lambda b, i, p: (b, 0, 0)),
                 pl.BlockSpec((None, NSA_H, kc), lambda b, i, p: (b, 0, 0)),
                 pl.BlockSpec((None, 1, 2 * kc), lambda b, i, p: (b, 0, 0)),
                 pl.BlockSpec((None, wb, 2 * kc), lambda b, i, p: (b, 0, 0)),
                 pl.BlockSpec((None, 1, 2 * kc), lambda b, i, p: (b, 0, 0)),
                 pl.BlockSpec((None, NSA_H, 3), lambda b, i, p: (b, 0, 0)),
                 pl.BlockSpec((None, NSA_H, LANES), lambda b, i, p: (b, 0, 0))]
    grid_spec = pltpu.PrefetchScalarGridSpec(
        num_scalar_prefetch=2,
        grid=(bsz,),
        in_specs=in_specs,
        out_specs=(pl.BlockSpec((None, NSA_H, NSA_D), lambda b, i, p: (b, 0, 0)),
                   pl.BlockSpec((None, wb, 2 * kc), lambda b, i, p: (b, 0, 0))),
        scratch_shapes=[pltpu.VMEM((pad(wb + 1), 2 * kc), F32)],
    )
    return pl.pallas_call(
        functools.partial(_nsa_sample_attn_kernel, n_past=n_past),
        grid_spec=grid_spec,
        out_shape=(jax.ShapeDtypeStruct((bsz, NSA_H, NSA_D), F32),
                   jax.ShapeDtypeStruct((bsz, wb, 2 * kc), F32)),
        compiler_params=_params(("arbitrary",)),
        name="nsa_sample_attn",
    )(idx, page_table, *([slc_cache] * (NSA_G * n_past)), q64, q_cmp, slc_new, win_cache, win_new, gates, ocmp)


def _proj_ln_kernel(a_ref, w_ref, x_ref, g_ref, b_ref, out_ref, *, alpha):
    y = _dot(a_ref[...].astype(BF16), w_ref[...])
    out_ref[...] = _layer_norm(alpha * x_ref[...] + y, g_ref[...], b_ref[...])


def _gla_proj_ln_kernel(o_ref, rs_ref, ng_ref, w_ref, x_ref, g_ref, b_ref, out_ref, *, alpha):
    a = (o_ref[...] * ng_ref[...] * rs_ref[...]).astype(BF16)
    y = _dot(a, w_ref[...])
    out_ref[...] = _layer_norm(alpha * x_ref[...] + y, g_ref[...], b_ref[...])


def _proj_ln(a, w, x, g, b, *, tm, alpha):
    n, d = x.shape
    row = lambda i: (i, 0)
    const = lambda i: (0, 0)
    return pl.pallas_call(
        functools.partial(_proj_ln_kernel, alpha=alpha),
        grid=(n // tm,),
        in_specs=[pl.BlockSpec((tm, a.shape[1]), row), pl.BlockSpec(w.shape, const),
                  pl.BlockSpec((tm, d), row), pl.BlockSpec((1, d), const), pl.BlockSpec((1, d), const)],
        out_specs=pl.BlockSpec((tm, d), row),
        out_shape=jax.ShapeDtypeStruct((n, d), F32),
        compiler_params=_params(("arbitrary",)),
        name="proj_ln",
    )(a, w, x, g, b)


def _gla_proj_ln(o, rs, ng, w, x, g, b, *, tm, alpha):
    n, d = x.shape
    row = lambda i: (i, 0)
    const = lambda i: (0, 0)
    return pl.pallas_call(
        functools.partial(_gla_proj_ln_kernel, alpha=alpha),
        grid=(n // tm,),
        in_specs=[pl.BlockSpec((tm, o.shape[1]), row), pl.BlockSpec((tm, o.shape[1]), row),
                  pl.BlockSpec((1, o.shape[1]), const), pl.BlockSpec(w.shape, const),
                  pl.BlockSpec((tm, d), row), pl.BlockSpec((1, d), const), pl.BlockSpec((1, d), const)],
        out_specs=pl.BlockSpec((tm, d), row),
        out_shape=jax.ShapeDtypeStruct((n, d), F32),
        compiler_params=_params(("arbitrary",)),
        name="gla_proj_ln",
    )(o, rs, ng, w, x, g, b)


def _ffn_kernel(x_ref, wg_ref, wu_ref, wd_ref, g_ref, b_ref, out_ref, acc_ref, xb_ref, *, alpha):
    k = pl.program_id(1)

    @pl.when(k == 0)
    def _():
        xb_ref[...] = x_ref[...].astype(BF16)
        acc_ref[...] = jnp.zeros_like(acc_ref)

    xb = xb_ref[...]
    h = _silu(_dot(xb, wg_ref[...])) * _dot(xb, wu_ref[...])
    acc_ref[...] += _dot(h.astype(BF16), wd_ref[...])

    @pl.when(k == pl.num_programs(1) - 1)
    def _():
        out_ref[...] = _layer_norm(alpha * x_ref[...] + acc_ref[...], g_ref[...], b_ref[...])


def _ffn(x, w_gu, w_down, g, b, *, tm, alpha):
    n, d = x.shape
    d_ff = w_down.shape[0]
    splits = 2
    tf = d_ff // splits
    assert tf * splits == d_ff and tf % LANES == 0
    return pl.pallas_call(
        functools.partial(_ffn_kernel, alpha=alpha),
        grid=(n // tm, splits),
        in_specs=[pl.BlockSpec((tm, d), lambda i, k: (i, 0)),
                  pl.BlockSpec((d, tf), lambda i, k: (0, k)),
                  pl.BlockSpec((d, tf), lambda i, k: (0, splits + k)),
                  pl.BlockSpec((tf, d), lambda i, k: (k, 0)),
                  pl.BlockSpec((1, d), lambda i, k: (0, 0)),
                  pl.BlockSpec((1, d), lambda i, k: (0, 0))],
        out_specs=pl.BlockSpec((tm, d), lambda i, k: (i, 0)),
        out_shape=jax.ShapeDtypeStruct((n, d), F32),
        scratch_shapes=[pltpu.VMEM((tm, d), F32), pltpu.VMEM((tm, d), BF16)],
        compiler_params=_params(("arbitrary", "arbitrary")),
        name="ffn",
    )(x, w_gu, w_gu, w_down, g, b)


def _gla_log_decay(low, w2_ref, b2_ref):
    z = _dot(low.astype(BF16), w2_ref[...]) + b2_ref[...]
    return jax.nn.log_sigmoid(z) / GLA_TAU


def _gla_proj_prompt_kernel(x_ref, w_ref, w2_ref, b2_ref, qi_ref, ki_ref, ks_ref, v_ref, rs_ref,
                            al_ref, *, dk, dv, scale):
    tm = x_ref.shape[0]
    xb = x_ref[...].astype(BF16)
    q = _dot(xb, w_ref[:, 0:dk])
    k = _dot(xb, w_ref[:, dk:2 * dk])
    v_ref[...] = _dot(xb, w_ref[:, 2 * dk:2 * dk + dv]).astype(BF16)
    rs_ref[...] = _silu(_dot(xb, w_ref[:, 2 * dk + dv:2 * dk + 2 * dv]))
    low = _dot(xb, w_ref[:, 2 * dk + 2 * dv:])
    la = _gla_log_decay(low, w2_ref, b2_ref)

    rin = lax.broadcasted_iota(jnp.int32, (tm, dk), 0) & (GLA_CHUNK - 1)
    lb = la
    s = 1
    while s < GLA_CHUNK:
        lb = lb + jnp.where(rin >= s, pltpu.roll(lb, s, 0), 0.0)
        s *= 2
    lb3 = lb.reshape(tm // GLA_CHUNK, GLA_CHUNK, dk)
    last = lb3[:, GLA_CHUNK - 1:GLA_CHUNK, :]
    qi_ref[...] = (q * jnp.exp(lb) * scale).astype(BF16)
    ki_ref[...] = (k * jnp.exp(-lb)).astype(BF16)
    ks_ref[...] = (k * jnp.exp(last - lb3).reshape(tm, dk)).astype(BF16)
    al_ref[...] = jnp.exp(last.reshape(tm // GLA_CHUNK, dk))


def _gla_proj_prompt(x, w, w2, b2, *, tm, dk, dv):
    n, d = x.shape
    row = lambda i: (i, 0)
    const = lambda i: (0, 0)
    return pl.pallas_call(
        functools.partial(_gla_proj_prompt_kernel, dk=dk, dv=dv, scale=(dk // GLA_H) ** -0.5),
        grid=(n // tm,),
        in_specs=[pl.BlockSpec((tm, d), row), pl.BlockSpec(w.shape, const),
                  pl.BlockSpec(w2.shape, const), pl.BlockSpec(b2.shape, const)],
        out_specs=(pl.BlockSpec((tm, dk), row), pl.BlockSpec((tm, dk), row), pl.BlockSpec((tm, dk), row),
                   pl.BlockSpec((tm, dv), row), pl.BlockSpec((tm, dv), row),
                   pl.BlockSpec((tm // GLA_CHUNK, dk), row)),
        out_shape=(jax.ShapeDtypeStruct((n, dk), BF16), jax.ShapeDtypeStruct((n, dk), BF16),
                   jax.ShapeDtypeStruct((n, dk), BF16), jax.ShapeDtypeStruct((n, dv), BF16),
                   jax.ShapeDtypeStruct((n, dv), F32),
                   jax.ShapeDtypeStruct((n // GLA_CHUNK, dk), F32)),
        compiler_params=_params(("arbitrary",)),
        name="gla_proj_prompt",
    )(x, w, w2, b2)


def _gla_proj_sample_kernel(x_ref, w_ref, w2_ref, b2_ref, q_ref, k_ref, v_ref, rs_ref, la_ref, *, dk, dv):
    xb = x_ref[...].astype(BF16)
    q_ref[...] = _dot(xb, w_ref[:, 0:dk])
    k_ref[...] = _dot(xb, w_ref[:, dk:2 * dk])
    v_ref[...] = _dot(xb, w_ref[:, 2 * dk:2 * dk + dv])
    rs_ref[...] = _silu(_dot(xb, w_ref[:, 2 * dk + dv:2 * dk + 2 * dv]))
    low = _dot(xb, w_ref[:, 2 * dk + 2 * dv:])
    la_ref[...] = _gla_log_decay(low, w2_ref, b2_ref)


def _gla_proj_sample(x, w, w2, b2, *, dk, dv):
    n, d = x.shape
    full = lambda s: pl.BlockSpec(s, lambda i: (0,) * len(s))
    return pl.pallas_call(
        functools.partial(_gla_proj_sample_kernel, dk=dk, dv=dv),
        grid=(1,),
        in_specs=[full((n, d)), full(w.shape), full(w2.shape), full(b2.shape)],
        out_specs=(full((n, dk)), full((n, dk)), full((n, dv)), full((n, dv)), full((n, dk))),
        out_shape=(jax.ShapeDtypeStruct((n, dk), F32), jax.ShapeDtypeStruct((n, dk), F32),
                   jax.ShapeDtypeStruct((n, dv), F32), jax.ShapeDtypeStruct((n, dv), F32),
                   jax.ShapeDtypeStruct((n, dk), F32)),
        compiler_params=_params(("arbitrary",)),
        name="gla_proj_sample",
    )(x, w, w2, b2)


def _gla_rec_kernel(qi_ref, ki_ref, ks_ref, v_ref, al_ref, o_ref, st_out_ref, st_ref):
    t = pl.program_id(2)

    @pl.when(t == 0)
    def _():
        st_ref[...] = jnp.zeros_like(st_ref)

    c = GLA_CHUNK
    ct = qi_ref.shape[0]
    n = ct // c
    qi, ki, vv = qi_ref[...], ki_ref[...], v_ref[...]
    row = lax.broadcasted_iota(jnp.int32, (ct, ct), 0)
    col = lax.broadcasted_iota(jnp.int32, (ct, ct), 1)
    keep = (row >= col) & ((row >> GLA_CHUNK_SHIFT) == (col >> GLA_CHUNK_SHIFT))
    att = jnp.where(keep, _dot_nt(qi, ki), 0.0)
    o_intra = _dot(att.astype(BF16), vv)
    incs = [_dot_tn(vv[i * c:(i + 1) * c], ks_ref[i * c:(i + 1) * c, :]) for i in range(n)]
    st = st_ref[...]
    entering = []
    for i in range(n):
        entering.append(st.astype(BF16))
        st = al_ref[i:i + 1, :] * st + incs[i]
    st_ref[...] = st
    o = o_intra + jnp.concatenate(
        [_dot_nt(qi[i * c:(i + 1) * c], entering[i]) for i in range(n)], axis=0)
    o_ref[...] = o * lax.rsqrt(jnp.mean(o * o, axis=-1, keepdims=True) + LN_EPS)

    @pl.when(t == pl.num_programs(2) - 1)
    def _():
        st_out_ref[...] = st


def _gla_rec(qi, ki, ks, v, al, *, batch, seq, dkh, dvh):
    ct = 256
    nct = seq // ct
    cpt = ct // GLA_CHUNK
    tok = lambda b, h, t: (b * nct + t, h)
    return pl.pallas_call(
        _gla_rec_kernel,
        grid=(batch, GLA_H, nct),
        in_specs=[pl.BlockSpec((ct, dkh), tok), pl.BlockSpec((ct, dkh), tok), pl.BlockSpec((ct, dkh), tok),
                  pl.BlockSpec((ct, dvh), tok), pl.BlockSpec((cpt, dkh), tok)],
        out_specs=(pl.BlockSpec((ct, dvh), tok),
                   pl.BlockSpec((None, None, dvh, dkh), lambda b, h, t: (b, h, 0, 0))),
        out_shape=(jax.ShapeDtypeStruct((batch * seq, GLA_H * dvh), F32),
                   jax.ShapeDtypeStruct((batch, GLA_H, dvh, dkh), F32)),
        scratch_shapes=[pltpu.VMEM((dvh, dkh), F32)],
        compiler_params=_params(("arbitrary", "arbitrary", "arbitrary")),
        name="gla_rec",
    )(qi, ki, ks, v, al)


def _gla_step_kernel(q_ref, k_ref, la_ref, v_ref, s0_ref, o_ref, s_ref, *, scale):
    la = la_ref[...]
    a = jnp.exp(la)
    k = k_ref[...]
    qi = q_ref[...] * a * scale
    ki = k * jnp.exp(-la)
    v = v_ref[...]
    s0 = s0_ref[...]
    att = jnp.sum(qi * ki, axis=0, keepdims=True)
    o = att * v + jnp.sum(qi * s0, axis=0, keepdims=True)
    s_ref[...] = a * s0 + k * v
    o_ref[...] = o * lax.rsqrt(jnp.mean(o * o, axis=-1, keepdims=True) + LN_EPS)


def _gla_step(q, k, la, v, s0, *, scale):
    bsz, h, dkh, dvh = s0.shape
    col = pl.BlockSpec((None, None, dkh, 1), lambda b, i: (b, i, 0, 0))
    rowv = pl.BlockSpec((None, None, 1, dvh), lambda b, i: (b, i, 0, 0))
    mat = pl.BlockSpec((None, None, dkh, dvh), lambda b, i: (b, i, 0, 0))
    return pl.pallas_call(
        functools.partial(_gla_step_kernel, scale=scale),
        grid=(bsz, h),
        in_specs=[col, col, col, rowv, mat],
        out_specs=(rowv, mat),
        out_shape=(jax.ShapeDtypeStruct((bsz, h, 1, dvh), F32),
                   jax.ShapeDtypeStruct((bsz, h, dkh, dvh), F32)),
        compiler_params=_params(("arbitrary", "arbitrary")),
        name="gla_step",
    )(q, k, la, v, s0)


def _rope_tables(pos):
    half = NSA_D // 2
    inv = ROPE_THETA ** (-jnp.arange(half, dtype=F32) / half)
    ang = pos.astype(F32)[:, None] * inv[None, :]
    cos, sin = jnp.cos(ang), jnp.sin(ang)
    zero = jnp.zeros_like(sin)
    cos_t = jnp.tile(cos, (1, LANES // half))
    sin_lo = jnp.tile(jnp.concatenate([-sin, zero], axis=1), (1, LANES // NSA_D))
    sin_hi = jnp.tile(jnp.concatenate([zero, sin], axis=1), (1, LANES // NSA_D))
    return cos_t, sin_lo, sin_hi


def _prep_nsa_weights(w_in, b_gate, pe, w1, w2):
    d = w_in.shape[0]
    qc = NSA_H * NSA_D
    kc = NSA_G * NSA_D
    gate_off = qc + 6 * kc
    wg = w_in[:, gate_off:].reshape(d, 3, NSA_G, NSA_R).transpose(0, 2, 1, 3).reshape(d, NSA_G, 3 * NSA_R)
    wg = jnp.pad(wg, ((0, 0), (0, 0), (0, LANES - 3 * NSA_R))).reshape(d, NSA_G * LANES)
    bg = b_gate.reshape(3, NSA_G, NSA_R).transpose(1, 0, 2).reshape(NSA_G, 3 * NSA_R)
    bg = jnp.pad(bg, ((0, 0), (0, LANES - 3 * NSA_R))).reshape(1, NSA_G * LANES)
    w = jnp.concatenate([w_in[:, :gate_off], wg], axis=1).astype(BF16)
    eye = jnp.eye(NSA_G, dtype=F32)
    pe_x = jnp.tile(pe.transpose(1, 0, 2), (1, 1, NSA_G))
    w1_bd = jnp.einsum('lcde,gh->clgdhe', w1, eye).reshape(2, CMP_BLK, kc, kc).astype(BF16)
    w2_dup = jnp.concatenate([w2, w2], axis=-1)
    w2_bd = jnp.einsum('cef,gh->cgehf', w2_dup, eye).reshape(2, kc, 2 * kc).astype(BF16)
    return w, bg, pe_x, w1_bd, w2_bd


def _prep_vt_weights(w_in):
    qc = NSA_H * NSA_D
    kc = NSA_G * NSA_D
    out = []
    for branch in (1, 2):
        lo = qc + branch * 2 * kc + kc
        out.append(w_in[:, lo:lo + kc].T)
    return jnp.stack(out).astype(BF16)


def _prep_page_compress_weights(pe, w1, w2):
    nd = NSA_D // 2
    eye = jnp.eye(PAGE // CMP_BLK, dtype=F32)
    pe_n = jnp.broadcast_to(pe.transpose(1, 2, 0).reshape(2, nd, 2, 1, CMP_BLK),
                            (2, nd, 2, PAGE // CMP_BLK, CMP_BLK)).reshape(2, nd, 2 * PAGE)
    w1c = w1.transpose(1, 2, 0, 3).reshape(2, nd, 2, CMP_BLK, NSA_D)
    w1_n = jnp.einsum('cqple,hk->cqphlke', w1c, eye).reshape(2, nd, 2 * PAGE, PAGE).astype(BF16)
    w2_n = jnp.einsum('cef,hk->chekf', w2, eye).reshape(2, PAGE, PAGE).astype(BF16)
    return pe_n, w1_n, w2_n


def kernel(x_prompt, x_sample, cache_cmp_kv, cache_slc_kv, cache_win_kv, state_gla, page_table,
           nsa_w_in, nsa_b_gate, nsa_pe_cmp, nsa_w_cmp1, nsa_w_cmp2, nsa_w_out,
           gla_w_in, gla_w_gate2, gla_b_gate2, gla_norm_g, gla_w_out,
           ffn_w_gu, ffn_w_down, ln_g, ln_b):
    batch, seq, d = x_prompt.shape
    bsz, dec_seq, _ = x_sample.shape
    depth = ffn_w_gu.shape[0]
    n_pool = cache_cmp_kv.shape[1]
    n_pages = page_table.shape[1]
    past_len = n_pages * PAGE
    kc = NSA_G * NSA_D
    assert dec_seq == 1 and depth == 2 and d == NSA_H * NSA_D
    assert seq % 512 == 0 and seq >= WINDOW and past_len % SEL_BLK == 0
    assert cache_win_kv.shape[2] == WINDOW and past_len // SEL_BLK >= N_SEL
    alpha = (2.0 * depth) ** 0.25
    tm = 512
    nbp = past_len // CMP_BLK

    xp = x_prompt.reshape(batch * seq, d)
    xs = x_sample.reshape(bsz, d)
    ln_g = ln_g.reshape(depth, 2, 1, d)
    ln_b = ln_b.reshape(depth, 2, 1, d)
    w_gu = ffn_w_gu.astype(BF16)
    w_dn = ffn_w_down.astype(BF16)

    w0, bg0, pe_x, w1_bd, w2_bd = _prep_nsa_weights(nsa_w_in[0], nsa_b_gate[0], nsa_pe_cmp[0],
                                                    nsa_w_cmp1[0], nsa_w_cmp2[0])
    w_out0 = nsa_w_out[0].astype(BF16)
    tabs_p = _rope_tables(jnp.arange(seq, dtype=jnp.int32))
    tabs_s = _rope_tables(jnp.full((bsz,), past_len, dtype=jnp.int32))

    seq_tiles = seq // tm
    (qraw, qrot, cmp_p, slc_p, win_p, gates_p, skd, wkd, svt, wvt) = _nsa_proj(
        xp, w0, bg0, *tabs_p, _prep_vt_weights(nsa_w_in[0]), tm=tm, seq_tiles=seq_tiles,
        win_rows=batch * WINDOW, win_map=lambda i: (i // seq_tiles, 0), attn_kt=ATTN_KT)
    ckv_p = _compress_prompt(cmp_p, pe_x, w1_bd, w2_bd, batch=batch, seq=seq)
    o_p = _nsa_attn_prompt(qraw, qrot, ckv_p, skd, svt, wkd, wvt, gates_p, batch=batch, seq=seq,
                           tq=ATTN_TQ, kt=ATTN_KT)
    yp = _proj_ln(o_p, w_out0, xp, ln_g[0, 0], ln_b[0, 0], tm=tm, alpha=alpha)

    (qraw_s, qrot_s, cmp_s, slc_s, win_s, gates_s) = _nsa_proj(
        xs, w0, bg0, *tabs_s, tm=bsz, seq_tiles=1, win_rows=bsz, win_map=lambda i: (i, 0))
    pages_fm = lambda c: jnp.transpose(c[0], (0, 2, 3, 4, 1))
    pe_n, w1_n, w2_n = _prep_page_compress_weights(nsa_pe_cmp[0], nsa_w_cmp1[0], nsa_w_cmp2[0])
    ckv_s = _compress_pages(pages_fm(cache_cmp_kv).reshape(n_pool, 2, kc, PAGE), page_table, pe_n, w1_n, w2_n)
    q16 = qraw_s.reshape(bsz, NSA_H, NSA_D)
    zeros16 = jnp.zeros_like(q16)
    q_e = jnp.concatenate([q16, zeros16], axis=-1)
    q_o = jnp.concatenate([zeros16, q16], axis=-1)
    ocmp_s, idx4 = _nsa_sample_select(q_e, q_o, ckv_s, bsz=bsz)
    idx = idx4[:, :, :N_SEL - 1, 0].reshape(bsz, NSA_G * (N_SEL - 1))
    q4r = qrot_s.reshape(bsz, NSA_G, NSA_R, NSA_D)
    q_cmp = jnp.einsum('bgrd,gh->bgrhd', q4r, jnp.eye(NSA_G, dtype=BF16)).reshape(bsz, NSA_H, kc)
    gates_s3 = gates_s.reshape(bsz, NSA_G, LANES)[:, :, :3 * NSA_R].reshape(bsz, NSA_G, 3, NSA_R)
    gates_s3 = gates_s3.transpose(0, 1, 3, 2).reshape(bsz, NSA_H, 3)
    o_s, win_buf_s = _nsa_sample_attn(
        idx, page_table, pages_fm(cache_slc_kv), qrot_s.reshape(bsz, NSA_H, NSA_D), q_cmp,
        slc_s.reshape(bsz, 1, 2 * kc), cache_win_kv[0].reshape(bsz, WINDOW, 2 * kc),
        win_s.reshape(bsz, 1, 2 * kc), gates_s3, ocmp_s)
    ys = _proj_ln(o_s.reshape(bsz, d), w_out0, xs, ln_g[0, 0], ln_b[0, 0], tm=bsz, alpha=alpha)

    yp = _ffn(yp, w_gu[0], w_dn[0], ln_g[0, 1], ln_b[0, 1], tm=tm, alpha=alpha)
    ys = _ffn(ys, w_gu[0], w_dn[0], ln_g[0, 1], ln_b[0, 1], tm=bsz, alpha=alpha)

    dk = gla_w_gate2.shape[2]
    dv = gla_w_out.shape[1]
    dkh, dvh = dk // GLA_H, dv // GLA_H
    gw = jnp.pad(gla_w_in[0], ((0, 0), (0, LANES - GLA_RANK))).astype(BF16)
    gw2 = jnp.pad(gla_w_gate2[0], ((0, LANES - GLA_RANK), (0, 0))).astype(BF16)
    gb2 = gla_b_gate2[0].reshape(1, dk)
    g_out = gla_w_out[0].astype(BF16)
    ng = gla_norm_g[0].reshape(1, dv)

    qi, ki, ks, vb, rs_p, al = _gla_proj_prompt(yp, gw, gw2, gb2, tm=tm, dk=dk, dv=dv)
    on_p, st_p = _gla_rec(qi, ki, ks, vb, al, batch=batch, seq=seq, dkh=dkh, dvh=dvh)
    yp = _gla_proj_ln(on_p, rs_p, ng, g_out, yp, ln_g[1, 0], ln_b[1, 0], tm=tm, alpha=alpha)

    q_s, k_s, v_s, rs_s, la_s = _gla_proj_sample(ys, gw, gw2, gb2, dk=dk, dv=dv)
    colv = lambda a: a.reshape(bsz, GLA_H, dkh, 1)
    on_s, st_s = _gla_step(colv(q_s), colv(k_s), colv(la_s), v_s.reshape(bsz, GLA_H, 1, dvh),
                           state_gla[0], scale=dkh ** -0.5)
    ys = _gla_proj_ln(on_s.reshape(bsz, dv), rs_s, ng, g_out, ys, ln_g[1, 0], ln_b[1, 0], tm=bsz, alpha=alpha)

    yp = _ffn(yp, w_gu[1], w_dn[1], ln_g[1, 1], ln_b[1, 1], tm=tm, alpha=alpha)
    ys = _ffn(ys, w_gu[1], w_dn[1], ln_g[1, 1], ln_b[1, 1], tm=bsz, alpha=alpha)

    kv6 = lambda a, rows: a.reshape(1, rows[0], rows[1], 2, NSA_G, NSA_D)
    return (yp.reshape(batch, seq, d), ys.reshape(bsz, 1, d),
            kv6(cmp_p, (batch, seq)), kv6(slc_p, (batch, seq)), kv6(win_p, (batch, WINDOW)),
            jnp.swapaxes(st_p, 2, 3)[None],
            kv6(cmp_s, (bsz, 1)), kv6(slc_s, (bsz, 1)), kv6(win_buf_s, (bsz, WINDOW)),
            st_s[None])
```

```python
import functools

import jax
import jax.numpy as jnp
from jax import lax
from jax.experimental import pallas as pl
from jax.experimental.pallas import tpu as pltpu

PAGE = 128
NSA_H = 16
NSA_D = 64
NSA_G = 4
NSA_R = NSA_H // NSA_G
CMP_BLK = 64
SEL_BLK = 64
SEL_SHIFT = 6
R_SHIFT = 2
LANE_SHIFT = 7
N_SEL = 16
WINDOW = 512
ATTN_SCALE = NSA_D ** -0.5
LOG2E = 1.4426950408889634
ROPE_THETA = 10000.0
NEG_INF = -1e30
SEL_FORCE = 1e4

GLA_H = 4
GLA_RANK = 16
GLA_TAU = 16.0
GLA_CHUNK = 32
GLA_CHUNK_SHIFT = 5
LN_EPS = 1e-5

LANES = 128
HALF = LANES // 2
ONES_ROWS = 16
ATTN_TQ = 128
ATTN_KT = 256
ATTN_UNROLL = 4
VMEM_LIMIT = 56 * 1024 * 1024

F32 = jnp.float32
BF16 = jnp.bfloat16

NT_DIMS = (((1,), (1,)), ((), ()))
TN_DIMS = (((0,), (0,)), ((), ()))


def _params(sem):
    return pltpu.CompilerParams(dimension_semantics=sem, vmem_limit_bytes=VMEM_LIMIT)


def _dot(a, b):
    return jnp.dot(a, b, preferred_element_type=F32)


def _dot_nt(a, b):
    return lax.dot_general(a, b, NT_DIMS, preferred_element_type=F32)


def _dot_tn(a, b):
    return lax.dot_general(a, b, TN_DIMS, preferred_element_type=F32)


def _layer_norm(z, g, b):
    mu = jnp.mean(z, axis=-1, keepdims=True)
    zc = z - mu
    var = jnp.mean(zc * zc, axis=-1, keepdims=True)
    return zc * lax.rsqrt(var + LN_EPS) * g + b


def _silu(x):
    return x * jax.nn.sigmoid(x)


def _rope(x, cos_t, sin_lo, sin_hi):
    out = []
    for p in range(x.shape[1] // LANES):
        blk = x[:, p * LANES:(p + 1) * LANES]
        x_up = pltpu.roll(blk, LANES - NSA_D // 2, 1)
        x_dn = pltpu.roll(blk, NSA_D // 2, 1)
        out.append(blk * cos_t + x_up * sin_lo + x_dn * sin_hi)
    return out[0] if len(out) == 1 else jnp.concatenate(out, axis=1)


def _dup_heads(x):
    rows, c = x.shape
    lane = lax.broadcasted_iota(jnp.int32, (rows, LANES), 1)
    lo = lane < HALF
    out = []
    for p in range(c // LANES):
        blk = x[:, p * LANES:(p + 1) * LANES]
        sw = pltpu.roll(blk, HALF, 1)
        out.append(jnp.where(lo, blk, sw))
        out.append(jnp.where(lo, sw, blk))
    return jnp.concatenate(out, axis=1)


def _nsa_proj_kernel(x_ref, w_ref, bg_ref, cos_ref, slo_ref, shi_ref, *refs, attn_kt):
    qc = NSA_H * NSA_D
    kc = NSA_G * NSA_D
    if attn_kt is None:
        qraw_ref, qrot_ref, cmp_ref, slc_ref, win_ref, gates_ref = refs
    else:
        (wvt_ref, qraw_ref, qrot_ref, cmp_ref, slc_ref, win_ref, gates_ref,
         skd_ref, wkd_ref, svt_ref, wvt_out_ref) = refs
    xb = x_ref[...].astype(BF16)
    cos_t, slo, shi = cos_ref[...], slo_ref[...], shi_ref[...]

    q = _dot(xb, w_ref[:, 0:qc])
    qraw_ref[...] = (q * ATTN_SCALE).astype(BF16)
    qrot_ref[...] = (_rope(q, cos_t, slo, shi) * (ATTN_SCALE * LOG2E)).astype(BF16)

    cmp_ref[...] = _dot(xb, w_ref[:, qc:qc + 2 * kc])

    slc = _dot(xb, w_ref[:, qc + 2 * kc:qc + 4 * kc])
    sk = _rope(slc[:, :kc], cos_t, slo, shi)
    slc_ref[:, :kc] = sk
    slc_ref[:, kc:] = slc[:, kc:]

    win = _dot(xb, w_ref[:, qc + 4 * kc:qc + 6 * kc])
    wk = _rope(win[:, :kc], cos_t, slo, shi)
    win_ref[:, :kc] = wk
    win_ref[:, kc:] = win[:, kc:]

    gz = _dot(xb, w_ref[:, qc + 6 * kc:qc + 6 * kc + NSA_G * LANES]) + bg_ref[...]
    gates_ref[...] = jax.nn.sigmoid(gz)

    if attn_kt is not None:
        skd_ref[...] = _dup_heads(sk).astype(BF16)
        wkd_ref[...] = _dup_heads(wk).astype(BF16)
        for src, dst in ((0, svt_ref), (1, wvt_out_ref)):
            vt = _dot_nt(wvt_ref[src], xb).astype(BF16)
            for t in range(x_ref.shape[0] // attn_kt):
                dst[t] = vt[:, t * attn_kt:(t + 1) * attn_kt]


def _nsa_proj(x, w, bg, cos_t, slo, shi, wvt=None, *, tm, seq_tiles, win_rows, win_map, attn_kt=None):
    n, d = x.shape
    qc = NSA_H * NSA_D
    kc2 = 2 * NSA_G * NSA_D
    kd = NSA_G * LANES
    row = lambda i: (i, 0)
    tab = lambda i: (i % seq_tiles, 0)
    const = lambda i: (0, 0)
    in_specs = [pl.BlockSpec((tm, d), row), pl.BlockSpec(w.shape, const), pl.BlockSpec(bg.shape, const),
                pl.BlockSpec((tm, LANES), tab), pl.BlockSpec((tm, LANES), tab), pl.BlockSpec((tm, LANES), tab)]
    out_shape = [jax.ShapeDtypeStruct((n, qc), BF16), jax.ShapeDtypeStruct((n, qc), BF16),
                 jax.ShapeDtypeStruct((n, kc2), F32), jax.ShapeDtypeStruct((n, kc2), F32),
                 jax.ShapeDtypeStruct((win_rows, kc2), F32), jax.ShapeDtypeStruct((n, kd), F32)]
    out_specs = [pl.BlockSpec((tm, qc), row), pl.BlockSpec((tm, qc), row),
                 pl.BlockSpec((tm, kc2), row), pl.BlockSpec((tm, kc2), row),
                 pl.BlockSpec((tm, kc2), win_map), pl.BlockSpec((tm, kd), row)]
    args = [x, w, bg, cos_t, slo, shi]
    if attn_kt is not None:
        assert tm % attn_kt == 0
        in_specs.append(pl.BlockSpec(wvt.shape, lambda i: (0, 0, 0)))
        args.append(wvt)
        tiles = tm // attn_kt
        out_shape += [jax.ShapeDtypeStruct((n, kd), BF16), jax.ShapeDtypeStruct((n, kd), BF16),
                      jax.ShapeDtypeStruct((n // attn_kt, kc2 // 2, attn_kt), BF16),
                      jax.ShapeDtypeStruct((n // attn_kt, kc2 // 2, attn_kt), BF16)]
        out_specs += [pl.BlockSpec((tm, kd), row), pl.BlockSpec((tm, kd), row),
                      pl.BlockSpec((tiles, kc2 // 2, attn_kt), lambda i: (i, 0, 0)),
                      pl.BlockSpec((tiles, kc2 // 2, attn_kt), lambda i: (i, 0, 0))]
    return pl.pallas_call(
        functools.partial(_nsa_proj_kernel, attn_kt=attn_kt),
        grid=(n // tm,),
        in_specs=in_specs,
        out_specs=tuple(out_specs),
        out_shape=tuple(out_shape),
        compiler_params=_params(("arbitrary",)),
        name="nsa_proj",
    )(*args)


def _compress_body(tok_refs, pe_ref, w1_ref, w2_ref, out_ref):
    m = tok_refs[0].shape[0] // CMP_BLK
    acc = jnp.zeros((m, w1_ref.shape[2]), F32)
    for l in range(CMP_BLK):
        xl = jnp.concatenate([r[pl.ds(l, m, stride=CMP_BLK), :] for r in tok_refs], axis=1)
        xl = xl + pe_ref[l:l + 1, :]
        acc = acc + _dot(xl.astype(BF16), w1_ref[l])
    h = _silu(acc)
    out_ref[...] = _dot(h.astype(BF16), w2_ref[...])


def _compress_prompt_kernel(tok0_ref, tok1_ref, pe_ref, w1_ref, w2_ref, out_ref):
    _compress_body((tok0_ref, tok1_ref), pe_ref, w1_ref, w2_ref, out_ref)


def _compress_prompt(cmp_kv, pe, w1, w2, *, batch, seq):
    kc = NSA_G * NSA_D
    nb = seq // CMP_BLK
    return pl.pallas_call(
        _compress_prompt_kernel,
        grid=(2, batch),
        in_specs=[pl.BlockSpec((seq, LANES), lambda c, b: (b, 2 * c)),
                  pl.BlockSpec((seq, LANES), lambda c, b: (b, 2 * c + 1)),
                  pl.BlockSpec((None, CMP_BLK, kc), lambda c, b: (c, 0, 0)),
                  pl.BlockSpec((None, CMP_BLK, kc, kc), lambda c, b: (c, 0, 0, 0)),
                  pl.BlockSpec((None, kc, 2 * kc), lambda c, b: (c, 0, 0))],
        out_specs=pl.BlockSpec((None, nb, 2 * kc), lambda c, b: (c, b, 0)),
        out_shape=jax.ShapeDtypeStruct((2, batch * nb, 2 * kc), F32),
        compiler_params=_params(("arbitrary", "arbitrary")),
        name="compress_prompt",
    )(cmp_kv, cmp_kv, pe, w1, w2)


def _compress_pages_kernel(pt_ref, *refs, pages):
    del pt_ref
    page_refs = refs[:pages]
    pe_ref, w1_ref, w2_ref, out_ref, buf_ref = refs[pages:]
    rows_pp = NSA_G * NSA_D
    for k in range(pages):
        buf_ref[k * rows_pp:(k + 1) * rows_pp, :] = page_refs[k][...]
    m = pages * NSA_G
    acc = jnp.zeros((m, LANES), F32)
    for dd in range(NSA_D // 2):
        x = jnp.concatenate([buf_ref[pl.ds(2 * dd, m, stride=NSA_D), :],
                             buf_ref[pl.ds(2 * dd + 1, m, stride=NSA_D), :]], axis=1)
        x = x + pe_ref[dd:dd + 1, :]
        acc = acc + _dot(x.astype(BF16), w1_ref[dd])
    out_ref[...] = _dot(_silu(acc).astype(BF16), w2_ref[...])


def _compress_pages(cache, page_table, pe, w1, w2):
    bsz, n_pages = page_table.shape
    pages = min(64, n_pages)
    groups = n_pages // pages
    gd = NSA_G * NSA_D
    assert PAGE == 2 * CMP_BLK and PAGE == LANES

    def page_map(k):
        return lambda c, b, h, pt: (pt[b, h * pages + k], c, 0, 0)

    in_specs = [pl.BlockSpec((None, None, gd, PAGE), page_map(k)) for k in range(pages)]
    in_specs += [pl.BlockSpec((None,) + pe.shape[1:], lambda c, b, h, pt: (c, 0, 0)),
                 pl.BlockSpec((None,) + w1.shape[1:], lambda c, b, h, pt: (c, 0, 0, 0)),
                 pl.BlockSpec((None,) + w2.shape[1:], lambda c, b, h, pt: (c, 0, 0))]
    grid_spec = pltpu.PrefetchScalarGridSpec(
        num_scalar_prefetch=1,
        grid=(2, bsz, groups),
        in_specs=in_specs,
        out_specs=pl.BlockSpec((None, pages * NSA_G, LANES), lambda c, b, h, pt: (c, b * groups + h, 0)),
        scratch_shapes=[pltpu.VMEM((pages * gd, PAGE), F32)],
    )
    return pl.pallas_call(
        functools.partial(_compress_pages_kernel, pages=pages),
        grid_spec=grid_spec,
        out_shape=jax.ShapeDtypeStruct((2, bsz * n_pages * NSA_G, LANES), F32),
        compiler_params=_params(("arbitrary", "arbitrary", "arbitrary")),
        name="compress_pages",
    )(page_table, *([cache] * pages), pe, w1, w2)


def _softmax_tile(carry, s_t, vt_tile, bias, tq):
    ms, acc = carry
    new_ms, ps, alphas = [], [], []
    for r in range(NSA_R):
        sb = s_t[:, r * tq:(r + 1) * tq] + bias
        m_new = jnp.maximum(ms[r], jnp.max(sb, axis=0, keepdims=True))
        ps.append(jnp.exp2(sb - m_new).astype(BF16))
        alphas.append(jnp.exp2(ms[r] - m_new))
        new_ms.append(m_new)
    acc = jnp.concatenate(alphas, axis=1) * acc + _dot(vt_tile, jnp.concatenate(ps, axis=1))
    return tuple(new_ms), acc


def _rank_blocks(score_ref, rank_ref, last_blk):
    nb, tq = score_ref.shape
    sub = lax.broadcasted_iota(jnp.int32, (8, tq), 0)
    rank_ref[...] = jnp.zeros((nb, tq), F32)
    for gi in range(nb // 8):
        @pl.when(gi * 8 <= last_blk)
        def _(gi=gi):
            rows8 = score_ref[gi * 8:(gi + 1) * 8, :]
            tops = [jnp.broadcast_to(rows8[u:u + 1, :], (8, tq)) for u in range(8)]
            for v in range(nb // 8):
                sv = score_ref[v * 8:(v + 1) * 8, :]
                cnt = jnp.zeros((8, tq), F32)
                for u in range(8):
                    if v > gi:
                        before = tops[u] >= sv
                    elif v < gi:
                        before = tops[u] > sv
                    else:
                        before = (tops[u] > sv) | ((tops[u] == sv) & (sub > u))
                    cnt = cnt + jnp.where(before, 1.0, 0.0)
                rank_ref[v * 8:(v + 1) * 8, :] += cnt


def _nsa_attn_kernel(qraw_ref, qrot_ref, ck_ref, cv_ref, sk_ref, svt_ref, wk_ref, wvt_ref,
                     gates_ref, out_ref, score_ref, rank_ref, *, tq, kt, n_sel, unroll):
    nb = ck_ref.shape[0]
    rows = NSA_R * tq
    q0 = pl.program_id(2) * tq

    lane = lax.broadcasted_iota(jnp.int32, (tq, LANES), 1)
    lo = lane < HALF

    def stack(ref):
        qa = ref[:, 0:LANES].astype(F32)
        qb = ref[:, LANES:2 * LANES].astype(F32)
        parts = [jnp.where(lo, qa, 0.0), jnp.where(lo, 0.0, qa),
                 jnp.where(lo, qb, 0.0), jnp.where(lo, 0.0, qb)]
        return jnp.concatenate(parts, axis=0).astype(BF16)

    q_raw = stack(qraw_ref)
    q_rot = stack(qrot_ref)
    ck = ck_ref[...].astype(BF16)
    cv = cv_ref[...].astype(BF16)

    st = _dot_nt(ck, q_raw)
    t_c = q0 + (lax.broadcasted_iota(jnp.int32, (nb, rows), 1) & (tq - 1))
    n_c = lax.broadcasted_iota(jnp.int32, (nb, rows), 0)
    mask_t = (n_c + 1) * CMP_BLK - 1 <= t_c
    st = jnp.where(mask_t, st, NEG_INF)
    pt = jnp.where(mask_t, jnp.exp(st - jnp.max(st, axis=0, keepdims=True)), 0.0)
    pt = pt / jnp.maximum(jnp.sum(pt, axis=0, keepdims=True), 1e-30)
    o_cmp = _dot_tn(cv[:, 0:NSA_D], pt.astype(BF16))

    imp = pt[:, 0:tq]
    for r in range(1, NSA_R):
        imp = imp + pt[:, r * tq:(r + 1) * tq]
    blk = lax.broadcasted_iota(jnp.int32, (nb, tq), 0)
    cur = (q0 + lax.broadcasted_iota(jnp.int32, (nb, tq), 1)) >> SEL_SHIFT
    forced = (blk == 0) | (blk == cur) | (blk == cur - 1)
    valid = blk <= cur
    score_ref[...] = jnp.where(forced, SEL_FORCE, jnp.where(valid, imp, -1.0))
    _rank_blocks(score_ref, rank_ref, (q0 + tq - 1) >> SEL_SHIFT)
    score_ref[...] = jnp.where((rank_ref[...] < n_sel) & valid, 0.0, NEG_INF)

    key_row = lax.broadcasted_iota(jnp.int32, (kt, tq), 0)
    t_lane = q0 + lax.broadcasted_iota(jnp.int32, (kt, tq), 1)
    ones = jnp.ones((ONES_ROWS, kt), BF16)

    init = (tuple(jnp.full((1, tq), NEG_INF, F32) for _ in range(NSA_R)),
            jnp.zeros((NSA_D + ONES_ROWS, rows), F32))

    hi = (q0 + tq + kt - 1) // kt
    n_tiles = sk_ref.shape[0] // kt

    def tile_loop(lo_tile, trips, k_ref, vt_ref, make_bias, unroll, carry):
        def body(i, carry):
            staged = []
            for u in range(unroll):
                j = lo_tile + i * unroll + u
                jc = jnp.minimum(j, n_tiles - 1)
                k0 = pl.multiple_of(jc * kt, kt)
                k_pos = jnp.where(j < hi, k0, 2 * n_tiles * kt) + key_row
                staged.append((_dot_nt(k_ref[pl.ds(k0, kt), :], q_rot), make_bias(k0, k_pos), jc))
            for s_t, bias, jc in staged:
                vt = jnp.concatenate([vt_ref[jc], ones], axis=0)
                carry = _softmax_tile(carry, s_t, vt, bias, tq)
            return carry
        return lax.fori_loop(0, trips, body, carry)

    def sel_bias(k0, k_pos):
        blk0 = k0 // SEL_BLK
        picked = jnp.concatenate(
            [jnp.broadcast_to(score_ref[pl.ds(blk0 + i, 1), :], (SEL_BLK, tq)) for i in range(kt // SEL_BLK)],
            axis=0)
        return jnp.where(k_pos <= t_lane, picked, NEG_INF)

    full = hi // unroll
    carry = tile_loop(0, full, sk_ref, svt_ref, sel_bias, unroll, init)
    acc_s = tile_loop(full * unroll, (hi - full * unroll + 1) // 2, sk_ref, svt_ref, sel_bias, 2, carry)[1]

    def win_bias(k0, k_pos):
        diff = t_lane - k_pos
        return jnp.where((diff >= 0) & (diff <= WINDOW), 0.0, NEG_INF)

    acc_w = tile_loop(jnp.maximum(q0 - WINDOW, 0) // kt, 1, wk_ref, wvt_ref, win_bias, WINDOW // kt + 1, init)[1]

    gt = gates_ref[...].T
    o_s = acc_s[0:NSA_D, :] / jnp.maximum(acc_s[NSA_D:NSA_D + 1, :], 1e-30)
    o_w = acc_w[0:NSA_D, :] / jnp.maximum(acc_w[NSA_D:NSA_D + 1, :], 1e-30)
    heads = []
    for r in range(NSA_R):
        sl = slice(r * tq, (r + 1) * tq)
        heads.append(gt[r:r + 1, :] * o_cmp[:, sl] + gt[NSA_R + r:NSA_R + r + 1, :] * o_s[:, sl]
                     + gt[2 * NSA_R + r:2 * NSA_R + r + 1, :] * o_w[:, sl])
    for pair in range(NSA_R // 2):
        both = jnp.concatenate(heads[2 * pair:2 * pair + 2], axis=0)
        out_ref[:, pair * LANES:(pair + 1) * LANES] = both.T.astype(BF16)


def _nsa_attn_prompt(qraw, qrot, ckv, skd, svt, wkd, wvt, gates, *, batch, seq, tq, kt):
    assert tq == LANES and kt % tq == 0 and WINDOW % kt == 0 and seq % kt == 0
    nqt = seq // tq
    nb = seq // CMP_BLK
    gc = NSA_R * NSA_D
    q_map = lambda b, g, t: (b * nqt + t, g)
    kv_map = lambda b, g, t: (b, g)
    vt_map = lambda b, g, t: (b, g, 0)
    return pl.pallas_call(
        functools.partial(_nsa_attn_kernel, tq=tq, kt=kt, n_sel=min(N_SEL, seq // SEL_BLK),
                          unroll=ATTN_UNROLL),
        grid=(batch, NSA_G, nqt),
        in_specs=[pl.BlockSpec((tq, gc), q_map), pl.BlockSpec((tq, gc), q_map),
                  pl.BlockSpec((None, nb, LANES), lambda b, g, t: (0, b, g)),
                  pl.BlockSpec((None, nb, LANES), lambda b, g, t: (1, b, g)),
                  pl.BlockSpec((seq, LANES), kv_map), pl.BlockSpec((seq // kt, NSA_D, kt), vt_map),
                  pl.BlockSpec((seq, LANES), kv_map), pl.BlockSpec((seq // kt, NSA_D, kt), vt_map),
                  pl.BlockSpec((tq, LANES), q_map)],
        out_specs=pl.BlockSpec((tq, gc), q_map),
        out_shape=jax.ShapeDtypeStruct((batch * seq, NSA_H * NSA_D), BF16),
        scratch_shapes=[pltpu.VMEM((nb, tq), F32), pltpu.VMEM((nb, tq), F32)],
        compiler_params=_params(("arbitrary", "arbitrary", "arbitrary")),
        name="nsa_attn_prompt",
    )(qraw, qrot, ckv, ckv, skd, svt, wkd, wvt, gates)


def _nsa_sample_select_kernel(qe_ref, qo_ref, ck_ref, cv_ref, ocmp_ref, idx_ref):
    npg = ck_ref.shape[0] // NSA_G
    nbp = 2 * npg
    qe, qo = qe_ref[...], qo_ref[...]
    head = lax.broadcasted_iota(jnp.int32, (NSA_H, LANES), 0)

    def block_of(j):
        return jnp.where(j < npg, 2 * j, 2 * (j - npg) + 1)

    n_row = block_of(lax.broadcasted_iota(jnp.int32, (1, nbp), 1))
    n_col = block_of(lax.broadcasted_iota(jnp.int32, (nbp, 1), 0))
    forced = (n_row == 0) | (n_row == nbp - 1)
    diag = lax.broadcasted_iota(jnp.int32, (nbp, nbp), 0) == lax.broadcasted_iota(jnp.int32, (nbp, nbp), 1)
    k_idx = lax.broadcasted_iota(jnp.int32, (N_SEL, nbp), 0).astype(F32)
    n_f = jnp.broadcast_to(n_row, (N_SEL, nbp)).astype(F32)
    ocmp = jnp.zeros((NSA_H, LANES), F32)
    for g in range(NSA_G):
        ck = ck_ref[pl.ds(g, npg, stride=NSA_G), :].astype(BF16)
        cv = cv_ref[pl.ds(g, npg, stride=NSA_G), :].astype(BF16)
        s = jnp.concatenate([_dot_nt(qe, ck), _dot_nt(qo, ck)], axis=1)
        p = jnp.exp(s - jnp.max(s, axis=-1, keepdims=True))
        p = p / jnp.maximum(jnp.sum(p, axis=-1, keepdims=True), 1e-30)
        o_e = _dot(p[:, :npg].astype(BF16), cv)
        o_o = _dot(p[:, npg:].astype(BF16), cv)
        ocmp = jnp.where((head >> R_SHIFT) == g, o_e + pltpu.roll(o_o, HALF, 1), ocmp)

        imp = p[NSA_R * g:NSA_R * g + 1, :]
        for r in range(1, NSA_R):
            imp = imp + p[NSA_R * g + r:NSA_R * g + r + 1, :]
        row = jnp.where(forced, SEL_FORCE, imp)
        col = jnp.sum(jnp.where(diag, row, 0.0), axis=1, keepdims=True)
        beats = (col > row) | ((col == row) & (n_col < n_row))
        rank = jnp.sum(jnp.where(beats, 1.0, 0.0), axis=0, keepdims=True)
        onehot = jnp.where(rank == k_idx, n_f, 0.0)
        idx_ref[g] = jnp.sum(onehot, axis=1, keepdims=True).astype(jnp.int32)
    ocmp_ref[...] = ocmp


def _nsa_sample_select(q_e, q_o, ckv, *, bsz):
    rows = ckv.shape[1] // bsz
    return pl.pallas_call(
        _nsa_sample_select_kernel,
        grid=(bsz,),
        in_specs=[pl.BlockSpec((None, NSA_H, LANES), lambda b: (b, 0, 0)),
                  pl.BlockSpec((None, NSA_H, LANES), lambda b: (b, 0, 0)),
                  pl.BlockSpec((None, rows, LANES), lambda b: (0, b, 0)),
                  pl.BlockSpec((None, rows, LANES), lambda b: (1, b, 0))],
        out_specs=(pl.BlockSpec((None, NSA_H, LANES), lambda b: (b, 0, 0)),
                   pl.BlockSpec((None, NSA_G, N_SEL, 1), lambda b: (b, 0, 0, 0))),
        out_shape=(jax.ShapeDtypeStruct((bsz, NSA_H, LANES), F32),
                   jax.ShapeDtypeStruct((bsz, NSA_G, N_SEL, 1), jnp.int32)),
        compiler_params=_params(("arbitrary",)),
        name="nsa_sample_select",
    )(q_e, q_o, ckv, ckv)


def _nsa_sample_attn_kernel(idx_ref, pt_ref, *refs, n_past):
    del pt_ref
    slabs = refs[:NSA_G * n_past]
    (q64_ref, q_ref, snew_ref, wcache_ref, wnew_ref, gates_ref, ocmp_ref,
     out_ref, wout_ref, wbuf_ref) = refs[NSA_G * n_past:]
    kc = NSA_G * NSA_D
    b = pl.program_id(0)
    q64 = q64_ref[...]
    q = q_ref[...]
    head_row = lax.broadcasted_iota(jnp.int32, (NSA_H, kc), 0)
    head_lane = lax.broadcasted_iota(jnp.int32, (NSA_H, kc), 1)
    own = (head_lane >> SEL_SHIFT) == (head_row >> R_SHIFT)
    fold = jnp.where((lax.broadcasted_iota(jnp.int32, (kc, NSA_D), 0) & (NSA_D - 1))
                     == lax.broadcasted_iota(jnp.int32, (kc, NSA_D), 1), 1.0, 0.0)
    hi = lax.Precision.HIGHEST

    def rounded(x):
        return x.astype(BF16).astype(F32)

    snew = snew_ref[...]
    s_new = jnp.sum(q.astype(F32) * rounded(snew[:, :kc]), axis=-1, keepdims=True)
    v_new = jnp.dot(jnp.where(own, rounded(snew[:, kc:]), 0.0), fold, precision=hi,
                    preferred_element_type=F32)
    half_of_lane = lax.broadcasted_iota(jnp.int32, (1, PAGE), 1) >> SEL_SHIFT
    row64 = lax.broadcasted_iota(jnp.int32, (NSA_H, NSA_D), 0)
    f_sel = jnp.zeros((NSA_H, NSA_D), F32)
    for g in range(NSA_G):
        mine = slabs[g * n_past:(g + 1) * n_past]
        k_cat = jnp.concatenate([r[0] for r in mine], axis=1).astype(BF16)
        v_cat = jnp.concatenate([r[1] for r in mine], axis=1).astype(BF16)
        bias = jnp.concatenate(
            [jnp.where(half_of_lane == (idx_ref[b, g * n_past + k] & 1), 0.0, NEG_INF)
             for k in range(n_past)], axis=1)
        sb = _dot(q64, k_cat) + bias
        m = jnp.maximum(jnp.max(sb, axis=-1, keepdims=True), s_new)
        p = jnp.exp2(sb - m)
        p_new = jnp.exp2(s_new - m)
        l = jnp.sum(p, axis=-1, keepdims=True) + p_new
        o = _dot_nt(p.astype(BF16), v_cat) + rounded(p_new) * v_new
        f_sel = jnp.where((row64 >> R_SHIFT) == g, o / jnp.maximum(l, 1e-30), f_sel)

    wb = wcache_ref.shape[0]
    wbuf_ref[0:wb, :] = wcache_ref[...]
    wbuf_ref[wb:wb + 1, :] = wnew_ref[...]
    wbuf_ref[wb + 1:, :] = jnp.zeros((wbuf_ref.shape[0] - wb - 1, 2 * kc), F32)
    s = _dot_nt(q, wbuf_ref[:, :kc].astype(BF16))
    ok = lax.broadcasted_iota(jnp.int32, s.shape, 1) < wb + 1
    s = jnp.where(ok, s, NEG_INF)
    p = jnp.where(ok, jnp.exp2(s - jnp.max(s, axis=-1, keepdims=True)), 0.0)
    p = p / jnp.maximum(jnp.sum(p, axis=-1, keepdims=True), 1e-30)
    o_win = _dot(p.astype(BF16), wbuf_ref[:, kc:].astype(BF16))
    f_win = jnp.dot(jnp.where(own, o_win, 0.0), fold, precision=hi, preferred_element_type=F32)
    wout_ref[...] = wbuf_ref[1:wb + 1, :]

    gts = gates_ref[...]
    out_ref[...] = (gts[:, 0:1] * ocmp_ref[:, 0:NSA_D] + gts[:, 1:2] * f_sel + gts[:, 2:3] * f_win)


def _nsa_sample_attn(idx, page_table, slc_cache, q64, q_cmp, slc_new, win_cache, win_new, gates, ocmp):
    bsz = page_table.shape[0]
    n_past = idx.shape[1] // NSA_G
    kc = NSA_G * NSA_D
    wb = win_cache.shape[1]
    assert PAGE == 2 * SEL_BLK

    def slab_map(j):
        def f(b, idx_ref, pt_ref):
            return (pt_ref[b, idx_ref[b, j] // 2], 0, j // n_past, 0, 0)
        return f

    pad = lambda n: -(-n // LANES) * LANES
    in_specs = [pl.BlockSpec((None, 2, None, NSA_D, PAGE), slab_map(j)) for j in range(NSA_G * n_past)]
    in_specs += [pl.BlockSpec((None, NSA_H, NSA_D), lambda b, i, p: (b, 0, 0)),
                 pl.BlockSpec((None, NSA_H, kc), lambda b, i, p: (b, 0, 0)),
                 pl.BlockSpec((None, 1, 2 * kc), lambda b, i, p: (b, 0, 0)),
                 pl.BlockSpec((None, wb, 2 * kc), lambda b, i, p: (b, 0, 0)),
                 pl.BlockSpec((None, 1, 2 * kc), lambda b, i, p: (b, 0, 0)),
                 pl.BlockSpec((None, NSA_H, 3), lambda b, i, p: (b, 0, 0)),
                 pl.BlockSpec((None, NSA_H, LANES), lambda b, i, p: (b, 0, 0))]
    grid_spec = pltpu.PrefetchScalarGridSpec(
        num_scalar_prefetch=2,
        grid=(bsz,),
        in_specs=in_specs,
        out_specs=(pl.BlockSpec((None, NSA_H, NSA_D), lambda b, i, p: (b, 0, 0)),
                   pl.BlockSpec((None, wb, 2 * kc), lambda b, i, p: (b, 0, 0))),
        scratch_shapes=[pltpu.VMEM((pad(wb + 1), 2 * kc), F32)],
    )
    return pl.pallas_call(
        functools.partial(_nsa_sample_attn_kernel, n_past=n_past),
        grid_spec=grid_spec,
        out_shape=(jax.ShapeDtypeStruct((bsz, NSA_H, NSA_D), F32),
                   jax.ShapeDtypeStruct((bsz, wb, 2 * kc), F32)),
        compiler_params=_params(("arbitrary",)),
        name="nsa_sample_attn",
    )(idx, page_table, *([slc_cache] * (NSA_G * n_past)), q64, q_cmp, slc_new, win_cache, win_new, gates, ocmp)


def _proj_ln_kernel(a_ref, w_ref, x_ref, g_ref, b_ref, out_ref, *, alpha):
    y = _dot(a_ref[...].astype(BF16), w_ref[...])
    out_ref[...] = _layer_norm(alpha * x_ref[...] + y, g_ref[...], b_ref[...])


def _gla_proj_ln_kernel(o_ref, rs_ref, ng_ref, w_ref, x_ref, g_ref, b_ref, out_ref, *, alpha):
    a = (o_ref[...] * ng_ref[...] * rs_ref[...]).astype(BF16)
    y = _dot(a, w_ref[...])
    out_ref[...] = _layer_norm(alpha * x_ref[...] + y, g_ref[...], b_ref[...])


def _proj_ln(a, w, x, g, b, *, tm, alpha):
    n, d = x.shape
    row = lambda i: (i, 0)
    const = lambda i: (0, 0)
    return pl.pallas_call(
        functools.partial(_proj_ln_kernel, alpha=alpha),
        grid=(n // tm,),
        in_specs=[pl.BlockSpec((tm, a.shape[1]), row), pl.BlockSpec(w.shape, const),
                  pl.BlockSpec((tm, d), row), pl.BlockSpec((1, d), const), pl.BlockSpec((1, d), const)],
        out_specs=pl.BlockSpec((tm, d), row),
        out_shape=jax.ShapeDtypeStruct((n, d), F32),
        compiler_params=_params(("arbitrary",)),
        name="proj_ln",
    )(a, w, x, g, b)


def _gla_proj_ln(o, rs, ng, w, x, g, b, *, tm, alpha):
    n, d = x.shape
    row = lambda i: (i, 0)
    const = lambda i: (0, 0)
    return pl.pallas_call(
        functools.partial(_gla_proj_ln_kernel, alpha=alpha),
        grid=(n // tm,),
        in_specs=[pl.BlockSpec((tm, o.shape[1]), row), pl.BlockSpec((tm, o.shape[1]), row),
                  pl.BlockSpec((1, o.shape[1]), const), pl.BlockSpec(w.shape, const),
                  pl.BlockSpec((tm, d), row), pl.BlockSpec((1, d), const), pl.BlockSpec((1, d), const)],
        out_specs=pl.BlockSpec((tm, d), row),
        out_shape=jax.ShapeDtypeStruct((n, d), F32),
        compiler_params=_params(("arbitrary",)),
        name="gla_proj_ln",
    )(o, rs, ng, w, x, g, b)


def _ffn_kernel(x_ref, wg_ref, wu_ref, wd_ref, g_ref, b_ref, out_ref, acc_ref, xb_ref, *, alpha):
    k = pl.program_id(1)

    @pl.when(k == 0)
    def _():
        xb_ref[...] = x_ref[...].astype(BF16)
        acc_ref[...] = jnp.zeros_like(acc_ref)

    xb = xb_ref[...]
    h = _silu(_dot(xb, wg_ref[...])) * _dot(xb, wu_ref[...])
    acc_ref[...] += _dot(h.astype(BF16), wd_ref[...])

    @pl.when(k == pl.num_programs(1) - 1)
    def _():
        out_ref[...] = _layer_norm(alpha * x_ref[...] + acc_ref[...], g_ref[...], b_ref[...])


def _ffn(x, w_gu, w_down, g, b, *, layer, tm, alpha):
    n, d = x.shape
    d_ff = w_down.shape[1]
    splits = 2
    tf = d_ff // splits
    assert tf * splits == d_ff and tf % LANES == 0
    return pl.pallas_call(
        functools.partial(_ffn_kernel, alpha=alpha),
        grid=(n // tm, splits),
        in_specs=[pl.BlockSpec((tm, d), lambda i, k: (i, 0)),
                  pl.BlockSpec((None, d, tf), lambda i, k: (layer, 0, k)),
                  pl.BlockSpec((None, d, tf), lambda i, k: (layer, 0, splits + k)),
                  pl.BlockSpec((None, tf, d), lambda i, k: (layer, k, 0)),
                  pl.BlockSpec((1, d), lambda i, k: (0, 0)),
                  pl.BlockSpec((1, d), lambda i, k: (0, 0))],
        out_specs=pl.BlockSpec((tm, d), lambda i, k: (i, 0)),
        out_shape=jax.ShapeDtypeStruct((n, d), F32),
        scratch_shapes=[pltpu.VMEM((tm, d), F32), pltpu.VMEM((tm, d), BF16)],
        compiler_params=_params(("arbitrary", "arbitrary")),
        name="ffn",
    )(x, w_gu, w_gu, w_down, g, b)


def _gla_log_decay(low, w2_ref, b2_ref):
    z = _dot(low.astype(BF16), w2_ref[...]) + b2_ref[...]
    return jax.nn.log_sigmoid(z) / GLA_TAU


def _gla_proj_prompt_kernel(x_ref, w_ref, w2_ref, b2_ref, qi_ref, ki_ref, ks_ref, v_ref, rs_ref,
                            al_ref, *, dk, dv, scale):
    tm = x_ref.shape[0]
    xb = x_ref[...].astype(BF16)
    q = _dot(xb, w_ref[:, 0:dk])
    k = _dot(xb, w_ref[:, dk:2 * dk])
    v_ref[...] = _dot(xb, w_ref[:, 2 * dk:2 * dk + dv]).astype(BF16)
    rs_ref[...] = _silu(_dot(xb, w_ref[:, 2 * dk + dv:2 * dk + 2 * dv]))
    low = _dot(xb, w_ref[:, 2 * dk + 2 * dv:])
    la = _gla_log_decay(low, w2_ref, b2_ref)

    rin = lax.broadcasted_iota(jnp.int32, (tm, dk), 0) & (GLA_CHUNK - 1)
    lb = la
    s = 1
    while s < GLA_CHUNK:
        lb = lb + jnp.where(rin >= s, pltpu.roll(lb, s, 0), 0.0)
        s *= 2
    lb3 = lb.reshape(tm // GLA_CHUNK, GLA_CHUNK, dk)
    last = lb3[:, GLA_CHUNK - 1:GLA_CHUNK, :]
    qi_ref[...] = (q * jnp.exp(lb) * scale).astype(BF16)
    ki_ref[...] = (k * jnp.exp(-lb)).astype(BF16)
    ks_ref[...] = (k * jnp.exp(last - lb3).reshape(tm, dk)).astype(BF16)
    al_ref[...] = jnp.exp(last.reshape(tm // GLA_CHUNK, dk))


def _gla_proj_prompt(x, w, w2, b2, *, tm, dk, dv):
    n, d = x.shape
    row = lambda i: (i, 0)
    const = lambda i: (0, 0)
    return pl.pallas_call(
        functools.partial(_gla_proj_prompt_kernel, dk=dk, dv=dv, scale=(dk // GLA_H) ** -0.5),
        grid=(n // tm,),
        in_specs=[pl.BlockSpec((tm, d), row), pl.BlockSpec(w.shape, const),
                  pl.BlockSpec(w2.shape, const), pl.BlockSpec(b2.shape, const)],
        out_specs=(pl.BlockSpec((tm, dk), row), pl.BlockSpec((tm, dk), row), pl.BlockSpec((tm, dk), row),
                   pl.BlockSpec((tm, dv), row), pl.BlockSpec((tm, dv), row),
                   pl.BlockSpec((tm // GLA_CHUNK, dk), row)),
        out_shape=(jax.ShapeDtypeStruct((n, dk), BF16), jax.ShapeDtypeStruct((n, dk), BF16),
                   jax.ShapeDtypeStruct((n, dk), BF16), jax.ShapeDtypeStruct((n, dv), BF16),
                   jax.ShapeDtypeStruct((n, dv), F32),
                   jax.ShapeDtypeStruct((n // GLA_CHUNK, dk), F32)),
        compiler_params=_params(("arbitrary",)),
        name="gla_proj_prompt",
    )(x, w, w2, b2)


def _gla_proj_sample_kernel(x_ref, w_ref, w2_ref, b2_ref, q_ref, k_ref, v_ref, rs_ref, la_ref, *, dk, dv):
    xb = x_ref[...].astype(BF16)
    q_ref[...] = _dot(xb, w_ref[:, 0:dk])
    k_ref[...] = _dot(xb, w_ref[:, dk:2 * dk])
    v_ref[...] = _dot(xb, w_ref[:, 2 * dk:2 * dk + dv])
    rs_ref[...] = _silu(_dot(xb, w_ref[:, 2 * dk + dv:2 * dk + 2 * dv]))
    low = _dot(xb, w_ref[:, 2 * dk + 2 * dv:])
    la_ref[...] = _gla_log_decay(low, w2_ref, b2_ref)


def _gla_proj_sample(x, w, w2, b2, *, dk, dv):
    n, d = x.shape
    full = lambda s: pl.BlockSpec(s, lambda i: (0,) * len(s))
    return pl.pallas_call(
        functools.partial(_gla_proj_sample_kernel, dk=dk, dv=dv),
        grid=(1,),
        in_specs=[full((n, d)), full(w.shape), full(w2.shape), full(b2.shape)],
        out_specs=(full((n, dk)), full((n, dk)), full((n, dv)), full((n, dv)), full((n, dk))),
        out_shape=(jax.ShapeDtypeStruct((n, dk), F32), jax.ShapeDtypeStruct((n, dk), F32),
                   jax.ShapeDtypeStruct((n, dv), F32), jax.ShapeDtypeStruct((n, dv), F32),
                   jax.ShapeDtypeStruct((n, dk), F32)),
        compiler_params=_params(("arbitrary",)),
        name="gla_proj_sample",
    )(x, w, w2, b2)


def _gla_rec_kernel(qi_ref, ki_ref, ks_ref, v_ref, al_ref, o_ref, st_out_ref, st_ref):
    t = pl.program_id(2)

    @pl.when(t == 0)
    def _():
        st_ref[...] = jnp.zeros_like(st_ref)

    c = GLA_CHUNK
    ct = qi_ref.shape[0]
    n = ct // c
    qi, ki, vv = qi_ref[...], ki_ref[...], v_ref[...]
    row = lax.broadcasted_iota(jnp.int32, (ct, ct), 0)
    col = lax.broadcasted_iota(jnp.int32, (ct, ct), 1)
    keep = (row >= col) & ((row >> GLA_CHUNK_SHIFT) == (col >> GLA_CHUNK_SHIFT))
    att = jnp.where(keep, _dot_nt(qi, ki), 0.0)
    o_intra = _dot(att.astype(BF16), vv)
    incs = [_dot_tn(vv[i * c:(i + 1) * c], ks_ref[i * c:(i + 1) * c, :]) for i in range(n)]
    st = st_ref[...]
    entering = []
    for i in range(n):
        entering.append(st.astype(BF16))
        st = al_ref[i:i + 1, :] * st + incs[i]
    st_ref[...] = st
    o = o_intra + jnp.concatenate(
        [_dot_nt(qi[i * c:(i + 1) * c], entering[i]) for i in range(n)], axis=0)
    o_ref[...] = o * lax.rsqrt(jnp.mean(o * o, axis=-1, keepdims=True) + LN_EPS)

    @pl.when(t == pl.num_programs(2) - 1)
    def _():
        st_out_ref[...] = st


def _gla_rec(qi, ki, ks, v, al, *, batch, seq, dkh, dvh):
    ct = 256
    nct = seq // ct
    cpt = ct // GLA_CHUNK
    tok = lambda b, h, t: (b * nct + t, h)
    return pl.pallas_call(
        _gla_rec_kernel,
        grid=(batch, GLA_H, nct),
        in_specs=[pl.BlockSpec((ct, dkh), tok), pl.BlockSpec((ct, dkh), tok), pl.BlockSpec((ct, dkh), tok),
                  pl.BlockSpec((ct, dvh), tok), pl.BlockSpec((cpt, dkh), tok)],
        out_specs=(pl.BlockSpec((ct, dvh), tok),
                   pl.BlockSpec((None, None, dvh, dkh), lambda b, h, t: (b, h, 0, 0))),
        out_shape=(jax.ShapeDtypeStruct((batch * seq, GLA_H * dvh), F32),
                   jax.ShapeDtypeStruct((batch, GLA_H, dvh, dkh), F32)),
        scratch_shapes=[pltpu.VMEM((dvh, dkh), F32)],
        compiler_params=_params(("arbitrary", "arbitrary", "arbitrary")),
        name="gla_rec",
    )(qi, ki, ks, v, al)


def _gla_step_kernel(q_ref, k_ref, la_ref, v_ref, s0_ref, o_ref, s_ref, *, scale):
    for h in range(s0_ref.shape[0]):
        la = la_ref[h]
        a = jnp.exp(la)
        k = k_ref[h]
        qi = q_ref[h] * a * scale
        ki = k * jnp.exp(-la)
        v = v_ref[h]
        s0 = s0_ref[h]
        att = jnp.sum(qi * ki, axis=0, keepdims=True)
        o = att * v + jnp.sum(qi * s0, axis=0, keepdims=True)
        s_ref[h] = a * s0 + k * v
        o_ref[h] = o * lax.rsqrt(jnp.mean(o * o, axis=-1, keepdims=True) + LN_EPS)


def _gla_step(q, k, la, v, s0, *, scale):
    bsz, h, dkh, dvh = s0.shape
    col = pl.BlockSpec((None, h, dkh, 1), lambda b: (b, 0, 0, 0))
    rowv = pl.BlockSpec((None, h, 1, dvh), lambda b: (b, 0, 0, 0))
    mat = pl.BlockSpec((None, h, dkh, dvh), lambda b: (b, 0, 0, 0))
    return pl.pallas_call(
        functools.partial(_gla_step_kernel, scale=scale),
        grid=(bsz,),
        in_specs=[col, col, col, rowv, mat],
        out_specs=(rowv, mat),
        out_shape=(jax.ShapeDtypeStruct((bsz, h, 1, dvh), F32),
                   jax.ShapeDtypeStruct((bsz, h, dkh, dvh), F32)),
        compiler_params=_params(("arbitrary",)),
        name="gla_step",
    )(q, k, la, v, s0)


def _rope_tables(pos):
    half = NSA_D // 2
    inv = ROPE_THETA ** (-jnp.arange(half, dtype=F32) / half)
    ang = pos.astype(F32)[:, None] * inv[None, :]
    cos, sin = jnp.cos(ang), jnp.sin(ang)
    zero = jnp.zeros_like(sin)
    cos_t = jnp.tile(cos, (1, LANES // half))
    sin_lo = jnp.tile(jnp.concatenate([-sin, zero], axis=1), (1, LANES // NSA_D))
    sin_hi = jnp.tile(jnp.concatenate([zero, sin], axis=1), (1, LANES // NSA_D))
    return cos_t, sin_lo, sin_hi


def _prep_nsa_weights(w_in, b_gate, pe, w1, w2):
    d = w_in.shape[0]
    qc = NSA_H * NSA_D
    kc = NSA_G * NSA_D
    gate_off = qc + 6 * kc
    wg = w_in[:, gate_off:].reshape(d, 3, NSA_G, NSA_R).transpose(0, 2, 1, 3).reshape(d, NSA_G, 3 * NSA_R)
    wg = jnp.pad(wg, ((0, 0), (0, 0), (0, LANES - 3 * NSA_R))).reshape(d, NSA_G * LANES)
    bg = b_gate.reshape(3, NSA_G, NSA_R).transpose(1, 0, 2).reshape(NSA_G, 3 * NSA_R)
    bg = jnp.pad(bg, ((0, 0), (0, LANES - 3 * NSA_R))).reshape(1, NSA_G * LANES)
    w = jnp.concatenate([w_in[:, :gate_off], wg], axis=1).astype(BF16)
    eye = jnp.eye(NSA_G, dtype=F32)
    pe_x = jnp.tile(pe.transpose(1, 0, 2), (1, 1, NSA_G))
    w1_c = w1.transpose(1, 0, 2, 3).astype(BF16)
    w1_bd = jnp.zeros((2, CMP_BLK, kc, kc), BF16)
    for g in range(NSA_G):
        w1_bd = w1_bd.at[:, :, g * NSA_D:(g + 1) * NSA_D, g * NSA_D:(g + 1) * NSA_D].set(w1_c)
    w2_dup = jnp.concatenate([w2, w2], axis=-1)
    w2_bd = jnp.einsum('cef,gh->cgehf', w2_dup, eye).reshape(2, kc, 2 * kc).astype(BF16)
    return w, bg, pe_x, w1_bd, w2_bd


def _prep_vt_weights(w_in):
    qc = NSA_H * NSA_D
    kc = NSA_G * NSA_D
    out = []
    for branch in (1, 2):
        lo = qc + branch * 2 * kc + kc
        out.append(w_in[:, lo:lo + kc].T)
    return jnp.stack(out).astype(BF16)


def _prep_page_compress_weights(pe, w1, w2):
    nd = NSA_D // 2
    eye = jnp.eye(PAGE // CMP_BLK, dtype=F32)
    pe_n = jnp.broadcast_to(pe.transpose(1, 2, 0).reshape(2, nd, 2, 1, CMP_BLK),
                            (2, nd, 2, PAGE // CMP_BLK, CMP_BLK)).reshape(2, nd, 2 * PAGE)
    w1c = w1.transpose(1, 2, 0, 3).reshape(2, nd, 2, CMP_BLK, NSA_D).astype(BF16)
    w1_n = jnp.zeros((2, nd, 2 * PAGE, PAGE), BF16)
    for p in range(2):
        for h in range(PAGE // CMP_BLK):
            r0 = (p * (PAGE // CMP_BLK) + h) * CMP_BLK
            w1_n = w1_n.at[:, :, r0:r0 + CMP_BLK, h * NSA_D:(h + 1) * NSA_D].set(w1c[:, :, p])
    w2_n = jnp.einsum('cef,hk->chekf', w2, eye).reshape(2, PAGE, PAGE).astype(BF16)
    return pe_n, w1_n, w2_n


def kernel(x_prompt, x_sample, cache_cmp_kv, cache_slc_kv, cache_win_kv, state_gla, page_table,
           nsa_w_in, nsa_b_gate, nsa_pe_cmp, nsa_w_cmp1, nsa_w_cmp2, nsa_w_out,
           gla_w_in, gla_w_gate2, gla_b_gate2, gla_norm_g, gla_w_out,
           ffn_w_gu, ffn_w_down, ln_g, ln_b):
    batch, seq, d = x_prompt.shape
    bsz, dec_seq, _ = x_sample.shape
    depth = ffn_w_gu.shape[0]
    n_pool = cache_cmp_kv.shape[1]
    n_pages = page_table.shape[1]
    past_len = n_pages * PAGE
    kc = NSA_G * NSA_D
    assert dec_seq == 1 and depth == 2 and d == NSA_H * NSA_D
    assert seq % 512 == 0 and seq >= WINDOW and past_len % SEL_BLK == 0
    assert cache_win_kv.shape[2] == WINDOW and past_len // SEL_BLK >= N_SEL
    alpha = (2.0 * depth) ** 0.25
    tm = 512
    nbp = past_len // CMP_BLK

    xp = x_prompt.reshape(batch * seq, d)
    xs = x_sample.reshape(bsz, d)
    ln_g = ln_g.reshape(depth, 2, 1, d)
    ln_b = ln_b.reshape(depth, 2, 1, d)
    w_gu = ffn_w_gu.astype(BF16)
    w_dn = ffn_w_down.astype(BF16)

    w0, bg0, pe_x, w1_bd, w2_bd = _prep_nsa_weights(nsa_w_in[0], nsa_b_gate[0], nsa_pe_cmp[0],
                                                    nsa_w_cmp1[0], nsa_w_cmp2[0])
    w_out0 = nsa_w_out[0].astype(BF16)
    tabs_p = _rope_tables(jnp.arange(seq, dtype=jnp.int32))
    tabs_s = _rope_tables(jnp.full((bsz,), past_len, dtype=jnp.int32))

    seq_tiles = seq // tm
    (qraw, qrot, cmp_p, slc_p, win_p, gates_p, skd, wkd, svt, wvt) = _nsa_proj(
        xp, w0, bg0, *tabs_p, _prep_vt_weights(nsa_w_in[0]), tm=tm, seq_tiles=seq_tiles,
        win_rows=batch * WINDOW, win_map=lambda i: (i // seq_tiles, 0), attn_kt=ATTN_KT)
    ckv_p = _compress_prompt(cmp_p, pe_x, w1_bd, w2_bd, batch=batch, seq=seq)
    o_p = _nsa_attn_prompt(qraw, qrot, ckv_p, skd, svt, wkd, wvt, gates_p, batch=batch, seq=seq,
                           tq=ATTN_TQ, kt=ATTN_KT)
    yp = _proj_ln(o_p, w_out0, xp, ln_g[0, 0], ln_b[0, 0], tm=tm, alpha=alpha)

    (qraw_s, qrot_s, cmp_s, slc_s, win_s, gates_s) = _nsa_proj(
        xs, w0, bg0, *tabs_s, tm=bsz, seq_tiles=1, win_rows=bsz, win_map=lambda i: (i, 0))
    pages_fm = lambda c: jnp.transpose(c[0], (0, 2, 3, 4, 1))
    pe_n, w1_n, w2_n = _prep_page_compress_weights(nsa_pe_cmp[0], nsa_w_cmp1[0], nsa_w_cmp2[0])
    ckv_s = _compress_pages(pages_fm(cache_cmp_kv).reshape(n_pool, 2, kc, PAGE), page_table, pe_n, w1_n, w2_n)
    q16 = qraw_s.reshape(bsz, NSA_H, NSA_D)
    zeros16 = jnp.zeros_like(q16)
    q_e = jnp.concatenate([q16, zeros16], axis=-1)
    q_o = jnp.concatenate([zeros16, q16], axis=-1)
    ocmp_s, idx4 = _nsa_sample_select(q_e, q_o, ckv_s, bsz=bsz)
    idx = idx4[:, :, :N_SEL - 1, 0].reshape(bsz, NSA_G * (N_SEL - 1))
    q4r = qrot_s.reshape(bsz, NSA_G, NSA_R, NSA_D)
    q_cmp = jnp.einsum('bgrd,gh->bgrhd', q4r, jnp.eye(NSA_G, dtype=BF16)).reshape(bsz, NSA_H, kc)
    gates_s3 = gates_s.reshape(bsz, NSA_G, LANES)[:, :, :3 * NSA_R].reshape(bsz, NSA_G, 3, NSA_R)
    gates_s3 = gates_s3.transpose(0, 1, 3, 2).reshape(bsz, NSA_H, 3)
    o_s, win_buf_s = _nsa_sample_attn(
        idx, page_table, pages_fm(cache_slc_kv), qrot_s.reshape(bsz, NSA_H, NSA_D), q_cmp,
        slc_s.reshape(bsz, 1, 2 * kc), cache_win_kv[0].reshape(bsz, WINDOW, 2 * kc),
        win_s.reshape(bsz, 1, 2 * kc), gates_s3, ocmp_s)
    ys = _proj_ln(o_s.reshape(bsz, d), w_out0, xs, ln_g[0, 0], ln_b[0, 0], tm=bsz, alpha=alpha)

    yp = _ffn(yp, w_gu, w_dn, ln_g[0, 1], ln_b[0, 1], layer=0, tm=tm, alpha=alpha)
    ys = _ffn(ys, w_gu, w_dn, ln_g[0, 1], ln_b[0, 1], layer=0, tm=bsz, alpha=alpha)

    dk = gla_w_gate2.shape[2]
    dv = gla_w_out.shape[1]
    dkh, dvh = dk // GLA_H, dv // GLA_H
    gw = jnp.pad(gla_w_in[0], ((0, 0), (0, LANES - GLA_RANK))).astype(BF16)
    gw2 = jnp.pad(gla_w_gate2[0], ((0, LANES - GLA_RANK), (0, 0))).astype(BF16)
    gb2 = gla_b_gate2[0].reshape(1, dk)
    g_out = gla_w_out[0].astype(BF16)
    ng = gla_norm_g[0].reshape(1, dv)

    qi, ki, ks, vb, rs_p, al = _gla_proj_prompt(yp, gw, gw2, gb2, tm=tm, dk=dk, dv=dv)
    on_p, st_p = _gla_rec(qi, ki, ks, vb, al, batch=batch, seq=seq, dkh=dkh, dvh=dvh)
    yp = _gla_proj_ln(on_p, rs_p, ng, g_out, yp, ln_g[1, 0], ln_b[1, 0], tm=tm, alpha=alpha)

    q_s, k_s, v_s, rs_s, la_s = _gla_proj_sample(ys, gw, gw2, gb2, dk=dk, dv=dv)
    colv = lambda a: a.reshape(bsz, GLA_H, dkh, 1)
    on_s, st_s = _gla_step(colv(q_s), colv(k_s), colv(la_s), v_s.reshape(bsz, GLA_H, 1, dvh),
                           state_gla[0], scale=dkh ** -0.5)
    ys = _gla_proj_ln(on_s.reshape(bsz, dv), rs_s, ng, g_out, ys, ln_g[1, 0], ln_b[1, 0], tm=bsz, alpha=alpha)

    yp = _ffn(yp, w_gu, w_dn, ln_g[1, 1], ln_b[1, 1], layer=1, tm=tm, alpha=alpha)
    ys = _ffn(ys, w_gu, w_dn, ln_g[1, 1], ln_b[1, 1], layer=1, tm=bsz, alpha=alpha)

    kv6 = lambda a, rows: a.reshape(1, rows[0], rows[1], 2, NSA_G, NSA_D)
    return (yp.reshape(batch, seq, d), ys.reshape(bsz, 1, d),
            kv6(cmp_p, (batch, seq)), kv6(slc_p, (batch, seq)), kv6(win_p, (batch, WINDOW)),
            jnp.swapaxes(st_p, 2, 3)[None],
            kv6(cmp_s, (bsz, 1)), kv6(slc_s, (bsz, 1)), kv6(win_buf_s, (bsz, WINDOW)),
            st_s[None])
```

```python
import functools

import jax
import jax.numpy as jnp
from jax import lax
from jax.experimental import pallas as pl
from jax.experimental.pallas import tpu as pltpu

PAGE = 128
NSA_H = 16
NSA_D = 64
NSA_G = 4
NSA_R = NSA_H // NSA_G
CMP_BLK = 64
SEL_BLK = 64
SEL_SHIFT = 6
R_SHIFT = 2
LANE_SHIFT = 7
N_SEL = 16
WINDOW = 512
ATTN_SCALE = NSA_D ** -0.5
LOG2E = 1.4426950408889634
ROPE_THETA = 10000.0
NEG_INF = -1e30
SEL_FORCE = 1e4

GLA_H = 4
GLA_RANK = 16
GLA_TAU = 16.0
GLA_CHUNK = 32
GLA_CHUNK_SHIFT = 5
LN_EPS = 1e-5

LANES = 128
HALF = LANES // 2
ONES_ROWS = 16
ATTN_TQ = 128
ATTN_KT = 256
ATTN_UNROLL = 4
VMEM_LIMIT = 56 * 1024 * 1024

F32 = jnp.float32
BF16 = jnp.bfloat16

NT_DIMS = (((1,), (1,)), ((), ()))
TN_DIMS = (((0,), (0,)), ((), ()))


def _params(sem):
    return pltpu.CompilerParams(dimension_semantics=sem, vmem_limit_bytes=VMEM_LIMIT)


def _dot(a, b):
    return jnp.dot(a, b, preferred_element_type=F32)


def _dot_nt(a, b):
    return lax.dot_general(a, b, NT_DIMS, preferred_element_type=F32)


def _dot_tn(a, b):
    return lax.dot_general(a, b, TN_DIMS, preferred_element_type=F32)


def _layer_norm(z, g, b):
    mu = jnp.mean(z, axis=-1, keepdims=True)
    zc = z - mu
    var = jnp.mean(zc * zc, axis=-1, keepdims=True)
    return zc * lax.rsqrt(var + LN_EPS) * g + b


def _silu(x):
    return x * jax.nn.sigmoid(x)


def _rope(x, cos_t, sin_lo, sin_hi):
    out = []
    for p in range(x.shape[1] // LANES):
        blk = x[:, p * LANES:(p + 1) * LANES]
        x_up = pltpu.roll(blk, LANES - NSA_D // 2, 1)
        x_dn = pltpu.roll(blk, NSA_D // 2, 1)
        out.append(blk * cos_t + x_up * sin_lo + x_dn * sin_hi)
    return out[0] if len(out) == 1 else jnp.concatenate(out, axis=1)


def _dup_heads(x):
    rows, c = x.shape
    lane = lax.broadcasted_iota(jnp.int32, (rows, LANES), 1)
    lo = lane < HALF
    out = []
    for p in range(c // LANES):
        blk = x[:, p * LANES:(p + 1) * LANES]
        sw = pltpu.roll(blk, HALF, 1)
        out.append(jnp.where(lo, blk, sw))
        out.append(jnp.where(lo, sw, blk))
    return jnp.concatenate(out, axis=1)


def _nsa_proj_kernel(x_ref, w_ref, bg_ref, cos_ref, slo_ref, shi_ref, *refs, attn_kt):
    qc = NSA_H * NSA_D
    kc = NSA_G * NSA_D
    if attn_kt is None:
        qraw_ref, qrot_ref, cmp_ref, slc_ref, win_ref, gates_ref = refs
    else:
        (wvt_ref, qraw_ref, qrot_ref, cmp_ref, slc_ref, win_ref, gates_ref,
         skd_ref, wkd_ref, svt_ref, wvt_out_ref) = refs
    xb = x_ref[...].astype(BF16)
    cos_t, slo, shi = cos_ref[...], slo_ref[...], shi_ref[...]

    q = _dot(xb, w_ref[:, 0:qc])
    qraw_ref[...] = (q * ATTN_SCALE).astype(BF16)
    qrot_ref[...] = (_rope(q, cos_t, slo, shi) * (ATTN_SCALE * LOG2E)).astype(BF16)

    cmp_ref[...] = _dot(xb, w_ref[:, qc:qc + 2 * kc])

    slc = _dot(xb, w_ref[:, qc + 2 * kc:qc + 4 * kc])
    sk = _rope(slc[:, :kc], cos_t, slo, shi)
    slc_ref[:, :kc] = sk
    slc_ref[:, kc:] = slc[:, kc:]

    win = _dot(xb, w_ref[:, qc + 4 * kc:qc + 6 * kc])
    wk = _rope(win[:, :kc], cos_t, slo, shi)
    win_ref[:, :kc] = wk
    win_ref[:, kc:] = win[:, kc:]

    gz = _dot(xb, w_ref[:, qc + 6 * kc:qc + 6 * kc + NSA_G * LANES]) + bg_ref[...]
    gates_ref[...] = jax.nn.sigmoid(gz)

    if attn_kt is not None:
        skd_ref[...] = _dup_heads(sk).astype(BF16)
        wkd_ref[...] = _dup_heads(wk).astype(BF16)
        for src, dst in ((0, svt_ref), (1, wvt_out_ref)):
            vt = _dot_nt(wvt_ref[src], xb).astype(BF16)
            for t in range(x_ref.shape[0] // attn_kt):
                dst[t] = vt[:, t * attn_kt:(t + 1) * attn_kt]


def _nsa_proj(x, w, bg, cos_t, slo, shi, wvt=None, *, tm, seq_tiles, win_rows, win_map, attn_kt=None):
    n, d = x.shape
    qc = NSA_H * NSA_D
    kc2 = 2 * NSA_G * NSA_D
    kd = NSA_G * LANES
    row = lambda i: (i, 0)
    tab = lambda i: (i % seq_tiles, 0)
    const = lambda i: (0, 0)
    in_specs = [pl.BlockSpec((tm, d), row), pl.BlockSpec(w.shape, const), pl.BlockSpec(bg.shape, const),
                pl.BlockSpec((tm, LANES), tab), pl.BlockSpec((tm, LANES), tab), pl.BlockSpec((tm, LANES), tab)]
    out_shape = [jax.ShapeDtypeStruct((n, qc), BF16), jax.ShapeDtypeStruct((n, qc), BF16),
                 jax.ShapeDtypeStruct((n, kc2), F32), jax.ShapeDtypeStruct((n, kc2), F32),
                 jax.ShapeDtypeStruct((win_rows, kc2), F32), jax.ShapeDtypeStruct((n, kd), F32)]
    out_specs = [pl.BlockSpec((tm, qc), row), pl.BlockSpec((tm, qc), row),
                 pl.BlockSpec((tm, kc2), row), pl.BlockSpec((tm, kc2), row),
                 pl.BlockSpec((tm, kc2), win_map), pl.BlockSpec((tm, kd), row)]
    args = [x, w, bg, cos_t, slo, shi]
    if attn_kt is not None:
        assert tm % attn_kt == 0
        in_specs.append(pl.BlockSpec(wvt.shape, lambda i: (0, 0, 0)))
        args.append(wvt)
        tiles = tm // attn_kt
        out_shape += [jax.ShapeDtypeStruct((n, kd), BF16), jax.ShapeDtypeStruct((n, kd), BF16),
                      jax.ShapeDtypeStruct((n // attn_kt, kc2 // 2, attn_kt), BF16),
                      jax.ShapeDtypeStruct((n // attn_kt, kc2 // 2, attn_kt), BF16)]
        out_specs += [pl.BlockSpec((tm, kd), row), pl.BlockSpec((tm, kd), row),
                      pl.BlockSpec((tiles, kc2 // 2, attn_kt), lambda i: (i, 0, 0)),
                      pl.BlockSpec((tiles, kc2 // 2, attn_kt), lambda i: (i, 0, 0))]
    return pl.pallas_call(
        functools.partial(_nsa_proj_kernel, attn_kt=attn_kt),
        grid=(n // tm,),
        in_specs=in_specs,
        out_specs=tuple(out_specs),
        out_shape=tuple(out_shape),
        compiler_params=_params(("arbitrary",)),
        name="nsa_proj",
    )(*args)


def _compress_body(tok_refs, pe_ref, w1_ref, w2_ref, out_ref):
    m = tok_refs[0].shape[0] // CMP_BLK
    acc = jnp.zeros((m, w1_ref.shape[2]), F32)
    for l in range(CMP_BLK):
        xl = jnp.concatenate([r[pl.ds(l, m, stride=CMP_BLK), :] for r in tok_refs], axis=1)
        xl = xl + pe_ref[l:l + 1, :]
        acc = acc + _dot(xl.astype(BF16), w1_ref[l])
    h = _silu(acc)
    out_ref[...] = _dot(h.astype(BF16), w2_ref[...])


def _compress_prompt_kernel(tok0_ref, tok1_ref, pe_ref, w1_ref, w2_ref, out_ref):
    _compress_body((tok0_ref, tok1_ref), pe_ref, w1_ref, w2_ref, out_ref)


def _compress_prompt(cmp_kv, pe, w1, w2, *, batch, seq):
    kc = NSA_G * NSA_D
    nb = seq // CMP_BLK
    return pl.pallas_call(
        _compress_prompt_kernel,
        grid=(2, batch),
        in_specs=[pl.BlockSpec((seq, LANES), lambda c, b: (b, 2 * c)),
                  pl.BlockSpec((seq, LANES), lambda c, b: (b, 2 * c + 1)),
                  pl.BlockSpec((None, CMP_BLK, kc), lambda c, b: (c, 0, 0)),
                  pl.BlockSpec((None, CMP_BLK, kc, kc), lambda c, b: (c, 0, 0, 0)),
                  pl.BlockSpec((None, kc, 2 * kc), lambda c, b: (c, 0, 0))],
        out_specs=pl.BlockSpec((None, nb, 2 * kc), lambda c, b: (c, b, 0)),
        out_shape=jax.ShapeDtypeStruct((2, batch * nb, 2 * kc), F32),
        compiler_params=_params(("arbitrary", "arbitrary")),
        name="compress_prompt",
    )(cmp_kv, cmp_kv, pe, w1, w2)


def _compress_pages_kernel(pt_ref, *refs, pages):
    del pt_ref
    page_refs = refs[:pages]
    pe_ref, w1_ref, w2_ref, out_ref, buf_ref = refs[pages:]
    rows_pp = NSA_G * NSA_D
    for k in range(pages):
        buf_ref[k * rows_pp:(k + 1) * rows_pp, :] = page_refs[k][...]
    m = pages * NSA_G
    acc = jnp.zeros((m, LANES), F32)
    for dd in range(NSA_D // 2):
        x = jnp.concatenate([buf_ref[pl.ds(2 * dd, m, stride=NSA_D), :],
                             buf_ref[pl.ds(2 * dd + 1, m, stride=NSA_D), :]], axis=1)
        x = x + pe_ref[dd:dd + 1, :]
        acc = acc + _dot(x.astype(BF16), w1_ref[dd])
    out_ref[...] = _dot(_silu(acc).astype(BF16), w2_ref[...])


def _compress_pages(cache, page_table, pe, w1, w2):
    bsz, n_pages = page_table.shape
    pages = min(64, n_pages)
    groups = n_pages // pages
    gd = NSA_G * NSA_D
    assert PAGE == 2 * CMP_BLK and PAGE == LANES

    def page_map(k):
        return lambda c, b, h, pt: (pt[b, h * pages + k], c, 0, 0)

    in_specs = [pl.BlockSpec((None, None, gd, PAGE), page_map(k)) for k in range(pages)]
    in_specs += [pl.BlockSpec((None,) + pe.shape[1:], lambda c, b, h, pt: (c, 0, 0)),
                 pl.BlockSpec((None,) + w1.shape[1:], lambda c, b, h, pt: (c, 0, 0, 0)),
                 pl.BlockSpec((None,) + w2.shape[1:], lambda c, b, h, pt: (c, 0, 0))]
    grid_spec = pltpu.PrefetchScalarGridSpec(
        num_scalar_prefetch=1,
        grid=(2, bsz, groups),
        in_specs=in_specs,
        out_specs=pl.BlockSpec((None, pages * NSA_G, LANES), lambda c, b, h, pt: (c, b * groups + h, 0)),
        scratch_shapes=[pltpu.VMEM((pages * gd, PAGE), F32)],
    )
    return pl.pallas_call(
        functools.partial(_compress_pages_kernel, pages=pages),
        grid_spec=grid_spec,
        out_shape=jax.ShapeDtypeStruct((2, bsz * n_pages * NSA_G, LANES), F32),
        compiler_params=_params(("arbitrary", "arbitrary", "arbitrary")),
        name="compress_pages",
    )(page_table, *([cache] * pages), pe, w1, w2)


def _softmax_tile(carry, s_t, vt_tile, bias, tq):
    ms, acc = carry
    new_ms, ps, alphas = [], [], []
    for r in range(NSA_R):
        sb = s_t[:, r * tq:(r + 1) * tq] + bias
        m_new = jnp.maximum(ms[r], jnp.max(sb, axis=0, keepdims=True))
        ps.append(jnp.exp2(sb - m_new).astype(BF16))
        alphas.append(jnp.exp2(ms[r] - m_new))
        new_ms.append(m_new)
    acc = jnp.concatenate(alphas, axis=1) * acc + _dot(vt_tile, jnp.concatenate(ps, axis=1))
    return tuple(new_ms), acc


def _rank_blocks(score_ref, rank_ref, last_blk):
    nb, tq = score_ref.shape
    sub = lax.broadcasted_iota(jnp.int32, (8, tq), 0)
    rank_ref[...] = jnp.zeros((nb, tq), F32)
    for gi in range(nb // 8):
        @pl.when(gi * 8 <= last_blk)
        def _(gi=gi):
            rows8 = score_ref[gi * 8:(gi + 1) * 8, :]
            tops = [jnp.broadcast_to(rows8[u:u + 1, :], (8, tq)) for u in range(8)]
            for v in range(nb // 8):
                sv = score_ref[v * 8:(v + 1) * 8, :]
                cnt = jnp.zeros((8, tq), F32)
                for u in range(8):
                    if v > gi:
                        before = tops[u] >= sv
                    elif v < gi:
                        before = tops[u] > sv
                    else:
                        before = (tops[u] > sv) | ((tops[u] == sv) & (sub > u))
                    cnt = cnt + jnp.where(before, 1.0, 0.0)
                rank_ref[v * 8:(v + 1) * 8, :] += cnt


def _nsa_attn_kernel(qraw_ref, qrot_ref, ck_ref, cv_ref, sk_ref, svt_ref, wk_ref, wvt_ref,
                     gates_ref, out_ref, score_ref, rank_ref, *, tq, kt, n_sel, unroll):
    nb = ck_ref.shape[0]
    rows = NSA_R * tq
    q0 = pl.program_id(2) * tq

    lane = lax.broadcasted_iota(jnp.int32, (tq, LANES), 1)
    lo = lane < HALF

    def stack(ref):
        qa = ref[:, 0:LANES].astype(F32)
        qb = ref[:, LANES:2 * LANES].astype(F32)
        parts = [jnp.where(lo, qa, 0.0), jnp.where(lo, 0.0, qa),
                 jnp.where(lo, qb, 0.0), jnp.where(lo, 0.0, qb)]
        return jnp.concatenate(parts, axis=0).astype(BF16)

    q_raw = stack(qraw_ref)
    q_rot = stack(qrot_ref)
    ck = ck_ref[...].astype(BF16)
    cv = cv_ref[...].astype(BF16)

    st = _dot_nt(ck, q_raw)
    t_c = q0 + (lax.broadcasted_iota(jnp.int32, (nb, rows), 1) & (tq - 1))
    n_c = lax.broadcasted_iota(jnp.int32, (nb, rows), 0)
    mask_t = (n_c + 1) * CMP_BLK - 1 <= t_c
    st = jnp.where(mask_t, st, NEG_INF)
    pt = jnp.where(mask_t, jnp.exp(st - jnp.max(st, axis=0, keepdims=True)), 0.0)
    pt = pt / jnp.maximum(jnp.sum(pt, axis=0, keepdims=True), 1e-30)
    o_cmp = _dot_tn(cv[:, 0:NSA_D], pt.astype(BF16))

    imp = pt[:, 0:tq]
    for r in range(1, NSA_R):
        imp = imp + pt[:, r * tq:(r + 1) * tq]
    blk = lax.broadcasted_iota(jnp.int32, (nb, tq), 0)
    cur = (q0 + lax.broadcasted_iota(jnp.int32, (nb, tq), 1)) >> SEL_SHIFT
    forced = (blk == 0) | (blk == cur) | (blk == cur - 1)
    valid = blk <= cur
    score_ref[...] = jnp.where(forced, SEL_FORCE, jnp.where(valid, imp, -1.0))
    _rank_blocks(score_ref, rank_ref, (q0 + tq - 1) >> SEL_SHIFT)
    score_ref[...] = jnp.where((rank_ref[...] < n_sel) & valid, 0.0, NEG_INF)

    key_row = lax.broadcasted_iota(jnp.int32, (kt, tq), 0)
    t_lane = q0 + lax.broadcasted_iota(jnp.int32, (kt, tq), 1)
    ones = jnp.ones((ONES_ROWS, kt), BF16)

    init = (tuple(jnp.full((1, tq), NEG_INF, F32) for _ in range(NSA_R)),
            jnp.zeros((NSA_D + ONES_ROWS, rows), F32))

    hi = (q0 + tq + kt - 1) // kt
    n_tiles = sk_ref.shape[0] // kt

    def tile_loop(lo_tile, trips, k_ref, vt_ref, make_bias, unroll, carry):
        def body(i, carry):
            staged = []
            for u in range(unroll):
                j = lo_tile + i * unroll + u
                jc = jnp.minimum(j, n_tiles - 1)
                k0 = pl.multiple_of(jc * kt, kt)
                k_pos = jnp.where(j < hi, k0, 2 * n_tiles * kt) + key_row
                staged.append((_dot_nt(k_ref[pl.ds(k0, kt), :], q_rot), make_bias(k0, k_pos), jc))
            for s_t, bias, jc in staged:
                vt = jnp.concatenate([vt_ref[jc], ones], axis=0)
                carry = _softmax_tile(carry, s_t, vt, bias, tq)
            return carry
        return lax.fori_loop(0, trips, body, carry)

    def sel_bias(k0, k_pos):
        blk0 = k0 // SEL_BLK
        picked = jnp.concatenate(
            [jnp.broadcast_to(score_ref[pl.ds(blk0 + i, 1), :], (SEL_BLK, tq)) for i in range(kt // SEL_BLK)],
            axis=0)
        return jnp.where(k_pos <= t_lane, picked, NEG_INF)

    full = (hi + 1) // unroll
    carry = tile_loop(0, full, sk_ref, svt_ref, sel_bias, unroll, init)
    acc_s = tile_loop(full * unroll, (hi - full * unroll + 1) // 2, sk_ref, svt_ref, sel_bias, 2, carry)[1]

    def win_bias(k0, k_pos):
        diff = t_lane - k_pos
        return jnp.where((diff >= 0) & (diff <= WINDOW), 0.0, NEG_INF)

    acc_w = tile_loop(jnp.maximum(q0 - WINDOW, 0) // kt, 1, wk_ref, wvt_ref, win_bias, WINDOW // kt + 1, init)[1]

    gt = gates_ref[...].T
    o_s = acc_s[0:NSA_D, :] / jnp.maximum(acc_s[NSA_D:NSA_D + 1, :], 1e-30)
    o_w = acc_w[0:NSA_D, :] / jnp.maximum(acc_w[NSA_D:NSA_D + 1, :], 1e-30)
    heads = []
    for r in range(NSA_R):
        sl = slice(r * tq, (r + 1) * tq)
        heads.append(gt[r:r + 1, :] * o_cmp[:, sl] + gt[NSA_R + r:NSA_R + r + 1, :] * o_s[:, sl]
                     + gt[2 * NSA_R + r:2 * NSA_R + r + 1, :] * o_w[:, sl])
    for pair in range(NSA_R // 2):
        both = jnp.concatenate(heads[2 * pair:2 * pair + 2], axis=0)
        out_ref[:, pair * LANES:(pair + 1) * LANES] = both.T.astype(BF16)


def _nsa_attn_prompt(qraw, qrot, ckv, skd, svt, wkd, wvt, gates, *, batch, seq, tq, kt):
    assert tq == LANES and kt % tq == 0 and WINDOW % kt == 0 and seq % kt == 0
    nqt = seq // tq
    nb = seq // CMP_BLK
    gc = NSA_R * NSA_D
    q_map = lambda b, g, t: (b * nqt + t, g)
    kv_map = lambda b, g, t: (b, g)
    vt_map = lambda b, g, t: (b, g, 0)
    return pl.pallas_call(
        functools.partial(_nsa_attn_kernel, tq=tq, kt=kt, n_sel=min(N_SEL, seq // SEL_BLK),
                          unroll=ATTN_UNROLL),
        grid=(batch, NSA_G, nqt),
        in_specs=[pl.BlockSpec((tq, gc), q_map), pl.BlockSpec((tq, gc), q_map),
                  pl.BlockSpec((None, nb, LANES), lambda b, g, t: (0, b, g)),
                  pl.BlockSpec((None, nb, LANES), lambda b, g, t: (1, b, g)),
                  pl.BlockSpec((seq, LANES), kv_map), pl.BlockSpec((seq // kt, NSA_D, kt), vt_map),
                  pl.BlockSpec((seq, LANES), kv_map), pl.BlockSpec((seq // kt, NSA_D, kt), vt_map),
                  pl.BlockSpec((tq, LANES), q_map)],
        out_specs=pl.BlockSpec((tq, gc), q_map),
        out_shape=jax.ShapeDtypeStruct((batch * seq, NSA_H * NSA_D), BF16),
        scratch_shapes=[pltpu.VMEM((nb, tq), F32), pltpu.VMEM((nb, tq), F32)],
        compiler_params=_params(("arbitrary", "arbitrary", "arbitrary")),
        name="nsa_attn_prompt",
    )(qraw, qrot, ckv, ckv, skd, svt, wkd, wvt, gates)


def _nsa_sample_select_kernel(qe_ref, qo_ref, ck_ref, cv_ref, ocmp_ref, idx_ref):
    npg = ck_ref.shape[0] // NSA_G
    nbp = 2 * npg
    qe, qo = qe_ref[...], qo_ref[...]
    head = lax.broadcasted_iota(jnp.int32, (NSA_H, LANES), 0)

    def block_of(j):
        return jnp.where(j < npg, 2 * j, 2 * (j - npg) + 1)

    n_row = block_of(lax.broadcasted_iota(jnp.int32, (1, nbp), 1))
    n_col = block_of(lax.broadcasted_iota(jnp.int32, (nbp, 1), 0))
    forced = (n_row == 0) | (n_row == nbp - 1)
    diag = lax.broadcasted_iota(jnp.int32, (nbp, nbp), 0) == lax.broadcasted_iota(jnp.int32, (nbp, nbp), 1)
    k_idx = lax.broadcasted_iota(jnp.int32, (N_SEL, nbp), 0).astype(F32)
    n_f = jnp.broadcast_to(n_row, (N_SEL, nbp)).astype(F32)
    ocmp = jnp.zeros((NSA_H, LANES), F32)
    for g in range(NSA_G):
        ck = ck_ref[pl.ds(g, npg, stride=NSA_G), :].astype(BF16)
        cv = cv_ref[pl.ds(g, npg, stride=NSA_G), :].astype(BF16)
        s = jnp.concatenate([_dot_nt(qe, ck), _dot_nt(qo, ck)], axis=1)
        p = jnp.exp(s - jnp.max(s, axis=-1, keepdims=True))
        p = p / jnp.maximum(jnp.sum(p, axis=-1, keepdims=True), 1e-30)
        o_e = _dot(p[:, :npg].astype(BF16), cv)
        o_o = _dot(p[:, npg:].astype(BF16), cv)
        ocmp = jnp.where((head >> R_SHIFT) == g, o_e + pltpu.roll(o_o, HALF, 1), ocmp)

        imp = p[NSA_R * g:NSA_R * g + 1, :]
        for r in range(1, NSA_R):
            imp = imp + p[NSA_R * g + r:NSA_R * g + r + 1, :]
        row = jnp.where(forced, SEL_FORCE, imp)
        col = jnp.sum(jnp.where(diag, row, 0.0), axis=1, keepdims=True)
        beats = (col > row) | ((col == row) & (n_col < n_row))
        rank = jnp.sum(jnp.where(beats, 1.0, 0.0), axis=0, keepdims=True)
        onehot = jnp.where(rank == k_idx, n_f, 0.0)
        idx_ref[g] = jnp.sum(onehot, axis=1, keepdims=True).astype(jnp.int32)
    ocmp_ref[...] = ocmp


def _nsa_sample_select(q_e, q_o, ckv, *, bsz):
    rows = ckv.shape[1] // bsz
    return pl.pallas_call(
        _nsa_sample_select_kernel,
        grid=(bsz,),
        in_specs=[pl.BlockSpec((None, NSA_H, LANES), lambda b: (b, 0, 0)),
                  pl.BlockSpec((None, NSA_H, LANES), lambda b: (b, 0, 0)),
                  pl.BlockSpec((None, rows, LANES), lambda b: (0, b, 0)),
                  pl.BlockSpec((None, rows, LANES), lambda b: (1, b, 0))],
        out_specs=(pl.BlockSpec((None, NSA_H, LANES), lambda b: (b, 0, 0)),
                   pl.BlockSpec((None, NSA_G, N_SEL, 1), lambda b: (b, 0, 0, 0))),
        out_shape=(jax.ShapeDtypeStruct((bsz, NSA_H, LANES), F32),
                   jax.ShapeDtypeStruct((bsz, NSA_G, N_SEL, 1), jnp.int32)),
        compiler_params=_params(("arbitrary",)),
        name="nsa_sample_select",
    )(q_e, q_o, ckv, ckv)


def _nsa_sample_attn_kernel(idx_ref, pt_ref, *refs, n_past):
    del pt_ref
    slabs = refs[:NSA_G * n_past]
    (q64_ref, q_ref, snew_ref, wcache_ref, wnew_ref, gates_ref, ocmp_ref,
     out_ref, wout_ref, wbuf_ref) = refs[NSA_G * n_past:]
    kc = NSA_G * NSA_D
    b = pl.program_id(0)
    q64 = q64_ref[...]
    q = q_ref[...]
    head_row = lax.broadcasted_iota(jnp.int32, (NSA_H, kc), 0)
    head_lane = lax.broadcasted_iota(jnp.int32, (NSA_H, kc), 1)
    own = (head_lane >> SEL_SHIFT) == (head_row >> R_SHIFT)
    fold = jnp.where((lax.broadcasted_iota(jnp.int32, (kc, NSA_D), 0) & (NSA_D - 1))
                     == lax.broadcasted_iota(jnp.int32, (kc, NSA_D), 1), 1.0, 0.0)
    hi = lax.Precision.HIGHEST

    def rounded(x):
        return x.astype(BF16).astype(F32)

    snew = snew_ref[...]
    s_new = jnp.sum(q.astype(F32) * rounded(snew[:, :kc]), axis=-1, keepdims=True)
    v_new = jnp.dot(jnp.where(own, rounded(snew[:, kc:]), 0.0), fold, precision=hi,
                    preferred_element_type=F32)
    half_of_lane = lax.broadcasted_iota(jnp.int32, (1, PAGE), 1) >> SEL_SHIFT
    row64 = lax.broadcasted_iota(jnp.int32, (NSA_H, NSA_D), 0)
    f_sel = jnp.zeros((NSA_H, NSA_D), F32)
    for g in range(NSA_G):
        mine = slabs[g * n_past:(g + 1) * n_past]
        k_cat = jnp.concatenate([r[0] for r in mine], axis=1).astype(BF16)
        v_cat = jnp.concatenate([r[1] for r in mine], axis=1).astype(BF16)
        bias = jnp.concatenate(
            [jnp.where(half_of_lane == (idx_ref[b, g * n_past + k] & 1), 0.0, NEG_INF)
             for k in range(n_past)], axis=1)
        sb = _dot(q64, k_cat) + bias
        m = jnp.maximum(jnp.max(sb, axis=-1, keepdims=True), s_new)
        p = jnp.exp2(sb - m)
        p_new = jnp.exp2(s_new - m)
        l = jnp.sum(p, axis=-1, keepdims=True) + p_new
        o = _dot_nt(p.astype(BF16), v_cat) + rounded(p_new) * v_new
        f_sel = jnp.where((row64 >> R_SHIFT) == g, o / jnp.maximum(l, 1e-30), f_sel)

    wb = wcache_ref.shape[0]
    wbuf_ref[0:wb, :] = wcache_ref[...]
    wbuf_ref[wb:wb + 1, :] = wnew_ref[...]
    wbuf_ref[wb + 1:, :] = jnp.zeros((wbuf_ref.shape[0] - wb - 1, 2 * kc), F32)
    s = _dot_nt(q, wbuf_ref[:, :kc].astype(BF16))
    ok = lax.broadcasted_iota(jnp.int32, s.shape, 1) < wb + 1
    s = jnp.where(ok, s, NEG_INF)
    p = jnp.where(ok, jnp.exp2(s - jnp.max(s, axis=-1, keepdims=True)), 0.0)
    p = p / jnp.maximum(jnp.sum(p, axis=-1, keepdims=True), 1e-30)
    o_win = _dot(p.astype(BF16), wbuf_ref[:, kc:].astype(BF16))
    f_win = jnp.dot(jnp.where(own, o_win, 0.0), fold, precision=hi, preferred_element_type=F32)
    wout_ref[...] = wbuf_ref[1:wb + 1, :]

    gts = gates_ref[...]
    out_ref[...] = (gts[:, 0:1] * ocmp_ref[:, 0:NSA_D] + gts[:, 1:2] * f_sel + gts[:, 2:3] * f_win)


def _nsa_sample_attn(idx, page_table, slc_cache, q64, q_cmp, slc_new, win_cache, win_new, gates, ocmp):
    bsz = page_table.shape[0]
    n_past = idx.shape[1] // NSA_G
    kc = NSA_G * NSA_D
    wb = win_cache.shape[1]
    assert PAGE == 2 * SEL_BLK

    def slab_map(j):
        def f(b, idx_ref, pt_ref):
            return (pt_ref[b, idx_ref[b, j] // 2], 0, j // n_past, 0, 0)
        return f

    pad = lambda n: -(-n // LANES) * LANES
    in_specs = [pl.BlockSpec((None, 2, None, NSA_D, PAGE), slab_map(j)) for j in range(NSA_G * n_past)]
    in_specs += [pl.BlockSpec((None, NSA_H, NSA_D), lambda b, i, p: (b, 0, 0)),
                 pl.BlockSpec((None, NSA_H, kc), lambda b, i, p: (b, 0, 0)),
                 pl.BlockSpec((None, 1, 2 * kc), lambda b, i, p: (b, 0, 0)),
                 pl.BlockSpec((None, wb, 2 * kc), lambda b, i, p: (b, 0, 0)),
                 pl.BlockSpec((None, 1, 2 * kc), lambda b, i, p: (b, 0, 0)),
                 pl.BlockSpec((None, NSA_H, 3), lambda b, i, p: (b, 0, 0)),
                 pl.BlockSpec((None, NSA_H, LANES), lambda b, i, p: (b, 0, 0))]
    grid_spec = pltpu.PrefetchScalarGridSpec(
        num_scalar_prefetch=2,
        grid=(bsz,),
        in_specs=in_specs,
        out_specs=(pl.BlockSpec((None, NSA_H, NSA_D), lambda b, i, p: (b, 0, 0)),
                   pl.BlockSpec((None, wb, 2 * kc), lambda b, i, p: (b, 0, 0))),
        scratch_shapes=[pltpu.VMEM((pad(wb + 1), 2 * kc), F32)],
    )
    return pl.pallas_call(
        functools.partial(_nsa_sample_attn_kernel, n_past=n_past),
        grid_spec=grid_spec,
        out_shape=(jax.ShapeDtypeStruct((bsz, NSA_H, NSA_D), F32),
                   jax.ShapeDtypeStruct((bsz, wb, 2 * kc), F32)),
        compiler_params=_params(("arbitrary",)),
        name="nsa_sample_attn",
    )(idx, page_table, *([slc_cache] * (NSA_G * n_past)), q64, q_cmp, slc_new, win_cache, win_new, gates, ocmp)


def _proj_ln_kernel(a_ref, w_ref, x_ref, g_ref, b_ref, out_ref, *, alpha):
    y = _dot(a_ref[...].astype(BF16), w_ref[...])
    out_ref[...] = _layer_norm(alpha * x_ref[...] + y, g_ref[...], b_ref[...])


def _gla_proj_ln_kernel(o_ref, rs_ref, ng_ref, w_ref, x_ref, g_ref, b_ref, out_ref, *, alpha):
    a = (o_ref[...] * ng_ref[...] * rs_ref[...]).astype(BF16)
    y = _dot(a, w_ref[...])
    out_ref[...] = _layer_norm(alpha * x_ref[...] + y, g_ref[...], b_ref[...])


def _proj_ln(a, w, x, g, b, *, tm, alpha):
    n, d = x.shape
    row = lambda i: (i, 0)
    const = lambda i: (0, 0)
    return pl.pallas_call(
        functools.partial(_proj_ln_kernel, alpha=alpha),
        grid=(n // tm,),
        in_specs=[pl.BlockSpec((tm, a.shape[1]), row), pl.BlockSpec(w.shape, const),
                  pl.BlockSpec((tm, d), row), pl.BlockSpec((1, d), const), pl.BlockSpec((1, d), const)],
        out_specs=pl.BlockSpec((tm, d), row),
        out_shape=jax.ShapeDtypeStruct((n, d), F32),
        compiler_params=_params(("arbitrary",)),
        name="proj_ln",
    )(a, w, x, g, b)


def _gla_proj_ln(o, rs, ng, w, x, g, b, *, tm, alpha):
    n, d = x.shape
    row = lambda i: (i, 0)
    const = lambda i: (0, 0)
    return pl.pallas_call(
        functools.partial(_gla_proj_ln_kernel, alpha=alpha),
        grid=(n // tm,),
        in_specs=[pl.BlockSpec((tm, o.shape[1]), row), pl.BlockSpec((tm, o.shape[1]), row),
                  pl.BlockSpec((1, o.shape[1]), const), pl.BlockSpec(w.shape, const),
                  pl.BlockSpec((tm, d), row), pl.BlockSpec((1, d), const), pl.BlockSpec((1, d), const)],
        out_specs=pl.BlockSpec((tm, d), row),
        out_shape=jax.ShapeDtypeStruct((n, d), F32),
        compiler_params=_params(("arbitrary",)),
        name="gla_proj_ln",
    )(o, rs, ng, w, x, g, b)


def _ffn_kernel(x_ref, wg_ref, wu_ref, wd_ref, g_ref, b_ref, out_ref, acc_ref, xb_ref, *, alpha):
    k = pl.program_id(1)

    @pl.when(k == 0)
    def _():
        xb_ref[...] = x_ref[...].astype(BF16)
        acc_ref[...] = jnp.zeros_like(acc_ref)

    xb = xb_ref[...]
    h = _silu(_dot(xb, wg_ref[...])) * _dot(xb, wu_ref[...])
    acc_ref[...] += _dot(h.astype(BF16), wd_ref[...])

    @pl.when(k == pl.num_programs(1) - 1)
    def _():
        out_ref[...] = _layer_norm(alpha * x_ref[...] + acc_ref[...], g_ref[...], b_ref[...])


def _ffn(x, w_gu, w_down, g, b, *, layer, tm, alpha):
    n, d = x.shape
    d_ff = w_down.shape[1]
    splits = 2
    tf = d_ff // splits
    assert tf * splits == d_ff and tf % LANES == 0
    return pl.pallas_call(
        functools.partial(_ffn_kernel, alpha=alpha),
        grid=(n // tm, splits),
        in_specs=[pl.BlockSpec((tm, d), lambda i, k: (i, 0)),
                  pl.BlockSpec((None, d, tf), lambda i, k: (layer, 0, k)),
                  pl.BlockSpec((None, d, tf), lambda i, k: (layer, 0, splits + k)),
                  pl.BlockSpec((None, tf, d), lambda i, k: (layer, k, 0)),
                  pl.BlockSpec((1, d), lambda i, k: (0, 0)),
                  pl.BlockSpec((1, d), lambda i, k: (0, 0))],
        out_specs=pl.BlockSpec((tm, d), lambda i, k: (i, 0)),
        out_shape=jax.ShapeDtypeStruct((n, d), F32),
        scratch_shapes=[pltpu.VMEM((tm, d), F32), pltpu.VMEM((tm, d), BF16)],
        compiler_params=_params(("arbitrary", "arbitrary")),
        name="ffn",
    )(x, w_gu, w_gu, w_down, g, b)


def _gla_log_decay(low, w2_ref, b2_ref):
    z = _dot(low.astype(BF16), w2_ref[...]) + b2_ref[...]
    return jax.nn.log_sigmoid(z) / GLA_TAU


def _gla_proj_prompt_kernel(x_ref, w_ref, w2_ref, b2_ref, qi_ref, ki_ref, ks_ref, v_ref, rs_ref,
                            al_ref, *, dk, dv, scale):
    tm = x_ref.shape[0]
    xb = x_ref[...].astype(BF16)
    q = _dot(xb, w_ref[:, 0:dk])
    k = _dot(xb, w_ref[:, dk:2 * dk])
    v_ref[...] = _dot(xb, w_ref[:, 2 * dk:2 * dk + dv]).astype(BF16)
    rs_ref[...] = _silu(_dot(xb, w_ref[:, 2 * dk + dv:2 * dk + 2 * dv]))
    low = _dot(xb, w_ref[:, 2 * dk + 2 * dv:])
    la = _gla_log_decay(low, w2_ref, b2_ref)

    rin = lax.broadcasted_iota(jnp.int32, (tm, dk), 0) & (GLA_CHUNK - 1)
    lb = la
    s = 1
    while s < GLA_CHUNK:
        lb = lb + jnp.where(rin >= s, pltpu.roll(lb, s, 0), 0.0)
        s *= 2
    lb3 = lb.reshape(tm // GLA_CHUNK, GLA_CHUNK, dk)
    last = lb3[:, GLA_CHUNK - 1:GLA_CHUNK, :]
    qi_ref[...] = (q * jnp.exp(lb) * scale).astype(BF16)
    ki_ref[...] = (k * jnp.exp(-lb)).astype(BF16)
    ks_ref[...] = (k * jnp.exp(last - lb3).reshape(tm, dk)).astype(BF16)
    al_ref[...] = jnp.exp(last.reshape(tm // GLA_CHUNK, dk))


def _gla_proj_prompt(x, w, w2, b2, *, tm, dk, dv):
    n, d = x.shape
    row = lambda i: (i, 0)
    const = lambda i: (0, 0)
    return pl.pallas_call(
        functools.partial(_gla_proj_prompt_kernel, dk=dk, dv=dv, scale=(dk // GLA_H) ** -0.5),
        grid=(n // tm,),
        in_specs=[pl.BlockSpec((tm, d), row), pl.BlockSpec(w.shape, const),
                  pl.BlockSpec(w2.shape, const), pl.BlockSpec(b2.shape, const)],
        out_specs=(pl.BlockSpec((tm, dk), row), pl.BlockSpec((tm, dk), row), pl.BlockSpec((tm, dk), row),
                   pl.BlockSpec((tm, dv), row), pl.BlockSpec((tm, dv), row),
                   pl.BlockSpec((tm // GLA_CHUNK, dk), row)),
        out_shape=(jax.ShapeDtypeStruct((n, dk), BF16), jax.ShapeDtypeStruct((n, dk), BF16),
                   jax.ShapeDtypeStruct((n, dk), BF16), jax.ShapeDtypeStruct((n, dv), BF16),
                   jax.ShapeDtypeStruct((n, dv), F32),
                   jax.ShapeDtypeStruct((n // GLA_CHUNK, dk), F32)),
        compiler_params=_params(("arbitrary",)),
        name="gla_proj_prompt",
    )(x, w, w2, b2)


def _gla_proj_sample_kernel(x_ref, w_ref, w2_ref, b2_ref, q_ref, k_ref, v_ref, rs_ref, la_ref, *, dk, dv):
    xb = x_ref[...].astype(BF16)
    q_ref[...] = _dot(xb, w_ref[:, 0:dk])
    k_ref[...] = _dot(xb, w_ref[:, dk:2 * dk])
    v_ref[...] = _dot(xb, w_ref[:, 2 * dk:2 * dk + dv])
    rs_ref[...] = _silu(_dot(xb, w_ref[:, 2 * dk + dv:2 * dk + 2 * dv]))
    low = _dot(xb, w_ref[:, 2 * dk + 2 * dv:])
    la_ref[...] = _gla_log_decay(low, w2_ref, b2_ref)


def _gla_proj_sample(x, w, w2, b2, *, dk, dv):
    n, d = x.shape
    full = lambda s: pl.BlockSpec(s, lambda i: (0,) * len(s))
    return pl.pallas_call(
        functools.partial(_gla_proj_sample_kernel, dk=dk, dv=dv),
        grid=(1,),
        in_specs=[full((n, d)), full(w.shape), full(w2.shape), full(b2.shape)],
        out_specs=(full((n, dk)), full((n, dk)), full((n, dv)), full((n, dv)), full((n, dk))),
        out_shape=(jax.ShapeDtypeStruct((n, dk), F32), jax.ShapeDtypeStruct((n, dk), F32),
                   jax.ShapeDtypeStruct((n, dv), F32), jax.ShapeDtypeStruct((n, dv), F32),
                   jax.ShapeDtypeStruct((n, dk), F32)),
        compiler_params=_params(("arbitrary",)),
        name="gla_proj_sample",
    )(x, w, w2, b2)


def _gla_rec_kernel(qi_ref, ki_ref, ks_ref, v_ref, al_ref, o_ref, st_out_ref, st_ref):
    t = pl.program_id(2)

    @pl.when(t == 0)
    def _():
        st_ref[...] = jnp.zeros_like(st_ref)

    c = GLA_CHUNK
    ct = qi_ref.shape[0]
    n = ct // c
    qi, ki, vv = qi_ref[...], ki_ref[...], v_ref[...]
    row = lax.broadcasted_iota(jnp.int32, (ct, ct), 0)
    col = lax.broadcasted_iota(jnp.int32, (ct, ct), 1)
    keep = (row >= col) & ((row >> GLA_CHUNK_SHIFT) == (col >> GLA_CHUNK_SHIFT))
    att = jnp.where(keep, _dot_nt(qi, ki), 0.0)
    o_intra = _dot(att.astype(BF16), vv)
    incs = [_dot_tn(vv[i * c:(i + 1) * c], ks_ref[i * c:(i + 1) * c, :]) for i in range(n)]
    st = st_ref[...]
    entering = []
    for i in range(n):
        entering.append(st.astype(BF16))
        st = al_ref[i:i + 1, :] * st + incs[i]
    st_ref[...] = st
    o = o_intra + jnp.concatenate(
        [_dot_nt(qi[i * c:(i + 1) * c], entering[i]) for i in range(n)], axis=0)
    o_ref[...] = o * lax.rsqrt(jnp.mean(o * o, axis=-1, keepdims=True) + LN_EPS)

    @pl.when(t == pl.num_programs(2) - 1)
    def _():
        st_out_ref[...] = st


def _gla_rec(qi, ki, ks, v, al, *, batch, seq, dkh, dvh):
    ct = 256
    nct = seq // ct
    cpt = ct // GLA_CHUNK
    tok = lambda b, h, t: (b * nct + t, h)
    return pl.pallas_call(
        _gla_rec_kernel,
        grid=(batch, GLA_H, nct),
        in_specs=[pl.BlockSpec((ct, dkh), tok), pl.BlockSpec((ct, dkh), tok), pl.BlockSpec((ct, dkh), tok),
                  pl.BlockSpec((ct, dvh), tok), pl.BlockSpec((cpt, dkh), tok)],
        out_specs=(pl.BlockSpec((ct, dvh), tok),
                   pl.BlockSpec((None, None, dvh, dkh), lambda b, h, t: (b, h, 0, 0))),
        out_shape=(jax.ShapeDtypeStruct((batch * seq, GLA_H * dvh), F32),
                   jax.ShapeDtypeStruct((batch, GLA_H, dvh, dkh), F32)),
        scratch_shapes=[pltpu.VMEM((dvh, dkh), F32)],
        compiler_params=_params(("arbitrary", "arbitrary", "arbitrary")),
        name="gla_rec",
    )(qi, ki, ks, v, al)


def _gla_step_kernel(q_ref, k_ref, la_ref, v_ref, s0_ref, o_ref, s_ref, *, scale):
    for h in range(s0_ref.shape[0]):
        la = la_ref[h]
        a = jnp.exp(la)
        k = k_ref[h]
        qi = q_ref[h] * a * scale
        ki = k * jnp.exp(-la)
        v = v_ref[h]
        s0 = s0_ref[h]
        att = jnp.sum(qi * ki, axis=0, keepdims=True)
        o = att * v + jnp.sum(qi * s0, axis=0, keepdims=True)
        s_ref[h] = a * s0 + k * v
        o_ref[h] = o * lax.rsqrt(jnp.mean(o * o, axis=-1, keepdims=True) + LN_EPS)


def _gla_step(q, k, la, v, s0, *, scale):
    bsz, h, dkh, dvh = s0.shape
    col = pl.BlockSpec((None, h, dkh, 1), lambda b: (b, 0, 0, 0))
    rowv = pl.BlockSpec((None, h, 1, dvh), lambda b: (b, 0, 0, 0))
    mat = pl.BlockSpec((None, h, dkh, dvh), lambda b: (b, 0, 0, 0))
    return pl.pallas_call(
        functools.partial(_gla_step_kernel, scale=scale),
        grid=(bsz,),
        in_specs=[col, col, col, rowv, mat],
        out_specs=(rowv, mat),
        out_shape=(jax.ShapeDtypeStruct((bsz, h, 1, dvh), F32),
                   jax.ShapeDtypeStruct((bsz, h, dkh, dvh), F32)),
        compiler_params=_params(("arbitrary",)),
        name="gla_step",
    )(q, k, la, v, s0)


def _rope_tables(pos):
    half = NSA_D // 2
    inv = ROPE_THETA ** (-jnp.arange(half, dtype=F32) / half)
    ang = pos.astype(F32)[:, None] * inv[None, :]
    cos, sin = jnp.cos(ang), jnp.sin(ang)
    zero = jnp.zeros_like(sin)
    cos_t = jnp.tile(cos, (1, LANES // half))
    sin_lo = jnp.tile(jnp.concatenate([-sin, zero], axis=1), (1, LANES // NSA_D))
    sin_hi = jnp.tile(jnp.concatenate([zero, sin], axis=1), (1, LANES // NSA_D))
    return cos_t, sin_lo, sin_hi


def _prep_nsa_weights(w_in, b_gate, pe, w1, w2):
    d = w_in.shape[0]
    qc = NSA_H * NSA_D
    kc = NSA_G * NSA_D
    gate_off = qc + 6 * kc
    wg = w_in[:, gate_off:].reshape(d, 3, NSA_G, NSA_R).transpose(0, 2, 1, 3).reshape(d, NSA_G, 3 * NSA_R)
    wg = jnp.pad(wg, ((0, 0), (0, 0), (0, LANES - 3 * NSA_R))).reshape(d, NSA_G * LANES)
    bg = b_gate.reshape(3, NSA_G, NSA_R).transpose(1, 0, 2).reshape(NSA_G, 3 * NSA_R)
    bg = jnp.pad(bg, ((0, 0), (0, LANES - 3 * NSA_R))).reshape(1, NSA_G * LANES)
    w = jnp.concatenate([w_in[:, :gate_off], wg], axis=1).astype(BF16)
    eye = jnp.eye(NSA_G, dtype=F32)
    pe_x = jnp.tile(pe.transpose(1, 0, 2), (1, 1, NSA_G))
    same_head = (jnp.arange(kc)[:, None] // NSA_D == jnp.arange(kc)[None, :] // NSA_D).astype(BF16)
    w1_bd = jnp.tile(w1.transpose(1, 0, 2, 3).astype(BF16), (1, 1, NSA_G, NSA_G)) * same_head
    w2_dup = jnp.concatenate([w2, w2], axis=-1)
    w2_bd = jnp.einsum('cef,gh->cgehf', w2_dup, eye).reshape(2, kc, 2 * kc).astype(BF16)
    return w, bg, pe_x, w1_bd, w2_bd


def _prep_vt_weights(w_in):
    qc = NSA_H * NSA_D
    kc = NSA_G * NSA_D
    out = []
    for branch in (1, 2):
        lo = qc + branch * 2 * kc + kc
        out.append(w_in[:, lo:lo + kc].T)
    return jnp.stack(out).astype(BF16)


def _prep_page_compress_weights(pe, w1, w2):
    nd = NSA_D // 2
    eye = jnp.eye(PAGE // CMP_BLK, dtype=F32)
    pe_n = jnp.broadcast_to(pe.transpose(1, 2, 0).reshape(2, nd, 2, 1, CMP_BLK),
                            (2, nd, 2, PAGE // CMP_BLK, CMP_BLK)).reshape(2, nd, 2 * PAGE)
    w1c = w1.transpose(1, 2, 0, 3).reshape(2, nd, 2, CMP_BLK, NSA_D).astype(BF16)
    halves = PAGE // CMP_BLK
    same_half = (jnp.arange(PAGE)[:, None] // CMP_BLK == jnp.arange(PAGE)[None, :] // NSA_D).astype(BF16)
    w1_n = jnp.concatenate([jnp.tile(w1c[:, :, p], (1, 1, halves, halves)) * same_half for p in range(2)],
                           axis=2)
    w2_n = jnp.einsum('cef,hk->chekf', w2, eye).reshape(2, PAGE, PAGE).astype(BF16)
    return pe_n, w1_n, w2_n


def kernel(x_prompt, x_sample, cache_cmp_kv, cache_slc_kv, cache_win_kv, state_gla, page_table,
           nsa_w_in, nsa_b_gate, nsa_pe_cmp, nsa_w_cmp1, nsa_w_cmp2, nsa_w_out,
           gla_w_in, gla_w_gate2, gla_b_gate2, gla_norm_g, gla_w_out,
           ffn_w_gu, ffn_w_down, ln_g, ln_b):
    batch, seq, d = x_prompt.shape
    bsz, dec_seq, _ = x_sample.shape
    depth = ffn_w_gu.shape[0]
    n_pool = cache_cmp_kv.shape[1]
    n_pages = page_table.shape[1]
    past_len = n_pages * PAGE
    kc = NSA_G * NSA_D
    assert dec_seq == 1 and depth == 2 and d == NSA_H * NSA_D
    assert seq % 512 == 0 and seq >= WINDOW and past_len % SEL_BLK == 0
    assert cache_win_kv.shape[2] == WINDOW and past_len // SEL_BLK >= N_SEL
    alpha = (2.0 * depth) ** 0.25
    tm = 512
    nbp = past_len // CMP_BLK

    xp = x_prompt.reshape(batch * seq, d)
    xs = x_sample.reshape(bsz, d)
    ln_g = ln_g.reshape(depth, 2, 1, d)
    ln_b = ln_b.reshape(depth, 2, 1, d)
    w_gu = ffn_w_gu.astype(BF16)
    w_dn = ffn_w_down.astype(BF16)

    w0, bg0, pe_x, w1_bd, w2_bd = _prep_nsa_weights(nsa_w_in[0], nsa_b_gate[0], nsa_pe_cmp[0],
                                                    nsa_w_cmp1[0], nsa_w_cmp2[0])
    w_out0 = nsa_w_out[0].astype(BF16)
    tabs_p = _rope_tables(jnp.arange(seq, dtype=jnp.int32))
    tabs_s = _rope_tables(jnp.full((bsz,), past_len, dtype=jnp.int32))

    seq_tiles = seq // tm
    (qraw, qrot, cmp_p, slc_p, win_p, gates_p, skd, wkd, svt, wvt) = _nsa_proj(
        xp, w0, bg0, *tabs_p, _prep_vt_weights(nsa_w_in[0]), tm=tm, seq_tiles=seq_tiles,
        win_rows=batch * WINDOW, win_map=lambda i: (i // seq_tiles, 0), attn_kt=ATTN_KT)
    ckv_p = _compress_prompt(cmp_p, pe_x, w1_bd, w2_bd, batch=batch, seq=seq)
    o_p = _nsa_attn_prompt(qraw, qrot, ckv_p, skd, svt, wkd, wvt, gates_p, batch=batch, seq=seq,
                           tq=ATTN_TQ, kt=ATTN_KT)
    yp = _proj_ln(o_p, w_out0, xp, ln_g[0, 0], ln_b[0, 0], tm=tm, alpha=alpha)

    (qraw_s, qrot_s, cmp_s, slc_s, win_s, gates_s) = _nsa_proj(
        xs, w0, bg0, *tabs_s, tm=bsz, seq_tiles=1, win_rows=bsz, win_map=lambda i: (i, 0))
    pages_fm = lambda c: jnp.transpose(c[0], (0, 2, 3, 4, 1))
    pe_n, w1_n, w2_n = _prep_page_compress_weights(nsa_pe_cmp[0], nsa_w_cmp1[0], nsa_w_cmp2[0])
    ckv_s = _compress_pages(pages_fm(cache_cmp_kv).reshape(n_pool, 2, kc, PAGE), page_table, pe_n, w1_n, w2_n)
    q16 = qraw_s.reshape(bsz, NSA_H, NSA_D)
    zeros16 = jnp.zeros_like(q16)
    q_e = jnp.concatenate([q16, zeros16], axis=-1)
    q_o = jnp.concatenate([zeros16, q16], axis=-1)
    ocmp_s, idx4 = _nsa_sample_select(q_e, q_o, ckv_s, bsz=bsz)
    idx = idx4[:, :, :N_SEL - 1, 0].reshape(bsz, NSA_G * (N_SEL - 1))
    q4r = qrot_s.reshape(bsz, NSA_G, NSA_R, NSA_D)
    q_cmp = jnp.einsum('bgrd,gh->bgrhd', q4r, jnp.eye(NSA_G, dtype=BF16)).reshape(bsz, NSA_H, kc)
    gates_s3 = gates_s.reshape(bsz, NSA_G, LANES)[:, :, :3 * NSA_R].reshape(bsz, NSA_G, 3, NSA_R)
    gates_s3 = gates_s3.transpose(0, 1, 3, 2).reshape(bsz, NSA_H, 3)
    o_s, win_buf_s = _nsa_sample_attn(
        idx, page_table, pages_fm(cache_slc_kv), qrot_s.reshape(bsz, NSA_H, NSA_D), q_cmp,
        slc_s.reshape(bsz, 1, 2 * kc), cache_win_kv[0].reshape(bsz, WINDOW, 2 * kc),
        win_s.reshape(bsz, 1, 2 * kc), gates_s3, ocmp_s)
    ys = _proj_ln(o_s.reshape(bsz, d), w_out0, xs, ln_g[0, 0], ln_b[0, 0], tm=bsz, alpha=alpha)

    yp = _ffn(yp, w_gu, w_dn, ln_g[0, 1], ln_b[0, 1], layer=0, tm=tm, alpha=alpha)
    ys = _ffn(ys, w_gu, w_dn, ln_g[0, 1], ln_b[0, 1], layer=0, tm=bsz, alpha=alpha)

    dk = gla_w_gate2.shape[2]
    dv = gla_w_out.shape[1]
    dkh, dvh = dk // GLA_H, dv // GLA_H
    gw = jnp.pad(gla_w_in[0], ((0, 0), (0, LANES - GLA_RANK))).astype(BF16)
    gw2 = jnp.pad(gla_w_gate2[0], ((0, LANES - GLA_RANK), (0, 0))).astype(BF16)
    gb2 = gla_b_gate2[0].reshape(1, dk)
    g_out = gla_w_out[0].astype(BF16)
    ng = gla_norm_g[0].reshape(1, dv)

    qi, ki, ks, vb, rs_p, al = _gla_proj_prompt(yp, gw, gw2, gb2, tm=tm, dk=dk, dv=dv)
    on_p, st_p = _gla_rec(qi, ki, ks, vb, al, batch=batch, seq=seq, dkh=dkh, dvh=dvh)
    yp = _gla_proj_ln(on_p, rs_p, ng, g_out, yp, ln_g[1, 0], ln_b[1, 0], tm=tm, alpha=alpha)

    q_s, k_s, v_s, rs_s, la_s = _gla_proj_sample(ys, gw, gw2, gb2, dk=dk, dv=dv)
    colv = lambda a: a.reshape(bsz, GLA_H, dkh, 1)
    on_s, st_s = _gla_step(colv(q_s), colv(k_s), colv(la_s), v_s.reshape(bsz, GLA_H, 1, dvh),
                           state_gla[0], scale=dkh ** -0.5)
    ys = _gla_proj_ln(on_s.reshape(bsz, dv), rs_s, ng, g_out, ys, ln_g[1, 0], ln_b[1, 0], tm=bsz, alpha=alpha)

    yp = _ffn(yp, w_gu, w_dn, ln_g[1, 1], ln_b[1, 1], layer=1, tm=tm, alpha=alpha)
    ys = _ffn(ys, w_gu, w_dn, ln_g[1, 1], ln_b[1, 1], layer=1, tm=bsz, alpha=alpha)

    kv6 = lambda a, rows: a.reshape(1, rows[0], rows[1], 2, NSA_G, NSA_D)
    return (yp.reshape(batch, seq, d), ys.reshape(bsz, 1, d),
            kv6(cmp_p, (batch, seq)), kv6(slc_p, (batch, seq)), kv6(win_p, (batch, WINDOW)),
            jnp.swapaxes(st_p, 2, 3)[None],
            kv6(cmp_s, (bsz, 1)), kv6(slc_s, (bsz, 1)), kv6(win_buf_s, (bsz, WINDOW)),
            st_s[None])
```

```python
import functools

import jax
import jax.numpy as jnp
from jax import lax
from jax.experimental import pallas as pl
from jax.experimental.pallas import tpu as pltpu

PAGE = 128
NSA_H = 16
NSA_D = 64
NSA_G = 4
NSA_R = NSA_H // NSA_G
CMP_BLK = 64
SEL_BLK = 64
SEL_SHIFT = 6
R_SHIFT = 2
LANE_SHIFT = 7
N_SEL = 16
WINDOW = 512
ATTN_SCALE = NSA_D ** -0.5
LOG2E = 1.4426950408889634
ROPE_THETA = 10000.0
NEG_INF = -1e30
SEL_FORCE = 1e4

GLA_H = 4
GLA_RANK = 16
GLA_TAU = 16.0
GLA_CHUNK = 32
GLA_CHUNK_SHIFT = 5
LN_EPS = 1e-5

LANES = 128
HALF = LANES // 2
ONES_ROWS = 16
ATTN_TQ = 128
ATTN_KT = 256
ATTN_UNROLL = 4
VMEM_LIMIT = 56 * 1024 * 1024

F32 = jnp.float32
BF16 = jnp.bfloat16

NT_DIMS = (((1,), (1,)), ((), ()))
TN_DIMS = (((0,), (0,)), ((), ()))


def _params(sem):
    return pltpu.CompilerParams(dimension_semantics=sem, vmem_limit_bytes=VMEM_LIMIT)


def _dot(a, b):
    return jnp.dot(a, b, preferred_element_type=F32)


def _dot_nt(a, b):
    return lax.dot_general(a, b, NT_DIMS, preferred_element_type=F32)


def _dot_tn(a, b):
    return lax.dot_general(a, b, TN_DIMS, preferred_element_type=F32)


def _layer_norm(z, g, b):
    mu = jnp.mean(z, axis=-1, keepdims=True)
    zc = z - mu
    var = jnp.mean(zc * zc, axis=-1, keepdims=True)
    return zc * lax.rsqrt(var + LN_EPS) * g + b


def _silu(x):
    return x * jax.nn.sigmoid(x)


def _rope(x, cos_t, sin_lo, sin_hi):
    out = []
    for p in range(x.shape[1] // LANES):
        blk = x[:, p * LANES:(p + 1) * LANES]
        x_up = pltpu.roll(blk, LANES - NSA_D // 2, 1)
        x_dn = pltpu.roll(blk, NSA_D // 2, 1)
        out.append(blk * cos_t + x_up * sin_lo + x_dn * sin_hi)
    return out[0] if len(out) == 1 else jnp.concatenate(out, axis=1)


def _dup_heads(x):
    rows, c = x.shape
    lane = lax.broadcasted_iota(jnp.int32, (rows, LANES), 1)
    lo = lane < HALF
    out = []
    for p in range(c // LANES):
        blk = x[:, p * LANES:(p + 1) * LANES]
        sw = pltpu.roll(blk, HALF, 1)
        out.append(jnp.where(lo, blk, sw))
        out.append(jnp.where(lo, sw, blk))
    return jnp.concatenate(out, axis=1)


def _nsa_proj_kernel(x_ref, w_ref, bg_ref, cos_ref, slo_ref, shi_ref, *refs, attn_kt):
    qc = NSA_H * NSA_D
    kc = NSA_G * NSA_D
    if attn_kt is None:
        qraw_ref, qrot_ref, cmp_ref, slc_ref, win_ref, gates_ref = refs
    else:
        (wvt_ref, qraw_ref, qrot_ref, cmp_ref, slc_ref, win_ref, gates_ref,
         skd_ref, wkd_ref, svt_ref, wvt_out_ref) = refs
    xb = x_ref[...].astype(BF16)
    cos_t, slo, shi = cos_ref[...], slo_ref[...], shi_ref[...]

    q = _dot(xb, w_ref[:, 0:qc])
    qraw_ref[...] = (q * ATTN_SCALE).astype(BF16)
    qrot_ref[...] = (_rope(q, cos_t, slo, shi) * (ATTN_SCALE * LOG2E)).astype(BF16)

    cmp_ref[...] = _dot(xb, w_ref[:, qc:qc + 2 * kc])

    slc = _dot(xb, w_ref[:, qc + 2 * kc:qc + 4 * kc])
    sk = _rope(slc[:, :kc], cos_t, slo, shi)
    slc_ref[:, :kc] = sk
    slc_ref[:, kc:] = slc[:, kc:]

    win = _dot(xb, w_ref[:, qc + 4 * kc:qc + 6 * kc])
    wk = _rope(win[:, :kc], cos_t, slo, shi)
    win_ref[:, :kc] = wk
    win_ref[:, kc:] = win[:, kc:]

    gz = _dot(xb, w_ref[:, qc + 6 * kc:qc + 6 * kc + NSA_G * LANES]) + bg_ref[...]
    gates_ref[...] = jax.nn.sigmoid(gz)

    if attn_kt is not None:
        skd_ref[...] = _dup_heads(sk).astype(BF16)
        wkd_ref[...] = _dup_heads(wk).astype(BF16)
        for src, dst in ((0, svt_ref), (1, wvt_out_ref)):
            vt = _dot_nt(wvt_ref[src], xb).astype(BF16)
            for t in range(x_ref.shape[0] // attn_kt):
                dst[t] = vt[:, t * attn_kt:(t + 1) * attn_kt]


def _nsa_proj(x, w, bg, cos_t, slo, shi, wvt=None, *, tm, seq_tiles, win_rows, win_map, attn_kt=None):
    n, d = x.shape
    qc = NSA_H * NSA_D
    kc2 = 2 * NSA_G * NSA_D
    kd = NSA_G * LANES
    row = lambda i: (i, 0)
    tab = lambda i: (i % seq_tiles, 0)
    const = lambda i: (0, 0)
    in_specs = [pl.BlockSpec((tm, d), row), pl.BlockSpec(w.shape, const), pl.BlockSpec(bg.shape, const),
                pl.BlockSpec((tm, LANES), tab), pl.BlockSpec((tm, LANES), tab), pl.BlockSpec((tm, LANES), tab)]
    out_shape = [jax.ShapeDtypeStruct((n, qc), BF16), jax.ShapeDtypeStruct((n, qc), BF16),
                 jax.ShapeDtypeStruct((n, kc2), F32), jax.ShapeDtypeStruct((n, kc2), F32),
                 jax.ShapeDtypeStruct((win_rows, kc2), F32), jax.ShapeDtypeStruct((n, kd), F32)]
    out_specs = [pl.BlockSpec((tm, qc), row), pl.BlockSpec((tm, qc), row),
                 pl.BlockSpec((tm, kc2), row), pl.BlockSpec((tm, kc2), row),
                 pl.BlockSpec((tm, kc2), win_map), pl.BlockSpec((tm, kd), row)]
    args = [x, w, bg, cos_t, slo, shi]
    if attn_kt is not None:
        assert tm % attn_kt == 0
        in_specs.append(pl.BlockSpec(wvt.shape, lambda i: (0, 0, 0)))
        args.append(wvt)
        tiles = tm // attn_kt
        out_shape += [jax.ShapeDtypeStruct((n, kd), BF16), jax.ShapeDtypeStruct((n, kd), BF16),
                      jax.ShapeDtypeStruct((n // attn_kt, kc2 // 2, attn_kt), BF16),
                      jax.ShapeDtypeStruct((n // attn_kt, kc2 // 2, attn_kt), BF16)]
        out_specs += [pl.BlockSpec((tm, kd), row), pl.BlockSpec((tm, kd), row),
                      pl.BlockSpec((tiles, kc2 // 2, attn_kt), lambda i: (i, 0, 0)),
                      pl.BlockSpec((tiles, kc2 // 2, attn_kt), lambda i: (i, 0, 0))]
    return pl.pallas_call(
        functools.partial(_nsa_proj_kernel, attn_kt=attn_kt),
        grid=(n // tm,),
        in_specs=in_specs,
        out_specs=tuple(out_specs),
        out_shape=tuple(out_shape),
        compiler_params=_params(("arbitrary",)),
        name="nsa_proj",
    )(*args)


def _compress_body(tok_refs, pe_ref, w1_ref, w2_ref, out_ref):
    m = tok_refs[0].shape[0] // CMP_BLK
    acc = jnp.zeros((m, w1_ref.shape[2]), F32)
    for l in range(CMP_BLK):
        xl = jnp.concatenate([r[pl.ds(l, m, stride=CMP_BLK), :] for r in tok_refs], axis=1)
        xl = xl + pe_ref[l:l + 1, :]
        acc = acc + _dot(xl.astype(BF16), w1_ref[l])
    h = _silu(acc)
    out_ref[...] = _dot(h.astype(BF16), w2_ref[...])


def _compress_prompt_kernel(tok0_ref, tok1_ref, pe_ref, w1_ref, w2_ref, out_ref):
    _compress_body((tok0_ref, tok1_ref), pe_ref, w1_ref, w2_ref, out_ref)


def _compress_prompt(cmp_kv, pe, w1, w2, *, batch, seq):
    kc = NSA_G * NSA_D
    nb = seq // CMP_BLK
    return pl.pallas_call(
        _compress_prompt_kernel,
        grid=(2, batch),
        in_specs=[pl.BlockSpec((seq, LANES), lambda c, b: (b, 2 * c)),
                  pl.BlockSpec((seq, LANES), lambda c, b: (b, 2 * c + 1)),
                  pl.BlockSpec((None, CMP_BLK, kc), lambda c, b: (c, 0, 0)),
                  pl.BlockSpec((None, CMP_BLK, kc, kc), lambda c, b: (c, 0, 0, 0)),
                  pl.BlockSpec((None, kc, 2 * kc), lambda c, b: (c, 0, 0))],
        out_specs=pl.BlockSpec((None, nb, 2 * kc), lambda c, b: (c, b, 0)),
        out_shape=jax.ShapeDtypeStruct((2, batch * nb, 2 * kc), F32),
        compiler_params=_params(("arbitrary", "arbitrary")),
        name="compress_prompt",
    )(cmp_kv, cmp_kv, pe, w1, w2)


def _compress_pages_kernel(pt_ref, *refs, pages):
    del pt_ref
    page_refs = refs[:pages]
    pe_ref, w1_ref, w2_ref, out_ref, buf_ref = refs[pages:]
    rows_pp = NSA_G * NSA_D
    for k in range(pages):
        buf_ref[k * rows_pp:(k + 1) * rows_pp, :] = page_refs[k][...]
    m = pages * NSA_G
    acc = jnp.zeros((m, LANES), F32)
    for dd in range(NSA_D // 2):
        x = jnp.concatenate([buf_ref[pl.ds(2 * dd, m, stride=NSA_D), :],
                             buf_ref[pl.ds(2 * dd + 1, m, stride=NSA_D), :]], axis=1)
        x = x + pe_ref[dd:dd + 1, :]
        acc = acc + _dot(x.astype(BF16), w1_ref[dd])
    out_ref[...] = _dot(_silu(acc).astype(BF16), w2_ref[...])


def _compress_pages(cache, page_table, pe, w1, w2):
    bsz, n_pages = page_table.shape
    pages = min(64, n_pages)
    groups = n_pages // pages
    gd = NSA_G * NSA_D
    assert PAGE == 2 * CMP_BLK and PAGE == LANES

    def page_map(k):
        return lambda c, b, h, pt: (pt[b, h * pages + k], c, 0, 0)

    in_specs = [pl.BlockSpec((None, None, gd, PAGE), page_map(k)) for k in range(pages)]
    in_specs += [pl.BlockSpec((None,) + pe.shape[1:], lambda c, b, h, pt: (c, 0, 0)),
                 pl.BlockSpec((None,) + w1.shape[1:], lambda c, b, h, pt: (c, 0, 0, 0)),
                 pl.BlockSpec((None,) + w2.shape[1:], lambda c, b, h, pt: (c, 0, 0))]
    grid_spec = pltpu.PrefetchScalarGridSpec(
        num_scalar_prefetch=1,
        grid=(2, bsz, groups),
        in_specs=in_specs,
        out_specs=pl.BlockSpec((None, pages * NSA_G, LANES), lambda c, b, h, pt: (c, b * groups + h, 0)),
        scratch_shapes=[pltpu.VMEM((pages * gd, PAGE), F32)],
    )
    return pl.pallas_call(
        functools.partial(_compress_pages_kernel, pages=pages),
        grid_spec=grid_spec,
        out_shape=jax.ShapeDtypeStruct((2, bsz * n_pages * NSA_G, LANES), F32),
        compiler_params=_params(("arbitrary", "arbitrary", "arbitrary")),
        name="compress_pages",
    )(page_table, *([cache] * pages), pe, w1, w2)


def _softmax_tile(carry, s_t, vt_tile, bias, tq):
    ms, acc = carry
    new_ms, ps, alphas = [], [], []
    for r in range(NSA_R):
        sb = s_t[:, r * tq:(r + 1) * tq] + bias
        m_new = jnp.maximum(ms[r], jnp.max(sb, axis=0, keepdims=True))
        ps.append(jnp.exp2(sb - m_new).astype(BF16))
        alphas.append(jnp.exp2(ms[r] - m_new))
        new_ms.append(m_new)
    acc = jnp.concatenate(alphas, axis=1) * acc + _dot(vt_tile, jnp.concatenate(ps, axis=1))
    return tuple(new_ms), acc


def _rank_blocks(score_ref, rank_ref, last_blk):
    nb, tq = score_ref.shape
    sub = lax.broadcasted_iota(jnp.int32, (8, tq), 0)
    rank_ref[...] = jnp.zeros((nb, tq), F32)
    for gi in range(nb // 8):
        @pl.when(gi * 8 <= last_blk)
        def _(gi=gi):
            rows8 = score_ref[gi * 8:(gi + 1) * 8, :]
            tops = [jnp.broadcast_to(rows8[u:u + 1, :], (8, tq)) for u in range(8)]
            for v in range(nb // 8):
                sv = score_ref[v * 8:(v + 1) * 8, :]
                cnt = jnp.zeros((8, tq), F32)
                for u in range(8):
                    if v > gi:
                        before = tops[u] >= sv
                    elif v < gi:
                        before = tops[u] > sv
                    else:
                        before = (tops[u] > sv) | ((tops[u] == sv) & (sub > u))
                    cnt = cnt + jnp.where(before, 1.0, 0.0)
                rank_ref[v * 8:(v + 1) * 8, :] += cnt


def _nsa_attn_kernel(qraw_ref, qrot_ref, ck_ref, cv_ref, sk_ref, svt_ref, wk_ref, wvt_ref,
                     gates_ref, out_ref, score_ref, rank_ref, *, tq, kt, n_sel, unroll):
    nb = ck_ref.shape[0]
    rows = NSA_R * tq
    q0 = pl.program_id(2) * tq

    lane = lax.broadcasted_iota(jnp.int32, (tq, LANES), 1)
    lo = lane < HALF

    def stack(ref):
        qa = ref[:, 0:LANES].astype(F32)
        qb = ref[:, LANES:2 * LANES].astype(F32)
        parts = [jnp.where(lo, qa, 0.0), jnp.where(lo, 0.0, qa),
                 jnp.where(lo, qb, 0.0), jnp.where(lo, 0.0, qb)]
        return jnp.concatenate(parts, axis=0).astype(BF16)

    q_raw = stack(qraw_ref)
    q_rot = stack(qrot_ref)
    ck = ck_ref[...].astype(BF16)
    cv = cv_ref[...].astype(BF16)

    st = _dot_nt(ck, q_raw)
    t_c = q0 + (lax.broadcasted_iota(jnp.int32, (nb, rows), 1) & (tq - 1))
    n_c = lax.broadcasted_iota(jnp.int32, (nb, rows), 0)
    mask_t = (n_c + 1) * CMP_BLK - 1 <= t_c
    st = jnp.where(mask_t, st, NEG_INF)
    pt = jnp.where(mask_t, jnp.exp(st - jnp.max(st, axis=0, keepdims=True)), 0.0)
    pt = pt / jnp.maximum(jnp.sum(pt, axis=0, keepdims=True), 1e-30)
    o_cmp = _dot_tn(cv[:, 0:NSA_D], pt.astype(BF16))

    imp = pt[:, 0:tq]
    for r in range(1, NSA_R):
        imp = imp + pt[:, r * tq:(r + 1) * tq]
    blk = lax.broadcasted_iota(jnp.int32, (nb, tq), 0)
    cur = (q0 + lax.broadcasted_iota(jnp.int32, (nb, tq), 1)) >> SEL_SHIFT
    forced = (blk == 0) | (blk == cur) | (blk == cur - 1)
    valid = blk <= cur
    score_ref[...] = jnp.where(forced, SEL_FORCE, jnp.where(valid, imp, -1.0))
    _rank_blocks(score_ref, rank_ref, (q0 + tq - 1) >> SEL_SHIFT)
    score_ref[...] = jnp.where((rank_ref[...] < n_sel) & valid, 0.0, NEG_INF)

    key_row = lax.broadcasted_iota(jnp.int32, (kt, tq), 0)
    t_lane = q0 + lax.broadcasted_iota(jnp.int32, (kt, tq), 1)
    ones = jnp.ones((ONES_ROWS, kt), BF16)

    init = (tuple(jnp.full((1, tq), NEG_INF, F32) for _ in range(NSA_R)),
            jnp.zeros((NSA_D + ONES_ROWS, rows), F32))

    hi = (q0 + tq + kt - 1) // kt
    n_tiles = sk_ref.shape[0] // kt

    def tile_loop(lo_tile, trips, k_ref, vt_ref, make_bias, unroll, carry):
        def body(i, carry):
            staged = []
            for u in range(unroll):
                j = lo_tile + i * unroll + u
                jc = jnp.minimum(j, n_tiles - 1)
                k0 = pl.multiple_of(jc * kt, kt)
                k_pos = jnp.where(j < hi, k0, 2 * n_tiles * kt) + key_row
                staged.append((_dot_nt(k_ref[pl.ds(k0, kt), :], q_rot), make_bias(k0, k_pos), jc))
            for s_t, bias, jc in staged:
                vt = jnp.concatenate([vt_ref[jc], ones], axis=0)
                carry = _softmax_tile(carry, s_t, vt, bias, tq)
            return carry
        return lax.fori_loop(0, trips, body, carry)

    def sel_bias(k0, k_pos):
        blk0 = k0 // SEL_BLK
        picked = jnp.concatenate(
            [jnp.broadcast_to(score_ref[pl.ds(blk0 + i, 1), :], (SEL_BLK, tq)) for i in range(kt // SEL_BLK)],
            axis=0)
        return jnp.where(k_pos <= t_lane, picked, NEG_INF)

    full = (hi + 1) // unroll
    carry = tile_loop(0, full, sk_ref, svt_ref, sel_bias, unroll, init)
    acc_s = tile_loop(full * unroll, (hi - full * unroll + 1) // 2, sk_ref, svt_ref, sel_bias, 2, carry)[1]

    def win_bias(k0, k_pos):
        diff = t_lane - k_pos
        return jnp.where((diff >= 0) & (diff <= WINDOW), 0.0, NEG_INF)

    acc_w = tile_loop(jnp.maximum(q0 - WINDOW, 0) // kt, 1, wk_ref, wvt_ref, win_bias, WINDOW // kt + 1, init)[1]

    gt = gates_ref[...].T
    o_s = acc_s[0:NSA_D, :] / jnp.maximum(acc_s[NSA_D:NSA_D + 1, :], 1e-30)
    o_w = acc_w[0:NSA_D, :] / jnp.maximum(acc_w[NSA_D:NSA_D + 1, :], 1e-30)
    heads = []
    for r in range(NSA_R):
        sl = slice(r * tq, (r + 1) * tq)
        heads.append(gt[r:r + 1, :] * o_cmp[:, sl] + gt[NSA_R + r:NSA_R + r + 1, :] * o_s[:, sl]
                     + gt[2 * NSA_R + r:2 * NSA_R + r + 1, :] * o_w[:, sl])
    for pair in range(NSA_R // 2):
        both = jnp.concatenate(heads[2 * pair:2 * pair + 2], axis=0)
        out_ref[:, pair * LANES:(pair + 1) * LANES] = both.T.astype(BF16)


def _nsa_attn_prompt(qraw, qrot, ckv, skd, svt, wkd, wvt, gates, *, batch, seq, tq, kt):
    assert tq == LANES and kt % tq == 0 and WINDOW % kt == 0 and seq % kt == 0
    nqt = seq // tq
    nb = seq // CMP_BLK
    gc = NSA_R * NSA_D
    q_map = lambda b, g, t: (b * nqt + t, g)
    kv_map = lambda b, g, t: (b, g)
    vt_map = lambda b, g, t: (b, g, 0)
    return pl.pallas_call(
        functools.partial(_nsa_attn_kernel, tq=tq, kt=kt, n_sel=min(N_SEL, seq // SEL_BLK),
                          unroll=ATTN_UNROLL),
        grid=(batch, NSA_G, nqt),
        in_specs=[pl.BlockSpec((tq, gc), q_map), pl.BlockSpec((tq, gc), q_map),
                  pl.BlockSpec((None, nb, LANES), lambda b, g, t: (0, b, g)),
                  pl.BlockSpec((None, nb, LANES), lambda b, g, t: (1, b, g)),
                  pl.BlockSpec((seq, LANES), kv_map), pl.BlockSpec((seq // kt, NSA_D, kt), vt_map),
                  pl.BlockSpec((seq, LANES), kv_map), pl.BlockSpec((seq // kt, NSA_D, kt), vt_map),
                  pl.BlockSpec((tq, LANES), q_map)],
        out_specs=pl.BlockSpec((tq, gc), q_map),
        out_shape=jax.ShapeDtypeStruct((batch * seq, NSA_H * NSA_D), BF16),
        scratch_shapes=[pltpu.VMEM((nb, tq), F32), pltpu.VMEM((nb, tq), F32)],
        compiler_params=_params(("arbitrary", "arbitrary", "arbitrary")),
        name="nsa_attn_prompt",
    )(qraw, qrot, ckv, ckv, skd, svt, wkd, wvt, gates)


def _nsa_sample_select_kernel(qe_ref, qo_ref, ck_ref, cv_ref, ocmp_ref, idx_ref):
    npg = ck_ref.shape[0] // NSA_G
    nbp = 2 * npg
    qe, qo = qe_ref[...], qo_ref[...]
    head = lax.broadcasted_iota(jnp.int32, (NSA_H, LANES), 0)

    def block_of(j):
        return jnp.where(j < npg, 2 * j, 2 * (j - npg) + 1)

    n_row = block_of(lax.broadcasted_iota(jnp.int32, (1, nbp), 1))
    n_col = block_of(lax.broadcasted_iota(jnp.int32, (nbp, 1), 0))
    forced = (n_row == 0) | (n_row == nbp - 1)
    diag = lax.broadcasted_iota(jnp.int32, (nbp, nbp), 0) == lax.broadcasted_iota(jnp.int32, (nbp, nbp), 1)
    k_idx = lax.broadcasted_iota(jnp.int32, (N_SEL, nbp), 0).astype(F32)
    n_f = jnp.broadcast_to(n_row, (N_SEL, nbp)).astype(F32)
    ocmp = jnp.zeros((NSA_H, LANES), F32)
    for g in range(NSA_G):
        ck = ck_ref[pl.ds(g, npg, stride=NSA_G), :].astype(BF16)
        cv = cv_ref[pl.ds(g, npg, stride=NSA_G), :].astype(BF16)
        s = jnp.concatenate([_dot_nt(qe, ck), _dot_nt(qo, ck)], axis=1)
        p = jnp.exp(s - jnp.max(s, axis=-1, keepdims=True))
        p = p / jnp.maximum(jnp.sum(p, axis=-1, keepdims=True), 1e-30)
        o_e = _dot(p[:, :npg].astype(BF16), cv)
        o_o = _dot(p[:, npg:].astype(BF16), cv)
        ocmp = jnp.where((head >> R_SHIFT) == g, o_e + pltpu.roll(o_o, HALF, 1), ocmp)

        imp = p[NSA_R * g:NSA_R * g + 1, :]
        for r in range(1, NSA_R):
            imp = imp + p[NSA_R * g + r:NSA_R * g + r + 1, :]
        row = jnp.where(forced, SEL_FORCE, imp)
        col = jnp.sum(jnp.where(diag, row, 0.0), axis=1, keepdims=True)
        beats = (col > row) | ((col == row) & (n_col < n_row))
        rank = jnp.sum(jnp.where(beats, 1.0, 0.0), axis=0, keepdims=True)
        onehot = jnp.where(rank == k_idx, n_f, 0.0)
        idx_ref[g] = jnp.sum(onehot, axis=1, keepdims=True).astype(jnp.int32)
    ocmp_ref[...] = ocmp


def _nsa_sample_select(q_e, q_o, ckv, *, bsz):
    rows = ckv.shape[1] // bsz
    return pl.pallas_call(
        _nsa_sample_select_kernel,
        grid=(bsz,),
        in_specs=[pl.BlockSpec((None, NSA_H, LANES), lambda b: (b, 0, 0)),
                  pl.BlockSpec((None, NSA_H, LANES), lambda b: (b, 0, 0)),
                  pl.BlockSpec((None, rows, LANES), lambda b: (0, b, 0)),
                  pl.BlockSpec((None, rows, LANES), lambda b: (1, b, 0))],
        out_specs=(pl.BlockSpec((None, NSA_H, LANES), lambda b: (b, 0, 0)),
                   pl.BlockSpec((None, NSA_G, N_SEL, 1), lambda b: (b, 0, 0, 0))),
        out_shape=(jax.ShapeDtypeStruct((bsz, NSA_H, LANES), F32),
                   jax.ShapeDtypeStruct((bsz, NSA_G, N_SEL, 1), jnp.int32)),
        compiler_params=_params(("arbitrary",)),
        name="nsa_sample_select",
    )(q_e, q_o, ckv, ckv)


def _nsa_sample_attn_kernel(idx_ref, pt_ref, *refs, n_past):
    del pt_ref
    slabs = refs[:NSA_G * n_past]
    (q64_ref, q_ref, snew_ref, wcache_ref, wnew_ref, gates_ref, ocmp_ref,
     out_ref, wout_ref, wbuf_ref) = refs[NSA_G * n_past:]
    kc = NSA_G * NSA_D
    b = pl.program_id(0)
    q64 = q64_ref[...]
    q = q_ref[...]
    head_row = lax.broadcasted_iota(jnp.int32, (NSA_H, kc), 0)
    head_lane = lax.broadcasted_iota(jnp.int32, (NSA_H, kc), 1)
    own = (head_lane >> SEL_SHIFT) == (head_row >> R_SHIFT)
    fold = jnp.where((lax.broadcasted_iota(jnp.int32, (kc, NSA_D), 0) & (NSA_D - 1))
                     == lax.broadcasted_iota(jnp.int32, (kc, NSA_D), 1), 1.0, 0.0)
    hi = lax.Precision.HIGHEST

    def rounded(x):
        return x.astype(BF16).astype(F32)

    snew = snew_ref[...]
    s_new = jnp.sum(q.astype(F32) * rounded(snew[:, :kc]), axis=-1, keepdims=True)
    v_new = jnp.dot(jnp.where(own, rounded(snew[:, kc:]), 0.0), fold, precision=hi,
                    preferred_element_type=F32)
    half_of_lane = lax.broadcasted_iota(jnp.int32, (1, PAGE), 1) >> SEL_SHIFT
    row64 = lax.broadcasted_iota(jnp.int32, (NSA_H, NSA_D), 0)
    f_sel = jnp.zeros((NSA_H, NSA_D), F32)
    for g in range(NSA_G):
        mine = slabs[g * n_past:(g + 1) * n_past]
        k_cat = jnp.concatenate([r[0] for r in mine], axis=1).astype(BF16)
        v_cat = jnp.concatenate([r[1] for r in mine], axis=1).astype(BF16)
        bias = jnp.concatenate(
            [jnp.where(half_of_lane == (idx_ref[b, g * n_past + k] & 1), 0.0, NEG_INF)
             for k in range(n_past)], axis=1)
        sb = _dot(q64, k_cat) + bias
        m = jnp.maximum(jnp.max(sb, axis=-1, keepdims=True), s_new)
        p = jnp.exp2(sb - m)
        p_new = jnp.exp2(s_new - m)
        l = jnp.sum(p, axis=-1, keepdims=True) + p_new
        o = _dot_nt(p.astype(BF16), v_cat) + rounded(p_new) * v_new
        f_sel = jnp.where((row64 >> R_SHIFT) == g, o / jnp.maximum(l, 1e-30), f_sel)

    wb = wcache_ref.shape[0]
    wbuf_ref[0:wb, :] = wcache_ref[...]
    wbuf_ref[wb:wb + 1, :] = wnew_ref[...]
    wbuf_ref[wb + 1:, :] = jnp.zeros((wbuf_ref.shape[0] - wb - 1, 2 * kc), F32)
    s = _dot_nt(q, wbuf_ref[:, :kc].astype(BF16))
    ok = lax.broadcasted_iota(jnp.int32, s.shape, 1) < wb + 1
    s = jnp.where(ok, s, NEG_INF)
    p = jnp.where(ok, jnp.exp2(s - jnp.max(s, axis=-1, keepdims=True)), 0.0)
    p = p / jnp.maximum(jnp.sum(p, axis=-1, keepdims=True), 1e-30)
    o_win = _dot(p.astype(BF16), wbuf_ref[:, kc:].astype(BF16))
    f_win = jnp.dot(jnp.where(own, o_win, 0.0), fold, precision=hi, preferred_element_type=F32)
    wout_ref[...] = wbuf_ref[1:wb + 1, :]

    gts = gates_ref[...]
    out_ref[...] = (gts[:, 0:1] * ocmp_ref[:, 0:NSA_D] + gts[:, 1:2] * f_sel + gts[:, 2:3] * f_win)


def _nsa_sample_attn(idx, page_table, slc_cache, q64, q_cmp, slc_new, win_cache, win_new, gates, ocmp):
    bsz = page_table.shape[0]
    n_past = idx.shape[1] // NSA_G
    kc = NSA_G * NSA_D
    wb = win_cache.shape[1]
    assert PAGE == 2 * SEL_BLK

    def slab_map(j):
        def f(b, idx_ref, pt_ref):
            return (pt_ref[b, idx_ref[b, j] // 2], 0, j // n_past, 0, 0)
        return f

    pad = lambda n: -(-n // LANES) * LANES
    in_specs = [pl.BlockSpec((None, 2, None, NSA_D, PAGE), slab_map(j)) for j in range(NSA_G * n_past)]
    in_specs += [pl.BlockSpec((None, NSA_H, NSA_D), lambda b, i, p: (b, 0, 0)),
                 pl.BlockSpec((None, NSA_H, kc), lambda b, i, p: (b, 0, 0)),
                 pl.BlockSpec((None, 1, 2 * kc), lambda b, i, p: (b, 0, 0)),
                 pl.BlockSpec((None, wb, 2 * kc), lambda b, i, p: (b, 0, 0)),
                 pl.BlockSpec((None, 1, 2 * kc), lambda b, i, p: (b, 0, 0)),
                 pl.BlockSpec((None, NSA_H, 3), lambda b, i, p: (b, 0, 0)),
                 pl.BlockSpec((None, NSA_H, LANES), lambda b, i, p: (b, 0, 0))]
    grid_spec = pltpu.PrefetchScalarGridSpec(
        num_scalar_prefetch=2,
        grid=(bsz,),
        in_specs=in_specs,
        out_specs=(pl.BlockSpec((None, NSA_H, NSA_D), lambda b, i, p: (b, 0, 0)),
                   pl.BlockSpec((None, wb, 2 * kc), lambda b, i, p: (b, 0, 0))),
        scratch_shapes=[pltpu.VMEM((pad(wb + 1), 2 * kc), F32)],
    )
    return pl.pallas_call(
        functools.partial(_nsa_sample_attn_kernel, n_past=n_past),
        grid_spec=grid_spec,
        out_shape=(jax.ShapeDtypeStruct((bsz, NSA_H, NSA_D), F32),
                   jax.ShapeDtypeStruct((bsz, wb, 2 * kc), F32)),
        compiler_params=_params(("arbitrary",)),
        name="nsa_sample_attn",
    )(idx, page_table, *([slc_cache] * (NSA_G * n_past)), q64, q_cmp, slc_new, win_cache, win_new, gates, ocmp)


def _proj_ln_kernel(a_ref, w_ref, x_ref, g_ref, b_ref, out_ref, *, alpha):
    y = _dot(a_ref[...].astype(BF16), w_ref[...])
    out_ref[...] = _layer_norm(alpha * x_ref[...] + y, g_ref[...], b_ref[...])


def _gla_proj_ln_kernel(o_ref, rs_ref, ng_ref, w_ref, x_ref, g_ref, b_ref, out_ref, *, alpha):
    a = (o_ref[...] * ng_ref[...] * rs_ref[...]).astype(BF16)
    y = _dot(a, w_ref[...])
    out_ref[...] = _layer_norm(alpha * x_ref[...] + y, g_ref[...], b_ref[...])


def _proj_ln(a, w, x, g, b, *, tm, alpha):
    n, d = x.shape
    row = lambda i: (i, 0)
    const = lambda i: (0, 0)
    return pl.pallas_call(
        functools.partial(_proj_ln_kernel, alpha=alpha),
        grid=(n // tm,),
        in_specs=[pl.BlockSpec((tm, a.shape[1]), row), pl.BlockSpec(w.shape, const),
                  pl.BlockSpec((tm, d), row), pl.BlockSpec((1, d), const), pl.BlockSpec((1, d), const)],
        out_specs=pl.BlockSpec((tm, d), row),
        out_shape=jax.ShapeDtypeStruct((n, d), F32),
        compiler_params=_params(("arbitrary",)),
        name="proj_ln",
    )(a, w, x, g, b)


def _gla_proj_ln(o, rs, ng, w, x, g, b, *, tm, alpha):
    n, d = x.shape
    row = lambda i: (i, 0)
    const = lambda i: (0, 0)
    return pl.pallas_call(
        functools.partial(_gla_proj_ln_kernel, alpha=alpha),
        grid=(n // tm,),
        in_specs=[pl.BlockSpec((tm, o.shape[1]), row), pl.BlockSpec((tm, o.shape[1]), row),
                  pl.BlockSpec((1, o.shape[1]), const), pl.BlockSpec(w.shape, const),
                  pl.BlockSpec((tm, d), row), pl.BlockSpec((1, d), const), pl.BlockSpec((1, d), const)],
        out_specs=pl.BlockSpec((tm, d), row),
        out_shape=jax.ShapeDtypeStruct((n, d), F32),
        compiler_params=_params(("arbitrary",)),
        name="gla_proj_ln",
    )(o, rs, ng, w, x, g, b)


def _ffn_kernel(x_ref, wg_ref, wu_ref, wd_ref, g_ref, b_ref, out_ref, acc_ref, xb_ref, *, alpha):
    k = pl.program_id(1)

    @pl.when(k == 0)
    def _():
        xb_ref[...] = x_ref[...].astype(BF16)
        acc_ref[...] = jnp.zeros_like(acc_ref)

    xb = xb_ref[...]
    h = _silu(_dot(xb, wg_ref[...])) * _dot(xb, wu_ref[...])
    acc_ref[...] += _dot(h.astype(BF16), wd_ref[...])

    @pl.when(k == pl.num_programs(1) - 1)
    def _():
        out_ref[...] = _layer_norm(alpha * x_ref[...] + acc_ref[...], g_ref[...], b_ref[...])


def _ffn(x, w_gu, w_down, g, b, *, layer, tm, alpha):
    n, d = x.shape
    d_ff = w_down.shape[1]
    splits = 2
    tf = d_ff // splits
    assert tf * splits == d_ff and tf % LANES == 0
    return pl.pallas_call(
        functools.partial(_ffn_kernel, alpha=alpha),
        grid=(n // tm, splits),
        in_specs=[pl.BlockSpec((tm, d), lambda i, k: (i, 0)),
                  pl.BlockSpec((None, d, tf), lambda i, k: (layer, 0, k)),
                  pl.BlockSpec((None, d, tf), lambda i, k: (layer, 0, splits + k)),
                  pl.BlockSpec((None, tf, d), lambda i, k: (layer, k, 0)),
                  pl.BlockSpec((1, d), lambda i, k: (0, 0)),
                  pl.BlockSpec((1, d), lambda i, k: (0, 0))],
        out_specs=pl.BlockSpec((tm, d), lambda i, k: (i, 0)),
        out_shape=jax.ShapeDtypeStruct((n, d), F32),
        scratch_shapes=[pltpu.VMEM((tm, d), F32), pltpu.VMEM((tm, d), BF16)],
        compiler_params=_params(("arbitrary", "arbitrary")),
        name="ffn",
    )(x, w_gu, w_gu, w_down, g, b)


def _gla_log_decay(low, w2_ref, b2_ref):
    z = _dot(low.astype(BF16), w2_ref[...]) + b2_ref[...]
    return jax.nn.log_sigmoid(z) / GLA_TAU


def _gla_proj_prompt_kernel(x_ref, w_ref, w2_ref, b2_ref, qi_ref, ki_ref, ks_ref, v_ref, rs_ref,
                            al_ref, *, dk, dv, scale):
    tm = x_ref.shape[0]
    xb = x_ref[...].astype(BF16)
    q = _dot(xb, w_ref[:, 0:dk])
    k = _dot(xb, w_ref[:, dk:2 * dk])
    v_ref[...] = _dot(xb, w_ref[:, 2 * dk:2 * dk + dv]).astype(BF16)
    rs_ref[...] = _silu(_dot(xb, w_ref[:, 2 * dk + dv:2 * dk + 2 * dv]))
    low = _dot(xb, w_ref[:, 2 * dk + 2 * dv:])
    la = _gla_log_decay(low, w2_ref, b2_ref)

    rin = lax.broadcasted_iota(jnp.int32, (tm, dk), 0) & (GLA_CHUNK - 1)
    lb = la
    s = 1
    while s < GLA_CHUNK:
        lb = lb + jnp.where(rin >= s, pltpu.roll(lb, s, 0), 0.0)
        s *= 2
    lb3 = lb.reshape(tm // GLA_CHUNK, GLA_CHUNK, dk)
    last = lb3[:, GLA_CHUNK - 1:GLA_CHUNK, :]
    qi_ref[...] = (q * jnp.exp(lb) * scale).astype(BF16)
    ki_ref[...] = (k * jnp.exp(-lb)).astype(BF16)
    ks_ref[...] = (k * jnp.exp(last - lb3).reshape(tm, dk)).astype(BF16)
    al_ref[...] = jnp.exp(last.reshape(tm // GLA_CHUNK, dk))


def _gla_proj_prompt(x, w, w2, b2, *, tm, dk, dv):
    n, d = x.shape
    row = lambda i: (i, 0)
    const = lambda i: (0, 0)
    return pl.pallas_call(
        functools.partial(_gla_proj_prompt_kernel, dk=dk, dv=dv, scale=(dk // GLA_H) ** -0.5),
        grid=(n // tm,),
        in_specs=[pl.BlockSpec((tm, d), row), pl.BlockSpec(w.shape, const),
                  pl.BlockSpec(w2.shape, const), pl.BlockSpec(b2.shape, const)],
        out_specs=(pl.BlockSpec((tm, dk), row), pl.BlockSpec((tm, dk), row), pl.BlockSpec((tm, dk), row),
                   pl.BlockSpec((tm, dv), row), pl.BlockSpec((tm, dv), row),
                   pl.BlockSpec((tm // GLA_CHUNK, dk), row)),
        out_shape=(jax.ShapeDtypeStruct((n, dk), BF16), jax.ShapeDtypeStruct((n, dk), BF16),
                   jax.ShapeDtypeStruct((n, dk), BF16), jax.ShapeDtypeStruct((n, dv), BF16),
                   jax.ShapeDtypeStruct((n, dv), F32),
                   jax.ShapeDtypeStruct((n // GLA_CHUNK, dk), F32)),
        compiler_params=_params(("arbitrary",)),
        name="gla_proj_prompt",
    )(x, w, w2, b2)


def _gla_proj_sample_kernel(x_ref, w_ref, w2_ref, b2_ref, q_ref, k_ref, v_ref, rs_ref, la_ref, *, dk, dv):
    xb = x_ref[...].astype(BF16)
    q_ref[...] = _dot(xb, w_ref[:, 0:dk])
    k_ref[...] = _dot(xb, w_ref[:, dk:2 * dk])
    v_ref[...] = _dot(xb, w_ref[:, 2 * dk:2 * dk + dv])
    rs_ref[...] = _silu(_dot(xb, w_ref[:, 2 * dk + dv:2 * dk + 2 * dv]))
    low = _dot(xb, w_ref[:, 2 * dk + 2 * dv:])
    la_ref[...] = _gla_log_decay(low, w2_ref, b2_ref)


def _gla_proj_sample(x, w, w2, b2, *, dk, dv):
    n, d = x.shape
    full = lambda s: pl.BlockSpec(s, lambda i: (0,) * len(s))
    return pl.pallas_call(
        functools.partial(_gla_proj_sample_kernel, dk=dk, dv=dv),
        grid=(1,),
        in_specs=[full((n, d)), full(w.shape), full(w2.shape), full(b2.shape)],
        out_specs=(full((n, dk)), full((n, dk)), full((n, dv)), full((n, dv)), full((n, dk))),
        out_shape=(jax.ShapeDtypeStruct((n, dk), F32), jax.ShapeDtypeStruct((n, dk), F32),
                   jax.ShapeDtypeStruct((n, dv), F32), jax.ShapeDtypeStruct((n, dv), F32),
                   jax.ShapeDtypeStruct((n, dk), F32)),
        compiler_params=_params(("arbitrary",)),
        name="gla_proj_sample",
    )(x, w, w2, b2)


def _gla_rec_kernel(qi_ref, ki_ref, ks_ref, v_ref, al_ref, o_ref, st_out_ref, st_ref):
    t = pl.program_id(2)

    @pl.when(t == 0)
    def _():
        st_ref[...] = jnp.zeros_like(st_ref)

    c = GLA_CHUNK
    ct = qi_ref.shape[0]
    n = ct // c
    qi, ki, vv = qi_ref[...], ki_ref[...], v_ref[...]
    row = lax.broadcasted_iota(jnp.int32, (ct, ct), 0)
    col = lax.broadcasted_iota(jnp.int32, (ct, ct), 1)
    keep = (row >= col) & ((row >> GLA_CHUNK_SHIFT) == (col >> GLA_CHUNK_SHIFT))
    att = jnp.where(keep, _dot_nt(qi, ki), 0.0)
    o_intra = _dot(att.astype(BF16), vv)
    incs = [_dot_tn(vv[i * c:(i + 1) * c], ks_ref[i * c:(i + 1) * c, :]) for i in range(n)]
    st = st_ref[...]
    entering = []
    for i in range(n):
        entering.append(st.astype(BF16))
        st = al_ref[i:i + 1, :] * st + incs[i]
    st_ref[...] = st
    o = o_intra + jnp.concatenate(
        [_dot_nt(qi[i * c:(i + 1) * c], entering[i]) for i in range(n)], axis=0)
    o_ref[...] = o * lax.rsqrt(jnp.mean(o * o, axis=-1, keepdims=True) + LN_EPS)

    @pl.when(t == pl.num_programs(2) - 1)
    def _():
        st_out_ref[...] = st


def _gla_rec(qi, ki, ks, v, al, *, batch, seq, dkh, dvh):
    ct = 512
    nct = seq // ct
    cpt = ct // GLA_CHUNK
    tok = lambda b, h, t: (b * nct + t, h)
    return pl.pallas_call(
        _gla_rec_kernel,
        grid=(batch, GLA_H, nct),
        in_specs=[pl.BlockSpec((ct, dkh), tok), pl.BlockSpec((ct, dkh), tok), pl.BlockSpec((ct, dkh), tok),
                  pl.BlockSpec((ct, dvh), tok), pl.BlockSpec((cpt, dkh), tok)],
        out_specs=(pl.BlockSpec((ct, dvh), tok),
                   pl.BlockSpec((None, None, dvh, dkh), lambda b, h, t: (b, h, 0, 0))),
        out_shape=(jax.ShapeDtypeStruct((batch * seq, GLA_H * dvh), F32),
                   jax.ShapeDtypeStruct((batch, GLA_H, dvh, dkh), F32)),
        scratch_shapes=[pltpu.VMEM((dvh, dkh), F32)],
        compiler_params=_params(("arbitrary", "arbitrary", "arbitrary")),
        name="gla_rec",
    )(qi, ki, ks, v, al)


def _gla_step_kernel(q_ref, k_ref, la_ref, v_ref, s0_ref, o_ref, s_ref, *, scale):
    for h in range(s0_ref.shape[0]):
        la = la_ref[h]
        a = jnp.exp(la)
        k = k_ref[h]
        qi = q_ref[h] * a * scale
        ki = k * jnp.exp(-la)
        v = v_ref[h]
        s0 = s0_ref[h]
        att = jnp.sum(qi * ki, axis=0, keepdims=True)
        o = att * v + jnp.sum(qi * s0, axis=0, keepdims=True)
        s_ref[h] = a * s0 + k * v
        o_ref[h] = o * lax.rsqrt(jnp.mean(o * o, axis=-1, keepdims=True) + LN_EPS)


def _gla_step(q, k, la, v, s0, *, scale):
    bsz, h, dkh, dvh = s0.shape
    col = pl.BlockSpec((None, h, dkh, 1), lambda b: (b, 0, 0, 0))
    rowv = pl.BlockSpec((None, h, 1, dvh), lambda b: (b, 0, 0, 0))
    mat = pl.BlockSpec((None, h, dkh, dvh), lambda b: (b, 0, 0, 0))
    return pl.pallas_call(
        functools.partial(_gla_step_kernel, scale=scale),
        grid=(bsz,),
        in_specs=[col, col, col, rowv, mat],
        out_specs=(rowv, mat),
        out_shape=(jax.ShapeDtypeStruct((bsz, h, 1, dvh), F32),
                   jax.ShapeDtypeStruct((bsz, h, dkh, dvh), F32)),
        compiler_params=_params(("arbitrary",)),
        name="gla_step",
    )(q, k, la, v, s0)


def _rope_tables(pos):
    half = NSA_D // 2
    inv = ROPE_THETA ** (-jnp.arange(half, dtype=F32) / half)
    ang = pos.astype(F32)[:, None] * inv[None, :]
    cos, sin = jnp.cos(ang), jnp.sin(ang)
    zero = jnp.zeros_like(sin)
    cos_t = jnp.tile(cos, (1, LANES // half))
    sin_lo = jnp.tile(jnp.concatenate([-sin, zero], axis=1), (1, LANES // NSA_D))
    sin_hi = jnp.tile(jnp.concatenate([zero, sin], axis=1), (1, LANES // NSA_D))
    return cos_t, sin_lo, sin_hi


def _prep_nsa_weights(w_in, b_gate, pe, w1, w2):
    d = w_in.shape[0]
    qc = NSA_H * NSA_D
    kc = NSA_G * NSA_D
    gate_off = qc + 6 * kc
    wg = w_in[:, gate_off:].reshape(d, 3, NSA_G, NSA_R).transpose(0, 2, 1, 3).reshape(d, NSA_G, 3 * NSA_R)
    wg = jnp.pad(wg, ((0, 0), (0, 0), (0, LANES - 3 * NSA_R))).reshape(d, NSA_G * LANES)
    bg = b_gate.reshape(3, NSA_G, NSA_R).transpose(1, 0, 2).reshape(NSA_G, 3 * NSA_R)
    bg = jnp.pad(bg, ((0, 0), (0, LANES - 3 * NSA_R))).reshape(1, NSA_G * LANES)
    w = jnp.concatenate([w_in[:, :gate_off], wg], axis=1).astype(BF16)
    eye = jnp.eye(NSA_G, dtype=F32)
    pe_x = jnp.tile(pe.transpose(1, 0, 2), (1, 1, NSA_G))
    same_head = (jnp.arange(kc)[:, None] // NSA_D == jnp.arange(kc)[None, :] // NSA_D).astype(BF16)
    w1_bd = jnp.tile(w1.transpose(1, 0, 2, 3).astype(BF16), (1, 1, NSA_G, NSA_G)) * same_head
    w2_dup = jnp.concatenate([w2, w2], axis=-1)
    w2_bd = jnp.einsum('cef,gh->cgehf', w2_dup, eye).reshape(2, kc, 2 * kc).astype(BF16)
    return w, bg, pe_x, w1_bd, w2_bd


def _prep_vt_weights(w_in):
    qc = NSA_H * NSA_D
    kc = NSA_G * NSA_D
    out = []
    for branch in (1, 2):
        lo = qc + branch * 2 * kc + kc
        out.append(w_in[:, lo:lo + kc].T)
    return jnp.stack(out).astype(BF16)


def _prep_page_compress_weights(pe, w1, w2):
    nd = NSA_D // 2
    eye = jnp.eye(PAGE // CMP_BLK, dtype=F32)
    pe_n = jnp.broadcast_to(pe.transpose(1, 2, 0).reshape(2, nd, 2, 1, CMP_BLK),
                            (2, nd, 2, PAGE // CMP_BLK, CMP_BLK)).reshape(2, nd, 2 * PAGE)
    w1c = w1.transpose(1, 2, 0, 3).reshape(2, nd, 2, CMP_BLK, NSA_D).astype(BF16)
    halves = PAGE // CMP_BLK
    same_half = (jnp.arange(PAGE)[:, None] // CMP_BLK == jnp.arange(PAGE)[None, :] // NSA_D).astype(BF16)
    w1_n = jnp.concatenate([jnp.tile(w1c[:, :, p], (1, 1, halves, halves)) * same_half for p in range(2)],
                           axis=2)
    w2_n = jnp.einsum('cef,hk->chekf', w2, eye).reshape(2, PAGE, PAGE).astype(BF16)
    return pe_n, w1_n, w2_n


def kernel(x_prompt, x_sample, cache_cmp_kv, cache_slc_kv, cache_win_kv, state_gla, page_table,
           nsa_w_in, nsa_b_gate, nsa_pe_cmp, nsa_w_cmp1, nsa_w_cmp2, nsa_w_out,
           gla_w_in, gla_w_gate2, gla_b_gate2, gla_norm_g, gla_w_out,
           ffn_w_gu, ffn_w_down, ln_g, ln_b):
    batch, seq, d = x_prompt.shape
    bsz, dec_seq, _ = x_sample.shape
    depth = ffn_w_gu.shape[0]
    n_pool = cache_cmp_kv.shape[1]
    n_pages = page_table.shape[1]
    past_len = n_pages * PAGE
    kc = NSA_G * NSA_D
    assert dec_seq == 1 and depth == 2 and d == NSA_H * NSA_D
    assert seq % 512 == 0 and seq >= WINDOW and past_len % SEL_BLK == 0
    assert cache_win_kv.shape[2] == WINDOW and past_len // SEL_BLK >= N_SEL
    alpha = (2.0 * depth) ** 0.25
    tm = 512
    nbp = past_len // CMP_BLK

    xp = x_prompt.reshape(batch * seq, d)
    xs = x_sample.reshape(bsz, d)
    ln_g = ln_g.reshape(depth, 2, 1, d)
    ln_b = ln_b.reshape(depth, 2, 1, d)
    w_gu = ffn_w_gu.astype(BF16)
    w_dn = ffn_w_down.astype(BF16)

    w0, bg0, pe_x, w1_bd, w2_bd = _prep_nsa_weights(nsa_w_in[0], nsa_b_gate[0], nsa_pe_cmp[0],
                                                    nsa_w_cmp1[0], nsa_w_cmp2[0])
    w_out0 = nsa_w_out[0].astype(BF16)
    tabs_p = _rope_tables(jnp.arange(seq, dtype=jnp.int32))
    tabs_s = _rope_tables(jnp.full((bsz,), past_len, dtype=jnp.int32))

    seq_tiles = seq // tm
    (qraw, qrot, cmp_p, slc_p, win_p, gates_p, skd, wkd, svt, wvt) = _nsa_proj(
        xp, w0, bg0, *tabs_p, _prep_vt_weights(nsa_w_in[0]), tm=tm, seq_tiles=seq_tiles,
        win_rows=batch * WINDOW, win_map=lambda i: (i // seq_tiles, 0), attn_kt=ATTN_KT)
    ckv_p = _compress_prompt(cmp_p, pe_x, w1_bd, w2_bd, batch=batch, seq=seq)
    o_p = _nsa_attn_prompt(qraw, qrot, ckv_p, skd, svt, wkd, wvt, gates_p, batch=batch, seq=seq,
                           tq=ATTN_TQ, kt=ATTN_KT)
    yp = _proj_ln(o_p, w_out0, xp, ln_g[0, 0], ln_b[0, 0], tm=tm, alpha=alpha)

    (qraw_s, qrot_s, cmp_s, slc_s, win_s, gates_s) = _nsa_proj(
        xs, w0, bg0, *tabs_s, tm=bsz, seq_tiles=1, win_rows=bsz, win_map=lambda i: (i, 0))
    pages_fm = lambda c: jnp.transpose(c[0], (0, 2, 3, 4, 1))
    pe_n, w1_n, w2_n = _prep_page_compress_weights(nsa_pe_cmp[0], nsa_w_cmp1[0], nsa_w_cmp2[0])
    ckv_s = _compress_pages(pages_fm(cache_cmp_kv).reshape(n_pool, 2, kc, PAGE), page_table, pe_n, w1_n, w2_n)
    q16 = qraw_s.reshape(bsz, NSA_H, NSA_D)
    zeros16 = jnp.zeros_like(q16)
    q_e = jnp.concatenate([q16, zeros16], axis=-1)
    q_o = jnp.concatenate([zeros16, q16], axis=-1)
    ocmp_s, idx4 = _nsa_sample_select(q_e, q_o, ckv_s, bsz=bsz)
    idx = idx4[:, :, :N_SEL - 1, 0].reshape(bsz, NSA_G * (N_SEL - 1))
    q4r = qrot_s.reshape(bsz, NSA_G, NSA_R, NSA_D)
    q_cmp = jnp.einsum('bgrd,gh->bgrhd', q4r, jnp.eye(NSA_G, dtype=BF16)).reshape(bsz, NSA_H, kc)
    gates_s3 = gates_s.reshape(bsz, NSA_G, LANES)[:, :, :3 * NSA_R].reshape(bsz, NSA_G, 3, NSA_R)
    gates_s3 = gates_s3.transpose(0, 1, 3, 2).reshape(bsz, NSA_H, 3)
    o_s, win_buf_s = _nsa_sample_attn(
        idx, page_table, pages_fm(cache_slc_kv), qrot_s.reshape(bsz, NSA_H, NSA_D), q_cmp,
        slc_s.reshape(bsz, 1, 2 * kc), cache_win_kv[0].reshape(bsz, WINDOW, 2 * kc),
        win_s.reshape(bsz, 1, 2 * kc), gates_s3, ocmp_s)
    ys = _proj_ln(o_s.reshape(bsz, d), w_out0, xs, ln_g[0, 0], ln_b[0, 0], tm=bsz, alpha=alpha)

    yp = _ffn(yp, w_gu, w_dn, ln_g[0, 1], ln_b[0, 1], layer=0, tm=tm, alpha=alpha)
    ys = _ffn(ys, w_gu, w_dn, ln_g[0, 1], ln_b[0, 1], layer=0, tm=bsz, alpha=alpha)

    dk = gla_w_gate2.shape[2]
    dv = gla_w_out.shape[1]
    dkh, dvh = dk // GLA_H, dv // GLA_H
    gw = jnp.pad(gla_w_in[0], ((0, 0), (0, LANES - GLA_RANK))).astype(BF16)
    gw2 = jnp.pad(gla_w_gate2[0], ((0, LANES - GLA_RANK), (0, 0))).astype(BF16)
    gb2 = gla_b_gate2[0].reshape(1, dk)
    g_out = gla_w_out[0].astype(BF16)
    ng = gla_norm_g[0].reshape(1, dv)

    qi, ki, ks, vb, rs_p, al = _gla_proj_prompt(yp, gw, gw2, gb2, tm=tm, dk=dk, dv=dv)
    on_p, st_p = _gla_rec(qi, ki, ks, vb, al, batch=batch, seq=seq, dkh=dkh, dvh=dvh)
    yp = _gla_proj_ln(on_p, rs_p, ng, g_out, yp, ln_g[1, 0], ln_b[1, 0], tm=tm, alpha=alpha)

    q_s, k_s, v_s, rs_s, la_s = _gla_proj_sample(ys, gw, gw2, gb2, dk=dk, dv=dv)
    colv = lambda a: a.reshape(bsz, GLA_H, dkh, 1)
    on_s, st_s = _gla_step(colv(q_s), colv(k_s), colv(la_s), v_s.reshape(bsz, GLA_H, 1, dvh),
                           state_gla[0], scale=dkh ** -0.5)
    ys = _gla_proj_ln(on_s.reshape(bsz, dv), rs_s, ng, g_out, ys, ln_g[1, 0], ln_b[1, 0], tm=bsz, alpha=alpha)

    yp = _ffn(yp, w_gu, w_dn, ln_g[1, 1], ln_b[1, 1], layer=1, tm=tm, alpha=alpha)
    ys = _ffn(ys, w_gu, w_dn, ln_g[1, 1], ln_b[1, 1], layer=1, tm=bsz, alpha=alpha)

    kv6 = lambda a, rows: a.reshape(1, rows[0], rows[1], 2, NSA_G, NSA_D)
    return (yp.reshape(batch, seq, d), ys.reshape(bsz, 1, d),
            kv6(cmp_p, (batch, seq)), kv6(slc_p, (batch, seq)), kv6(win_p, (batch, WINDOW)),
            jnp.swapaxes(st_p, 2, 3)[None],
            kv6(cmp_s, (bsz, 1)), kv6(slc_s, (bsz, 1)), kv6(win_buf_s, (bsz, WINDOW)),
            st_s[None])
```
